```python
import jax
import jax.numpy as jnp
from jax import lax
import numpy as np

D_MODEL = 2048
BATCH = 8
SEQ = 8192
DEPTH = 4

CHUNK = 64
N_MIXERS = 3
RMS_EPS = 1e-6

GLA_HEADS = 4
GLA_KEY_DIM = D_MODEL // 2
GLA_VAL_DIM = D_MODEL
GLA_HEAD_K = GLA_KEY_DIM // GLA_HEADS
GLA_HEAD_V = GLA_VAL_DIM // GLA_HEADS
GLA_GATE_RANK = 16
GLA_GATE_TAU = 16.0
GLA_IN = 2 * GLA_KEY_DIM + 2 * GLA_VAL_DIM + GLA_GATE_RANK

LRU_WIDTH = D_MODEL
LRU_BLOCKS = LRU_WIDTH // 256
LRU_BLOCK_W = LRU_WIDTH // LRU_BLOCKS
LRU_CONV = 4
LRU_C = 8.0

RET_HEADS = 8
RET_HEAD_K = D_MODEL // RET_HEADS
RET_HEAD_V = 2 * RET_HEAD_K
RET_KEY_DIM = RET_HEADS * RET_HEAD_K
RET_VAL_DIM = RET_HEADS * RET_HEAD_V
RET_IN = 2 * RET_KEY_DIM + 2 * RET_VAL_DIM
ROPE_BASE = 10000.0

FFN_DIM = 3 * D_MODEL
FFN_CONV = 3

N_GLA = (DEPTH + 2) // 3
N_LRU = (DEPTH + 1) // 3
N_RET = DEPTH // 3

kernel_name = "hybrid_gla_rglru_retention_convffn"

F32 = jnp.float32


def rms_norm(x, w):
    xf = x.astype(F32)
    y = xf * lax.rsqrt(jnp.mean(xf * xf, axis=-1, keepdims=True) + RMS_EPS)
    return (y * w.astype(F32)).astype(x.dtype)


def head_rms_norm(o, w):
    B, S, H, d = o.shape
    o = o * lax.rsqrt(jnp.mean(o * o, axis=-1, keepdims=True) + RMS_EPS)
    return o.reshape(B, S, H * d) * w.astype(F32)


def head_layer_norm(o, w):
    B, S, H, d = o.shape
    mu = jnp.mean(o, axis=-1, keepdims=True)
    oc = o - mu
    o = oc * lax.rsqrt(jnp.mean(oc * oc, axis=-1, keepdims=True) + RMS_EPS)
    return o.reshape(B, S, H * d) * w.astype(F32)


def causal_depthwise_conv(x, w, b):
    K = w.shape[0]
    y = lax.conv_general_dilated(
        x, w[:, None, :].astype(x.dtype), window_strides=(1,), padding=((K - 1, 0),),
        dimension_numbers=("NWC", "WIO", "NWC"), feature_group_count=x.shape[-1])
    return y + b.astype(x.dtype)


def to_chunks(t, heads):
    B, S, _ = t.shape
    return t.reshape(B, S // CHUNK, CHUNK, heads, -1).transpose(0, 3, 1, 2, 4)


def from_chunks(t):
    B, H, N, C, d = t.shape
    return t.transpose(0, 2, 3, 1, 4).reshape(B, N * C, H, d)


def chunk_state_readout(q_in, k_in, v, decay):
    B, H, N, C, dk = q_in.shape
    dv = v.shape[-1]

    def step(S, inp):
        qn, kn, vn, dn = inp
        o = jnp.einsum("bhcd,bhde->bhce", qn, S)
        S = dn[..., None] * S + jnp.einsum("bhcd,bhce->bhde", kn, vn)
        return S, o

    xs = (jnp.moveaxis(q_in, 2, 0), jnp.moveaxis(k_in, 2, 0),
          jnp.moveaxis(v, 2, 0), jnp.moveaxis(decay, 2, 0))
    S0 = jnp.zeros((B, H, dk, dv), F32)
    _, o = lax.scan(step, S0, xs)
    return jnp.moveaxis(o, 0, 2)


def rope(x):
    S, d = x.shape[1], x.shape[-1]
    half = d // 2
    inv = ROPE_BASE ** (-jnp.arange(half, dtype=F32) / half)
    ang = jnp.arange(S, dtype=F32)[:, None] * inv[None, :]
    cos = jnp.cos(ang)[None, :, None, :]
    sin = jnp.sin(ang)[None, :, None, :]
    x1, x2 = x[..., :half], x[..., half:]
    return jnp.concatenate([x1 * cos - x2 * sin, x2 * cos + x1 * sin], axis=-1)


def gla_mixer(h, w_in, w_gk, b_gk, norm_w, w_out):
    B, S, _ = h.shape
    proj = h @ w_in
    q, k, v, g, z = jnp.split(
        proj, [GLA_KEY_DIM, 2 * GLA_KEY_DIM, 2 * GLA_KEY_DIM + GLA_VAL_DIM,
               2 * GLA_KEY_DIM + 2 * GLA_VAL_DIM], axis=-1)
    log_a = jax.nn.log_sigmoid((z @ w_gk + b_gk).astype(F32)) / GLA_GATE_TAU
    q = to_chunks(q.astype(F32) * (GLA_HEAD_K ** -0.5), GLA_HEADS)
    k = to_chunks(k.astype(F32), GLA_HEADS)
    v = to_chunks(v.astype(F32), GLA_HEADS)
    b = jnp.cumsum(to_chunks(log_a, GLA_HEADS), axis=3)
    b_last = b[:, :, :, -1:, :]
    eb, enb = jnp.exp(b), jnp.exp(-b)
    q_dec, k_grow = q * eb, k * enb
    q_grow, k_dec = q * enb, k * eb
    idx = jnp.arange(CHUNK)
    lower = idx[:, None] >= idx[None, :]
    scores = jnp.where(lower,
                       jnp.einsum("bhntd,bhnsd->bhnts", q_dec, k_grow),
                       jnp.einsum("bhntd,bhnsd->bhnts", q_grow, k_dec))
    o = jnp.einsum("bhnts,bhnse->bhnte", scores, v)
    o = o + chunk_state_readout(q_dec, k * jnp.exp(b_last - b), v, jnp.exp(b_last[:, :, :, 0, :]))
    o = head_rms_norm(from_chunks(o), norm_w) * jax.nn.silu(g.astype(F32))
    return o.astype(h.dtype) @ w_out


def _lin_rec_combine(c1, c2):
    a1, b1 = c1
    a2, b2 = c2
    return a1 * a2, a2 * b1 + b2


def rglru_mixer(h, w_in, conv_w, conv_b, w_ga, b_ga, w_gx, b_gx, lam, w_out):
    B, S, _ = h.shape
    xb, yb = jnp.split(h @ w_in, 2, axis=-1)
    yb = jax.nn.gelu(yb)
    xb = causal_depthwise_conv(xb, conv_w, conv_b)
    xblk = xb.reshape(B, S, LRU_BLOCKS, LRU_BLOCK_W)
    r = jax.nn.sigmoid(jnp.einsum("bsnc,ncd->bsnd", xblk, w_ga).reshape(B, S, LRU_WIDTH) + b_ga)
    i = jax.nn.sigmoid(jnp.einsum("bsnc,ncd->bsnd", xblk, w_gx).reshape(B, S, LRU_WIDTH) + b_gx)
    log_a = -LRU_C * r.astype(F32) * jax.nn.softplus(-lam.astype(F32))
    a = jnp.exp(log_a)
    u = xb.astype(F32) * i.astype(F32) * jnp.sqrt(-jnp.expm1(2.0 * log_a))
    _, hs = lax.associative_scan(_lin_rec_combine, (a, u), axis=1)
    return (hs * yb.astype(F32)).astype(h.dtype) @ w_out


def retention_mixer(h, w_in, norm_w, w_out):
    B, S, _ = h.shape
    q, k, v, g = jnp.split(h @ w_in, [RET_KEY_DIM, 2 * RET_KEY_DIM, 2 * RET_KEY_DIM + RET_VAL_DIM], axis=-1)
    q = rope(q.astype(F32).reshape(B, S, RET_HEADS, RET_HEAD_K)).reshape(B, S, RET_KEY_DIM)
    k = rope(k.astype(F32).reshape(B, S, RET_HEADS, RET_HEAD_K)).reshape(B, S, RET_KEY_DIM) * (RET_HEAD_K ** -0.5)
    q = to_chunks(q, RET_HEADS)
    k = to_chunks(k, RET_HEADS)
    v = to_chunks(v.astype(F32), RET_HEADS)
    log_g = jnp.log(1.0 - jnp.exp2(-5.0 - jnp.arange(RET_HEADS, dtype=F32)))
    pos = jnp.arange(CHUNK, dtype=F32)
    dist = jnp.abs(pos[:, None] - pos[None, :])
    D = jnp.exp(log_g[:, None, None] * dist)
    scores = jnp.einsum("bhntd,bhnsd->bhnts", q, k) * D[None, :, None]
    o = jnp.einsum("bhnts,bhnse->bhnte", scores, v)
    xi = jnp.exp(log_g[:, None] * (pos + 1.0))[None, :, None, :, None]
    zeta = jnp.exp(log_g[:, None] * (CHUNK - 1.0 - pos))[None, :, None, :, None]
    n_chunks = S // CHUNK
    decay = jnp.broadcast_to(jnp.exp(log_g * CHUNK)[None, :, None, None], (B, RET_HEADS, n_chunks, RET_HEAD_K))
    o = o + chunk_state_readout(q * xi, k * zeta, v, decay)
    o = head_layer_norm(from_chunks(o), norm_w) * jax.nn.silu(g.astype(F32))
    return o.astype(h.dtype) @ w_out


def conv_ffn(h, w_up, conv_w, conv_b, w_down):
    u = causal_depthwise_conv(h @ w_up, conv_w, conv_b)
    gate, val = jnp.split(u, 2, axis=-1)
    return (jax.nn.gelu(gate) * val) @ w_down


def _fwd_setup_inputs(seed: int = 0) -> dict:
    key = jax.random.key(seed)
    ks = iter(jax.random.split(key, 32))

    def nrm(shape, fan_in):
        return jax.random.normal(next(ks), shape, F32) * (fan_in ** -0.5)

    def gain(shape):
        return 1.0 + 0.02 * jax.random.normal(next(ks), shape, F32)

    def bias(shape):
        return 0.01 * jax.random.normal(next(ks), shape, F32)

    x = jax.random.normal(next(ks), (BATCH, SEQ, D_MODEL), F32)
    norm_mix_w = gain((DEPTH, D_MODEL))
    norm_ffn_w = gain((DEPTH, D_MODEL))
    norm_out_w = gain((D_MODEL,))

    gla_w_in = nrm((N_GLA, D_MODEL, GLA_IN), D_MODEL)
    gla_w_gk = nrm((N_GLA, GLA_GATE_RANK, GLA_KEY_DIM), GLA_GATE_RANK)
    gla_b_gk = bias((N_GLA, GLA_KEY_DIM))
    gla_norm_w = gain((N_GLA, GLA_VAL_DIM))
    gla_w_out = nrm((N_GLA, GLA_VAL_DIM, D_MODEL), GLA_VAL_DIM)

    lru_w_in = nrm((N_LRU, D_MODEL, 2 * LRU_WIDTH), D_MODEL)
    lru_conv_w = nrm((N_LRU, LRU_CONV, LRU_WIDTH), LRU_CONV)
    lru_conv_b = bias((N_LRU, LRU_WIDTH))
    lru_w_ga = nrm((N_LRU, LRU_BLOCKS, LRU_BLOCK_W, LRU_BLOCK_W), LRU_BLOCK_W)
    lru_b_ga = bias((N_LRU, LRU_WIDTH))
    lru_w_gx = nrm((N_LRU, LRU_BLOCKS, LRU_BLOCK_W, LRU_BLOCK_W), LRU_BLOCK_W)
    lru_b_gx = bias((N_LRU, LRU_WIDTH))
    a_pow = jax.random.uniform(next(ks), (N_LRU, LRU_WIDTH), F32, minval=0.9, maxval=0.999)
    s = a_pow ** (1.0 / LRU_C)
    lru_lambda = jnp.log(s) - jnp.log1p(-s)
    lru_w_out = nrm((N_LRU, LRU_WIDTH, D_MODEL), LRU_WIDTH)

    ret_w_in = nrm((N_RET, D_MODEL, RET_IN), D_MODEL)
    ret_norm_w = gain((N_RET, RET_VAL_DIM))
    ret_w_out = nrm((N_RET, RET_VAL_DIM, D_MODEL), RET_VAL_DIM)

    ffn_w_up = nrm((DEPTH, D_MODEL, 2 * FFN_DIM), D_MODEL)
    ffn_conv_w = nrm((DEPTH, FFN_CONV, 2 * FFN_DIM), FFN_CONV)
    ffn_conv_b = bias((DEPTH, 2 * FFN_DIM))
    ffn_w_down = nrm((DEPTH, FFN_DIM, D_MODEL), FFN_DIM)

    return {
        "x": x, "norm_mix_w": norm_mix_w, "norm_ffn_w": norm_ffn_w, "norm_out_w": norm_out_w,
        "gla_w_in": gla_w_in, "gla_w_gk": gla_w_gk, "gla_b_gk": gla_b_gk,
        "gla_norm_w": gla_norm_w, "gla_w_out": gla_w_out,
        "lru_w_in": lru_w_in, "lru_conv_w": lru_conv_w, "lru_conv_b": lru_conv_b,
        "lru_w_ga": lru_w_ga, "lru_b_ga": lru_b_ga, "lru_w_gx": lru_w_gx, "lru_b_gx": lru_b_gx,
        "lru_lambda": lru_lambda, "lru_w_out": lru_w_out,
        "ret_w_in": ret_w_in, "ret_norm_w": ret_norm_w, "ret_w_out": ret_w_out,
        "ffn_w_up": ffn_w_up, "ffn_conv_w": ffn_conv_w, "ffn_conv_b": ffn_conv_b,
        "ffn_w_down": ffn_w_down,
    }


def _fwd_reference(x, norm_mix_w, norm_ffn_w, norm_out_w,
              gla_w_in, gla_w_gk, gla_b_gk, gla_norm_w, gla_w_out,
              lru_w_in, lru_conv_w, lru_conv_b, lru_w_ga, lru_b_ga, lru_w_gx, lru_b_gx,
              lru_lambda, lru_w_out,
              ret_w_in, ret_norm_w, ret_w_out,
              ffn_w_up, ffn_conv_w, ffn_conv_b, ffn_w_down):
    for i in range(DEPTH):
        kind, j = i % N_MIXERS, i // N_MIXERS
        h = rms_norm(x, norm_mix_w[i])
        if kind == 0:
            y = gla_mixer(h, gla_w_in[j], gla_w_gk[j], gla_b_gk[j], gla_norm_w[j], gla_w_out[j])
        elif kind == 1:
            y = rglru_mixer(h, lru_w_in[j], lru_conv_w[j], lru_conv_b[j], lru_w_ga[j], lru_b_ga[j],
                            lru_w_gx[j], lru_b_gx[j], lru_lambda[j], lru_w_out[j])
        else:
            y = retention_mixer(h, ret_w_in[j], ret_norm_w[j], ret_w_out[j])
        x = x + y.astype(x.dtype)
        x = x + conv_ffn(rms_norm(x, norm_ffn_w[i]), ffn_w_up[i], ffn_conv_w[i], ffn_conv_b[i],
                         ffn_w_down[i]).astype(x.dtype)
    return rms_norm(x, norm_out_w)


import jax as _jax
import jax.numpy as _jnp

TWIN_FORMAT = 'train_step'
FWD_PARAMS = ['x', 'norm_mix_w', 'norm_ffn_w', 'norm_out_w', 'gla_w_in', 'gla_w_gk', 'gla_b_gk', 'gla_norm_w', 'gla_w_out', 'lru_w_in', 'lru_conv_w', 'lru_conv_b', 'lru_w_ga', 'lru_b_ga', 'lru_w_gx', 'lru_b_gx', 'lru_lambda', 'lru_w_out', 'ret_w_in', 'ret_norm_w', 'ret_w_out', 'ffn_w_up', 'ffn_conv_w', 'ffn_conv_b', 'ffn_w_down']
TWIN_WEIGHTS = ['norm_mix_w', 'norm_ffn_w', 'norm_out_w', 'gla_w_in', 'gla_w_gk', 'gla_b_gk', 'gla_norm_w', 'gla_w_out', 'lru_w_in', 'lru_conv_w', 'lru_conv_b', 'lru_w_ga', 'lru_b_ga', 'lru_w_gx', 'lru_b_gx', 'lru_lambda', 'lru_w_out', 'ret_w_in', 'ret_norm_w', 'ret_w_out', 'ffn_w_up', 'ffn_conv_w', 'ffn_conv_b', 'ffn_w_down']
TWIN_DIFF_INPUT = 'x'
TWIN_INPUTS = ['x', 'norm_mix_w', 'norm_ffn_w', 'norm_out_w', 'gla_w_in', 'gla_w_gk', 'gla_b_gk', 'gla_norm_w', 'gla_w_out', 'lru_w_in', 'lru_conv_w', 'lru_conv_b', 'lru_w_ga', 'lru_b_ga', 'lru_w_gx', 'lru_b_gx', 'lru_lambda', 'lru_w_out', 'ret_w_in', 'ret_norm_w', 'ret_w_out', 'ffn_w_up', 'ffn_conv_w', 'ffn_conv_b', 'ffn_w_down', 'loss_target', 'm_norm_mix_w', 'm_norm_ffn_w', 'm_norm_out_w', 'm_gla_w_in', 'm_gla_w_gk', 'm_gla_b_gk', 'm_gla_norm_w', 'm_gla_w_out', 'm_lru_w_in', 'm_lru_conv_w', 'm_lru_conv_b', 'm_lru_w_ga', 'm_lru_b_ga', 'm_lru_w_gx', 'm_lru_b_gx', 'm_lru_lambda', 'm_lru_w_out', 'm_ret_w_in', 'm_ret_norm_w', 'm_ret_w_out', 'm_ffn_w_up', 'm_ffn_conv_w', 'm_ffn_conv_b', 'm_ffn_w_down', 'v_norm_mix_w', 'v_norm_ffn_w', 'v_norm_out_w', 'v_gla_w_in', 'v_gla_w_gk', 'v_gla_b_gk', 'v_gla_norm_w', 'v_gla_w_out', 'v_lru_w_in', 'v_lru_conv_w', 'v_lru_conv_b', 'v_lru_w_ga', 'v_lru_b_ga', 'v_lru_w_gx', 'v_lru_b_gx', 'v_lru_lambda', 'v_lru_w_out', 'v_ret_w_in', 'v_ret_norm_w', 'v_ret_w_out', 'v_ffn_w_up', 'v_ffn_conv_w', 'v_ffn_conv_b', 'v_ffn_w_down']
TWIN_OUTPUTS = ['loss', 'grad_x', 'grad_norm_mix_w', 'grad_norm_ffn_w', 'grad_norm_out_w', 'grad_gla_w_in', 'grad_gla_w_gk', 'grad_gla_b_gk', 'grad_gla_norm_w', 'grad_gla_w_out', 'grad_lru_w_in', 'grad_lru_conv_w', 'grad_lru_conv_b', 'grad_lru_w_ga', 'grad_lru_b_ga', 'grad_lru_w_gx', 'grad_lru_b_gx', 'grad_lru_lambda', 'grad_lru_w_out', 'grad_ret_w_in', 'grad_ret_norm_w', 'grad_ret_w_out', 'grad_ffn_w_up', 'grad_ffn_conv_w', 'grad_ffn_conv_b', 'grad_ffn_w_down', 'delta_norm_mix_w', 'delta_norm_ffn_w', 'delta_norm_out_w', 'delta_gla_w_in', 'delta_gla_w_gk', 'delta_gla_b_gk', 'delta_gla_norm_w', 'delta_gla_w_out', 'delta_lru_w_in', 'delta_lru_conv_w', 'delta_lru_conv_b', 'delta_lru_w_ga', 'delta_lru_b_ga', 'delta_lru_w_gx', 'delta_lru_b_gx', 'delta_lru_lambda', 'delta_lru_w_out', 'delta_ret_w_in', 'delta_ret_norm_w', 'delta_ret_w_out', 'delta_ffn_w_up', 'delta_ffn_conv_w', 'delta_ffn_conv_b', 'delta_ffn_w_down', 'new_m_norm_mix_w', 'new_m_norm_ffn_w', 'new_m_norm_out_w', 'new_m_gla_w_in', 'new_m_gla_w_gk', 'new_m_gla_b_gk', 'new_m_gla_norm_w', 'new_m_gla_w_out', 'new_m_lru_w_in', 'new_m_lru_conv_w', 'new_m_lru_conv_b', 'new_m_lru_w_ga', 'new_m_lru_b_ga', 'new_m_lru_w_gx', 'new_m_lru_b_gx', 'new_m_lru_lambda', 'new_m_lru_w_out', 'new_m_ret_w_in', 'new_m_ret_norm_w', 'new_m_ret_w_out', 'new_m_ffn_w_up', 'new_m_ffn_conv_w', 'new_m_ffn_conv_b', 'new_m_ffn_w_down', 'new_v_norm_mix_w', 'new_v_norm_ffn_w', 'new_v_norm_out_w', 'new_v_gla_w_in', 'new_v_gla_w_gk', 'new_v_gla_b_gk', 'new_v_gla_norm_w', 'new_v_gla_w_out', 'new_v_lru_w_in', 'new_v_lru_conv_w', 'new_v_lru_conv_b', 'new_v_lru_w_ga', 'new_v_lru_b_ga', 'new_v_lru_w_gx', 'new_v_lru_b_gx', 'new_v_lru_lambda', 'new_v_lru_w_out', 'new_v_ret_w_in', 'new_v_ret_norm_w', 'new_v_ret_w_out', 'new_v_ffn_w_up', 'new_v_ffn_conv_w', 'new_v_ffn_conv_b', 'new_v_ffn_w_down']
TWIN_LEAF_KINDS = {'loss': 'loss', 'grad_x': 'grad_x', 'grad_norm_mix_w': 'grad_w', 'grad_norm_ffn_w': 'grad_w', 'grad_norm_out_w': 'grad_w', 'grad_gla_w_in': 'grad_w', 'grad_gla_w_gk': 'grad_w', 'grad_gla_b_gk': 'grad_w', 'grad_gla_norm_w': 'grad_w', 'grad_gla_w_out': 'grad_w', 'grad_lru_w_in': 'grad_w', 'grad_lru_conv_w': 'grad_w', 'grad_lru_conv_b': 'grad_w', 'grad_lru_w_ga': 'grad_w', 'grad_lru_b_ga': 'grad_w', 'grad_lru_w_gx': 'grad_w', 'grad_lru_b_gx': 'grad_w', 'grad_lru_lambda': 'grad_w', 'grad_lru_w_out': 'grad_w', 'grad_ret_w_in': 'grad_w', 'grad_ret_norm_w': 'grad_w', 'grad_ret_w_out': 'grad_w', 'grad_ffn_w_up': 'grad_w', 'grad_ffn_conv_w': 'grad_w', 'grad_ffn_conv_b': 'grad_w', 'grad_ffn_w_down': 'grad_w', 'delta_norm_mix_w': 'delta_w', 'delta_norm_ffn_w': 'delta_w', 'delta_norm_out_w': 'delta_w', 'delta_gla_w_in': 'delta_w', 'delta_gla_w_gk': 'delta_w', 'delta_gla_b_gk': 'delta_w', 'delta_gla_norm_w': 'delta_w', 'delta_gla_w_out': 'delta_w', 'delta_lru_w_in': 'delta_w', 'delta_lru_conv_w': 'delta_w', 'delta_lru_conv_b': 'delta_w', 'delta_lru_w_ga': 'delta_w', 'delta_lru_b_ga': 'delta_w', 'delta_lru_w_gx': 'delta_w', 'delta_lru_b_gx': 'delta_w', 'delta_lru_lambda': 'delta_w', 'delta_lru_w_out': 'delta_w', 'delta_ret_w_in': 'delta_w', 'delta_ret_norm_w': 'delta_w', 'delta_ret_w_out': 'delta_w', 'delta_ffn_w_up': 'delta_w', 'delta_ffn_conv_w': 'delta_w', 'delta_ffn_conv_b': 'delta_w', 'delta_ffn_w_down': 'delta_w', 'new_m_norm_mix_w': 'new_m', 'new_m_norm_ffn_w': 'new_m', 'new_m_norm_out_w': 'new_m', 'new_m_gla_w_in': 'new_m', 'new_m_gla_w_gk': 'new_m', 'new_m_gla_b_gk': 'new_m', 'new_m_gla_norm_w': 'new_m', 'new_m_gla_w_out': 'new_m', 'new_m_lru_w_in': 'new_m', 'new_m_lru_conv_w': 'new_m', 'new_m_lru_conv_b': 'new_m', 'new_m_lru_w_ga': 'new_m', 'new_m_lru_b_ga': 'new_m', 'new_m_lru_w_gx': 'new_m', 'new_m_lru_b_gx': 'new_m', 'new_m_lru_lambda': 'new_m', 'new_m_lru_w_out': 'new_m', 'new_m_ret_w_in': 'new_m', 'new_m_ret_norm_w': 'new_m', 'new_m_ret_w_out': 'new_m', 'new_m_ffn_w_up': 'new_m', 'new_m_ffn_conv_w': 'new_m', 'new_m_ffn_conv_b': 'new_m', 'new_m_ffn_w_down': 'new_m', 'new_v_norm_mix_w': 'new_v', 'new_v_norm_ffn_w': 'new_v', 'new_v_norm_out_w': 'new_v', 'new_v_gla_w_in': 'new_v', 'new_v_gla_w_gk': 'new_v', 'new_v_gla_b_gk': 'new_v', 'new_v_gla_norm_w': 'new_v', 'new_v_gla_w_out': 'new_v', 'new_v_lru_w_in': 'new_v', 'new_v_lru_conv_w': 'new_v', 'new_v_lru_conv_b': 'new_v', 'new_v_lru_w_ga': 'new_v', 'new_v_lru_b_ga': 'new_v', 'new_v_lru_w_gx': 'new_v', 'new_v_lru_b_gx': 'new_v', 'new_v_lru_lambda': 'new_v', 'new_v_lru_w_out': 'new_v', 'new_v_ret_w_in': 'new_v', 'new_v_ret_norm_w': 'new_v', 'new_v_ret_w_out': 'new_v', 'new_v_ffn_w_up': 'new_v', 'new_v_ffn_conv_w': 'new_v', 'new_v_ffn_conv_b': 'new_v', 'new_v_ffn_w_down': 'new_v'}


def _forward(args):
    return _fwd_reference(*[args[k] for k in FWD_PARAMS])


def _output_shape():
    def fwd():
        inp = _fwd_setup_inputs(0)
        return _fwd_reference(*[inp[k] for k in FWD_PARAMS])
    out = _jax.eval_shape(fwd)
    return out.shape, out.dtype

N_MICROBATCH = 1
ADAM_LR = 0.001
ADAM_B1 = 0.9
ADAM_B2 = 0.999
ADAM_EPS = 1e-08
ADAM_WD = 0.01
ADAM_STEP = 10
PER_EXAMPLE_BATCH_AXIS = {'x': 0, 'loss_target': 0}
SHARED_INPUTS = []
_WEIGHT_DTYPES = {'norm_mix_w': _jnp.float32, 'norm_ffn_w': _jnp.float32, 'norm_out_w': _jnp.float32, 'gla_w_in': _jnp.float32, 'gla_w_gk': _jnp.float32, 'gla_b_gk': _jnp.float32, 'gla_norm_w': _jnp.float32, 'gla_w_out': _jnp.float32, 'lru_w_in': _jnp.float32, 'lru_conv_w': _jnp.float32, 'lru_conv_b': _jnp.float32, 'lru_w_ga': _jnp.float32, 'lru_b_ga': _jnp.float32, 'lru_w_gx': _jnp.float32, 'lru_b_gx': _jnp.float32, 'lru_lambda': _jnp.float32, 'lru_w_out': _jnp.float32, 'ret_w_in': _jnp.float32, 'ret_norm_w': _jnp.float32, 'ret_w_out': _jnp.float32, 'ffn_w_up': _jnp.float32, 'ffn_conv_w': _jnp.float32, 'ffn_conv_b': _jnp.float32, 'ffn_w_down': _jnp.float32}
MOMENT_SCALE = {'norm_mix_w': 1.606980e-01, 'norm_ffn_w': 1.112348e-01, 'norm_out_w': 3.198275e+01, 'gla_w_in': 1.094847e-01, 'gla_w_gk': 1.453573e-02, 'gla_b_gk': 6.245431e-02, 'gla_norm_w': 9.747138e-02, 'gla_w_out': 9.318189e-02, 'lru_w_in': 6.217079e-02, 'lru_conv_w': 6.639719e-02, 'lru_conv_b': 8.194404e-01, 'lru_w_ga': 1.818881e-02, 'lru_b_ga': 1.554961e-02, 'lru_w_gx': 3.210483e-02, 'lru_b_gx': 2.294863e-02, 'lru_lambda': 3.337528e-02, 'lru_w_out': 6.130653e-02, 'ret_w_in': 4.879287e-02, 'ret_norm_w': 4.377868e-02, 'ret_w_out': 5.889852e-02, 'ffn_w_up': 4.487249e-02, 'ffn_conv_w': 4.497812e-02, 'ffn_conv_b': 5.589164e-02, 'ffn_w_down': 7.648353e-02}


def _to_microbatches(a, axis):
    t = _jnp.moveaxis(a, axis, 0)
    t = t.reshape((N_MICROBATCH, t.shape[0] // N_MICROBATCH) + t.shape[1:])
    return _jnp.moveaxis(t, 1, axis + 1)


def setup_inputs(seed: int = 0) -> dict:
    inp = _fwd_setup_inputs(seed)
    key = _jax.random.fold_in(_jax.random.key(seed), 7919)
    shape, _ = _output_shape()
    out = dict(inp)
    out["loss_target"] = _jax.random.normal(_jax.random.fold_in(key, 0), shape, _jnp.float32)
    for i, name in enumerate(TWIN_WEIGHTS):
        w = inp[name].astype(_jnp.float32)
        if MOMENT_SCALE is None:
            s = _jnp.sqrt(_jnp.mean(_jnp.square(w)) + 1e-30)
        else:
            s = MOMENT_SCALE[name]
        km, kv = _jax.random.split(_jax.random.fold_in(key, i + 1))
        out[name] = w
        out["m_" + name] = s * _jax.random.normal(km, w.shape, _jnp.float32)
        out["v_" + name] = (s * s) * _jax.random.uniform(kv, w.shape, _jnp.float32, 0.5, 1.5)
    if N_MICROBATCH > 1:
        for name, axis in PER_EXAMPLE_BATCH_AXIS.items():
            out[name] = _to_microbatches(out[name], axis)
    return {'x': out['x'], 'norm_mix_w': out['norm_mix_w'], 'norm_ffn_w': out['norm_ffn_w'], 'norm_out_w': out['norm_out_w'], 'gla_w_in': out['gla_w_in'], 'gla_w_gk': out['gla_w_gk'], 'gla_b_gk': out['gla_b_gk'], 'gla_norm_w': out['gla_norm_w'], 'gla_w_out': out['gla_w_out'], 'lru_w_in': out['lru_w_in'], 'lru_conv_w': out['lru_conv_w'], 'lru_conv_b': out['lru_conv_b'], 'lru_w_ga': out['lru_w_ga'], 'lru_b_ga': out['lru_b_ga'], 'lru_w_gx': out['lru_w_gx'], 'lru_b_gx': out['lru_b_gx'], 'lru_lambda': out['lru_lambda'], 'lru_w_out': out['lru_w_out'], 'ret_w_in': out['ret_w_in'], 'ret_norm_w': out['ret_norm_w'], 'ret_w_out': out['ret_w_out'], 'ffn_w_up': out['ffn_w_up'], 'ffn_conv_w': out['ffn_conv_w'], 'ffn_conv_b': out['ffn_conv_b'], 'ffn_w_down': out['ffn_w_down'], 'loss_target': out['loss_target'], 'm_norm_mix_w': out['m_norm_mix_w'], 'm_norm_ffn_w': out['m_norm_ffn_w'], 'm_norm_out_w': out['m_norm_out_w'], 'm_gla_w_in': out['m_gla_w_in'], 'm_gla_w_gk': out['m_gla_w_gk'], 'm_gla_b_gk': out['m_gla_b_gk'], 'm_gla_norm_w': out['m_gla_norm_w'], 'm_gla_w_out': out['m_gla_w_out'], 'm_lru_w_in': out['m_lru_w_in'], 'm_lru_conv_w': out['m_lru_conv_w'], 'm_lru_conv_b': out['m_lru_conv_b'], 'm_lru_w_ga': out['m_lru_w_ga'], 'm_lru_b_ga': out['m_lru_b_ga'], 'm_lru_w_gx': out['m_lru_w_gx'], 'm_lru_b_gx': out['m_lru_b_gx'], 'm_lru_lambda': out['m_lru_lambda'], 'm_lru_w_out': out['m_lru_w_out'], 'm_ret_w_in': out['m_ret_w_in'], 'm_ret_norm_w': out['m_ret_norm_w'], 'm_ret_w_out': out['m_ret_w_out'], 'm_ffn_w_up': out['m_ffn_w_up'], 'm_ffn_conv_w': out['m_ffn_conv_w'], 'm_ffn_conv_b': out['m_ffn_conv_b'], 'm_ffn_w_down': out['m_ffn_w_down'], 'v_norm_mix_w': out['v_norm_mix_w'], 'v_norm_ffn_w': out['v_norm_ffn_w'], 'v_norm_out_w': out['v_norm_out_w'], 'v_gla_w_in': out['v_gla_w_in'], 'v_gla_w_gk': out['v_gla_w_gk'], 'v_gla_b_gk': out['v_gla_b_gk'], 'v_gla_norm_w': out['v_gla_norm_w'], 'v_gla_w_out': out['v_gla_w_out'], 'v_lru_w_in': out['v_lru_w_in'], 'v_lru_conv_w': out['v_lru_conv_w'], 'v_lru_conv_b': out['v_lru_conv_b'], 'v_lru_w_ga': out['v_lru_w_ga'], 'v_lru_b_ga': out['v_lru_b_ga'], 'v_lru_w_gx': out['v_lru_w_gx'], 'v_lru_b_gx': out['v_lru_b_gx'], 'v_lru_lambda': out['v_lru_lambda'], 'v_lru_w_out': out['v_lru_w_out'], 'v_ret_w_in': out['v_ret_w_in'], 'v_ret_norm_w': out['v_ret_norm_w'], 'v_ret_w_out': out['v_ret_w_out'], 'v_ffn_w_up': out['v_ffn_w_up'], 'v_ffn_conv_w': out['v_ffn_conv_w'], 'v_ffn_conv_b': out['v_ffn_conv_b'], 'v_ffn_w_down': out['v_ffn_w_down']}


def _loss(weights, diff, rest, loss_target):
    with _jax.named_scope("forward"):
        args = {**rest, TWIN_DIFF_INPUT: diff, **{k: w.astype(_WEIGHT_DTYPES[k]) for k, w in weights.items()}}
        y = _forward(args)
    with _jax.named_scope("loss_head"):
        err = _jnp.square(y.astype(_jnp.float32) - loss_target)
        return 0.5 * _jnp.sum(_jnp.mean(err, axis=-1)) if err.ndim else 0.5 * err


def _adamw(w, g, m, v):
    m = ADAM_B1 * m + (1.0 - ADAM_B1) * g
    v = ADAM_B2 * v + (1.0 - ADAM_B2) * _jnp.square(g)
    m_hat = m / (1.0 - ADAM_B1 ** ADAM_STEP)
    v_hat = v / (1.0 - ADAM_B2 ** ADAM_STEP)
    delta = -ADAM_LR * (m_hat / (_jnp.sqrt(v_hat) + ADAM_EPS) + ADAM_WD * w)
    return delta, m, v


def reference(x, norm_mix_w, norm_ffn_w, norm_out_w, gla_w_in, gla_w_gk, gla_b_gk, gla_norm_w, gla_w_out, lru_w_in, lru_conv_w, lru_conv_b, lru_w_ga, lru_b_ga, lru_w_gx, lru_b_gx, lru_lambda, lru_w_out, ret_w_in, ret_norm_w, ret_w_out, ffn_w_up, ffn_conv_w, ffn_conv_b, ffn_w_down, loss_target, m_norm_mix_w, m_norm_ffn_w, m_norm_out_w, m_gla_w_in, m_gla_w_gk, m_gla_b_gk, m_gla_norm_w, m_gla_w_out, m_lru_w_in, m_lru_conv_w, m_lru_conv_b, m_lru_w_ga, m_lru_b_ga, m_lru_w_gx, m_lru_b_gx, m_lru_lambda, m_lru_w_out, m_ret_w_in, m_ret_norm_w, m_ret_w_out, m_ffn_w_up, m_ffn_conv_w, m_ffn_conv_b, m_ffn_w_down, v_norm_mix_w, v_norm_ffn_w, v_norm_out_w, v_gla_w_in, v_gla_w_gk, v_gla_b_gk, v_gla_norm_w, v_gla_w_out, v_lru_w_in, v_lru_conv_w, v_lru_conv_b, v_lru_w_ga, v_lru_b_ga, v_lru_w_gx, v_lru_b_gx, v_lru_lambda, v_lru_w_out, v_ret_w_in, v_ret_norm_w, v_ret_w_out, v_ffn_w_up, v_ffn_conv_w, v_ffn_conv_b, v_ffn_w_down):
    given = dict(x=x, norm_mix_w=norm_mix_w, norm_ffn_w=norm_ffn_w, norm_out_w=norm_out_w, gla_w_in=gla_w_in, gla_w_gk=gla_w_gk, gla_b_gk=gla_b_gk, gla_norm_w=gla_norm_w, gla_w_out=gla_w_out, lru_w_in=lru_w_in, lru_conv_w=lru_conv_w, lru_conv_b=lru_conv_b, lru_w_ga=lru_w_ga, lru_b_ga=lru_b_ga, lru_w_gx=lru_w_gx, lru_b_gx=lru_b_gx, lru_lambda=lru_lambda, lru_w_out=lru_w_out, ret_w_in=ret_w_in, ret_norm_w=ret_norm_w, ret_w_out=ret_w_out, ffn_w_up=ffn_w_up, ffn_conv_w=ffn_conv_w, ffn_conv_b=ffn_conv_b, ffn_w_down=ffn_w_down, loss_target=loss_target, m_norm_mix_w=m_norm_mix_w, m_norm_ffn_w=m_norm_ffn_w, m_norm_out_w=m_norm_out_w, m_gla_w_in=m_gla_w_in, m_gla_w_gk=m_gla_w_gk, m_gla_b_gk=m_gla_b_gk, m_gla_norm_w=m_gla_norm_w, m_gla_w_out=m_gla_w_out, m_lru_w_in=m_lru_w_in, m_lru_conv_w=m_lru_conv_w, m_lru_conv_b=m_lru_conv_b, m_lru_w_ga=m_lru_w_ga, m_lru_b_ga=m_lru_b_ga, m_lru_w_gx=m_lru_w_gx, m_lru_b_gx=m_lru_b_gx, m_lru_lambda=m_lru_lambda, m_lru_w_out=m_lru_w_out, m_ret_w_in=m_ret_w_in, m_ret_norm_w=m_ret_norm_w, m_ret_w_out=m_ret_w_out, m_ffn_w_up=m_ffn_w_up, m_ffn_conv_w=m_ffn_conv_w, m_ffn_conv_b=m_ffn_conv_b, m_ffn_w_down=m_ffn_w_down, v_norm_mix_w=v_norm_mix_w, v_norm_ffn_w=v_norm_ffn_w, v_norm_out_w=v_norm_out_w, v_gla_w_in=v_gla_w_in, v_gla_w_gk=v_gla_w_gk, v_gla_b_gk=v_gla_b_gk, v_gla_norm_w=v_gla_norm_w, v_gla_w_out=v_gla_w_out, v_lru_w_in=v_lru_w_in, v_lru_conv_w=v_lru_conv_w, v_lru_conv_b=v_lru_conv_b, v_lru_w_ga=v_lru_w_ga, v_lru_b_ga=v_lru_b_ga, v_lru_w_gx=v_lru_w_gx, v_lru_b_gx=v_lru_b_gx, v_lru_lambda=v_lru_lambda, v_lru_w_out=v_lru_w_out, v_ret_w_in=v_ret_w_in, v_ret_norm_w=v_ret_norm_w, v_ret_w_out=v_ret_w_out, v_ffn_w_up=v_ffn_w_up, v_ffn_conv_w=v_ffn_conv_w, v_ffn_conv_b=v_ffn_conv_b, v_ffn_w_down=v_ffn_w_down)
    weights = {n: given[n] for n in TWIN_WEIGHTS}
    shared = {n: given[n] for n in SHARED_INPUTS}
    per_example = {n: given[n] for n in ['x']}
    grad_fn = _jax.value_and_grad(_loss, argnums=(0, 1))

    def one_microbatch(ex, loss_target):
        ex = dict(ex)
        diff = ex.pop(TWIN_DIFF_INPUT)
        return grad_fn(weights, diff, {**shared, **ex}, loss_target)

    if N_MICROBATCH == 1:
        loss, (grad_w, grad_x) = one_microbatch(per_example, given["loss_target"])
    else:
        def body(carry, xs):
            loss_sum, grad_sum = carry
            l_k, (gw_k, gx_k) = one_microbatch(xs[0], xs[1])
            with _jax.named_scope("update"):
                return (loss_sum + l_k, _jax.tree.map(_jnp.add, grad_sum, gw_k)), gx_k

        init = (_jnp.zeros((), _jnp.float32), _jax.tree.map(_jnp.zeros_like, weights))
        (loss, grad_w), grad_x = _jax.lax.scan(body, init, (per_example, given["loss_target"]))
    with _jax.named_scope("update"):
        delta_w, new_m, new_v = {}, {}, {}
        for n in TWIN_WEIGHTS:
            delta_w[n], new_m[n], new_v[n] = _adamw(weights[n], grad_w[n], given["m_" + n], given["v_" + n])
    return (loss, grad_x, *[grad_w[n] for n in TWIN_WEIGHTS], *[delta_w[n] for n in TWIN_WEIGHTS],
            *[new_m[n] for n in TWIN_WEIGHTS], *[new_v[n] for n in TWIN_WEIGHTS])
```

```python
import functools
import math

import jax
import jax.numpy as jnp
from jax import lax
from jax.experimental import pallas as pl
from jax.experimental.pallas import tpu as pltpu

F32 = jnp.float32
BF16 = jnp.bfloat16

N_DEV = 8
CHUNK = 64
RMS_EPS = 1e-6
GLA_HEADS = 4
GLA_GATE_RANK = 16
GLA_GATE_TAU = 16.0
GATE_PAD = 128
LRU_BLOCK_W = 256
LRU_C = 8.0
RET_HEADS = 8
ROPE_BASE = 10000.0
ADAM_LR, ADAM_B1, ADAM_B2, ADAM_EPS, ADAM_WD, ADAM_STEP = 0.001, 0.9, 0.999, 1e-08, 0.01, 10

HALO = 16
VMEM_LIMIT = 56 * 1024 * 1024
ROW_TILE = 256
COMM_LANES = 1024

MESH = pl.DeviceIdType.MESH

WEIGHTS = ['norm_mix_w', 'norm_ffn_w', 'norm_out_w', 'gla_w_in', 'gla_w_gk', 'gla_b_gk', 'gla_norm_w',
           'gla_w_out', 'lru_w_in', 'lru_conv_w', 'lru_conv_b', 'lru_w_ga', 'lru_b_ga', 'lru_w_gx',
           'lru_b_gx', 'lru_lambda', 'lru_w_out', 'ret_w_in', 'ret_norm_w', 'ret_w_out', 'ffn_w_up',
           'ffn_conv_w', 'ffn_conv_b', 'ffn_w_down']
SHARD_AX = {'norm_mix_w': None, 'norm_ffn_w': None, 'norm_out_w': None, 'gla_w_in': 2, 'gla_w_gk': 2,
            'gla_b_gk': 1, 'gla_norm_w': 1, 'gla_w_out': 1, 'lru_w_in': 2, 'lru_conv_w': 2,
            'lru_conv_b': None, 'lru_w_ga': 2, 'lru_b_ga': None, 'lru_w_gx': 2, 'lru_b_gx': None,
            'lru_lambda': None, 'lru_w_out': 1, 'ret_w_in': 2, 'ret_norm_w': 1, 'ret_w_out': 1,
            'ffn_w_up': 2, 'ffn_conv_w': 2, 'ffn_conv_b': None, 'ffn_w_down': 1}
BIG = ['gla_w_in', 'gla_w_out', 'lru_w_in', 'lru_w_ga', 'lru_w_gx', 'lru_w_out', 'ret_w_in', 'ret_w_out',
       'ffn_w_up', 'ffn_w_down']
SMALL_SHARDED = ['gla_w_gk', 'gla_b_gk', 'gla_norm_w', 'lru_conv_w', 'ret_norm_w', 'ffn_conv_w']
REPLICATED = [n for n in WEIGHTS if SHARD_AX[n] is None]


def _pcall(body, **kw):
    return pl.pallas_call(body, **kw)


def _params(sem=None, **kw):
    return pltpu.CompilerParams(dimension_semantics=sem, vmem_limit_bytes=VMEM_LIMIT, **kw)


def _tile(n, pref, align=128):
    if n <= pref:
        return n
    t = (pref // align) * align
    while t >= align:
        if n % t == 0:
            return t
        t -= align
    return n


def _row_iota(shape):
    return lax.broadcasted_iota(jnp.int32, shape, 0)


def _shift_down(x, halo, s):
    t = x.shape[0]
    rolled = pltpu.roll(x, s, 0)
    hb = jnp.tile(pltpu.roll(halo, s, 0), (t // 8, 1))
    return jnp.where(_row_iota(x.shape) < s, hb, rolled)


def _shift_up(x, nxt, s):
    t = x.shape[0]
    rolled = pltpu.roll(x, t - s, 0)
    nb = jnp.tile(pltpu.roll(nxt, 8 - s, 0), (t // 8, 1))
    return jnp.where(_row_iota(x.shape) >= t - s, nb, rolled)


def _cumsum_rows(x):
    t, row, s = x.shape[0], _row_iota(x.shape), 1
    while s < t:
        x = x + jnp.where(row >= s, pltpu.roll(x, s, 0), 0.0)
        s *= 2
    return x


def _rev_cumsum_rows(x):
    t, row, s = x.shape[0], _row_iota(x.shape), 1
    while s < t:
        x = x + jnp.where(row < t - s, pltpu.roll(x, t - s, 0), 0.0)
        s *= 2
    return x


def _scan_fwd(a, u, h0):
    t, row, s = a.shape[0], _row_iota(a.shape), 1
    while s < t:
        keep = row >= s
        u = u + a * jnp.where(keep, pltpu.roll(u, s, 0), 0.0)
        a = a * jnp.where(keep, pltpu.roll(a, s, 0), 1.0)
        s *= 2
    return u + a * h0


def _scan_rev(c, g, d_end):
    t, row, s = c.shape[0], _row_iota(c.shape), 1
    while s < t:
        keep = row < t - s
        g = g + c * jnp.where(keep, pltpu.roll(g, t - s, 0), 0.0)
        c = c * jnp.where(keep, pltpu.roll(c, t - s, 0), 1.0)
        s *= 2
    return g + c * d_end


def _pick_row(x, r):
    return jnp.sum(jnp.where(_row_iota(x.shape) == r, x, 0.0), axis=0, keepdims=True)


def _sigmoid(x):
    return 1.0 / (1.0 + jnp.exp(-x))


def _softplus(x):
    return jnp.maximum(x, 0.0) + jnp.log(1.0 + jnp.exp(-jnp.abs(x)))


_GELU_C = math.sqrt(2.0 / math.pi)


def _gelu_and_grad(x):
    x2 = x * x
    th = jnp.tanh(_GELU_C * (x + 0.044715 * x * x2))
    g = 0.5 * x * (1.0 + th)
    dg = 0.5 * (1.0 + th) + 0.5 * x * (1.0 - th * th) * _GELU_C * (1.0 + 3.0 * 0.044715 * x2)
    return g, dg


def _neg_expm1(y):
    small = -(y * (1.0 + y * (0.5 + y * (1.0 / 6.0 + y * (1.0 / 24.0)))))
    return jnp.where(y > -0.01, small, 1.0 - jnp.exp(y))


def _dot(a, b, dims):
    return lax.dot_general(a.astype(BF16), b.astype(BF16), (dims, ((), ())), preferred_element_type=F32)


def _dot_nn(a, b):
    return _dot(a, b, ((1,), (0,)))


def _dot_nt(a, b):
    return _dot(a, b, ((1,), (1,)))


def _dot_tn(a, b):
    return _dot(a, b, ((0,), (0,)))


def _mm(a, b, mode, out_dtype=F32, res=None, name="mm"):
    if mode == 'nn':
        (m, k), n = a.shape, b.shape[1]
    elif mode == 'nt':
        (m, k), n = a.shape, b.shape[0]
    else:
        (k, m), n = a.shape, b.shape[1]
    tm, tn, tk = _tile(m, 1024), _tile(n, 1024), _tile(k, 512)
    nk = k // tk
    a_spec = pl.BlockSpec((tk, tm), lambda i, j, kk: (kk, i)) if mode == 'tn' else pl.BlockSpec((tm, tk), lambda i, j, kk: (i, kk))
    b_spec = pl.BlockSpec((tn, tk), lambda i, j, kk: (j, kk)) if mode == 'nt' else pl.BlockSpec((tk, tn), lambda i, j, kk: (kk, j))
    o_spec = pl.BlockSpec((tm, tn), lambda i, j, kk: (i, j))
    dot = {'nn': _dot_nn, 'nt': _dot_nt, 'tn': _dot_tn}[mode]

    def body(*refs):
        if res is None:
            a_ref, b_ref, o_ref, acc = refs
        else:
            a_ref, b_ref, r_ref, o_ref, acc = refs
        kk = pl.program_id(2)

        @pl.when(kk == 0)
        def _():
            acc[...] = jnp.zeros_like(acc)

        acc[...] += dot(a_ref[...], b_ref[...])

        @pl.when(kk == nk - 1)
        def _():
            out = acc[...]
            if res is not None:
                out = out + r_ref[...].astype(F32)
            o_ref[...] = out.astype(out_dtype)

    args, specs = [a, b], [a_spec, b_spec]
    if res is not None:
        args.append(res)
        specs.append(o_spec)
    return _pcall(body, name=name, out_shape=jax.ShapeDtypeStruct((m, n), out_dtype),
                  grid=(m // tm, n // tn, nk), in_specs=specs, out_specs=o_spec,
                  scratch_shapes=[pltpu.VMEM((tm, tn), F32)],
                  compiler_params=_params(("parallel", "parallel", "arbitrary")))(*args)


def _rms_fwd(x, w):
    s, d = x.shape
    tr = _tile(s, ROW_TILE, 16)

    def body(x_ref, w_ref, o_ref):
        xv = x_ref[...]
        r = lax.rsqrt(jnp.mean(xv * xv, axis=-1, keepdims=True) + RMS_EPS)
        o_ref[...] = (xv * r * w_ref[...]).astype(BF16)

    return _pcall(body, name="rms_fwd", out_shape=jax.ShapeDtypeStruct((s, d), BF16), grid=(s // tr,),
                  in_specs=[pl.BlockSpec((tr, d), lambda i: (i, 0)), pl.BlockSpec((1, d), lambda i: (0, 0))],
                  out_specs=pl.BlockSpec((tr, d), lambda i: (i, 0)), compiler_params=_params(("parallel",)))(x, w)


def _rms_bwd(x, w, dh, dres):
    s, d = x.shape
    tr = _tile(s, ROW_TILE, 16)

    def body(x_ref, w_ref, dh_ref, dr_ref, dx_ref, dw_ref):
        i = pl.program_id(0)
        xv = x_ref[...]
        r = lax.rsqrt(jnp.mean(xv * xv, axis=-1, keepdims=True) + RMS_EPS)
        xh = xv * r
        dhv = dh_ref[...].astype(F32)
        dxh = dhv * w_ref[...]
        dx_ref[...] = dr_ref[...] + r * (dxh - xh * jnp.mean(dxh * xh, axis=-1, keepdims=True))
        part = jnp.sum(dhv * xh, axis=0, keepdims=True)

        @pl.when(i == 0)
        def _():
            dw_ref[...] = part

        @pl.when(i > 0)
        def _():
            dw_ref[...] += part

    row = pl.BlockSpec((tr, d), lambda i: (i, 0))
    vec = pl.BlockSpec((1, d), lambda i: (0, 0))
    return _pcall(body, name="rms_bwd",
                  out_shape=(jax.ShapeDtypeStruct((s, d), F32), jax.ShapeDtypeStruct((1, d), F32)),
                  grid=(s // tr,), in_specs=[row, vec, row, row], out_specs=(row, vec),
                  compiler_params=_params(("arbitrary",)))(x, w, dh, dres)


def _final_loss(x, w, target):
    s, d = x.shape
    tr = _tile(s, ROW_TILE, 16)

    def body(x_ref, w_ref, t_ref, l_ref, dx_ref, dw_ref):
        i = pl.program_id(0)
        xv = x_ref[...]
        r = lax.rsqrt(jnp.mean(xv * xv, axis=-1, keepdims=True) + RMS_EPS)
        xh = xv * r
        err = xh * w_ref[...] - t_ref[...]
        lpart = 0.5 * jnp.sum(jnp.mean(err * err, axis=-1, keepdims=True), axis=0, keepdims=True)
        dy = err * (1.0 / d)
        dxh = dy * w_ref[...]
        dx_ref[...] = r * (dxh - xh * jnp.mean(dxh * xh, axis=-1, keepdims=True))
        part = jnp.sum(dy * xh, axis=0, keepdims=True)

        @pl.when(i == 0)
        def _():
            dw_ref[...] = part
            l_ref[...] = jnp.broadcast_to(lpart, l_ref.shape)

        @pl.when(i > 0)
        def _():
            dw_ref[...] += part
            l_ref[...] += jnp.broadcast_to(lpart, l_ref.shape)

    row = pl.BlockSpec((tr, d), lambda i: (i, 0))
    vec = pl.BlockSpec((1, d), lambda i: (0, 0))
    return _pcall(body, name="final_loss",
                  out_shape=(jax.ShapeDtypeStruct((8, 128), F32), jax.ShapeDtypeStruct((s, d), F32),
                             jax.ShapeDtypeStruct((1, d), F32)),
                  grid=(s // tr,), in_specs=[row, vec, row],
                  out_specs=(pl.BlockSpec((8, 128), lambda i: (0, 0)), row, vec),
                  compiler_params=_params(("arbitrary",)))(x, w, target)


def _halo_prev(tt, cmap):
    return lambda *g: (jnp.maximum(g[-1] * (tt // HALO) - 1, 0), cmap(*g))


def _ffn_act_fwd(u, cw, cb):
    s, f2 = u.shape
    f = f2 // 2
    tt, tc = _tile(s, ROW_TILE, 16), _tile(f, 512)
    nc = f // tc

    def body(g_ref, gh_ref, v_ref, vh_ref, wg_ref, wv_ref, bg_ref, bv_ref, a_ref):
        t = pl.program_id(1)

        def conv(x_ref, h_ref, w_ref, b_ref):
            x = x_ref[...].astype(F32)
            hal = jnp.where(t > 0, h_ref[...].astype(F32)[8:16], 0.0)
            return (w_ref[2:3, :] * x + w_ref[1:2, :] * _shift_down(x, hal, 1)
                    + w_ref[0:1, :] * _shift_down(x, hal, 2) + b_ref[...])

        gate = conv(g_ref, gh_ref, wg_ref, bg_ref)
        val = conv(v_ref, vh_ref, wv_ref, bv_ref)
        a_ref[...] = (_gelu_and_grad(gate)[0] * val).astype(BF16)

    def main(off):
        return pl.BlockSpec((tt, tc), lambda j, t: (t, j + off))

    def halo(off):
        return pl.BlockSpec((HALO, tc), _halo_prev(tt, lambda j, t: j + off))

    def vec(rows, off):
        return pl.BlockSpec((rows, tc), lambda j, t: (0, j + off))

    return _pcall(body, name="ffn_act_fwd", out_shape=jax.ShapeDtypeStruct((s, f), BF16), grid=(nc, s // tt),
                  in_specs=[main(0), halo(0), main(nc), halo(nc), vec(3, 0), vec(3, nc), vec(1, 0), vec(1, nc)],
                  out_specs=pl.BlockSpec((tt, tc), lambda j, t: (t, j)),
                  compiler_params=_params(("parallel", "parallel")))(u, u, u, u, cw, cw, cb, cb)


def _ffn_act_bwd(u, cw, cb, da):
    s, f2 = u.shape
    f = f2 // 2
    tt, tc = _tile(s, ROW_TILE, 16), _tile(f, 512)
    nc, nt = f // tc, s // tt

    def body(g_ref, gh_ref, v_ref, vh_ref, da_ref, wg_ref, wv_ref, bg_ref, bv_ref,
             dug_ref, duv_ref, dwg_ref, dwv_ref, dbg_ref, dbv_ref, carry):
        t = pl.program_id(1)
        first_tile = t == nt - 1

        @pl.when(t == 0)
        def _():
            carry[...] = jnp.zeros_like(carry)

        def shifted(x_ref, h_ref):
            x = x_ref[...].astype(F32)
            hal = jnp.where(first_tile, 0.0, h_ref[...].astype(F32)[8:16])
            return x, _shift_down(x, hal, 1), _shift_down(x, hal, 2)

        def conv(xs, w_ref, b_ref):
            return w_ref[2:3, :] * xs[0] + w_ref[1:2, :] * xs[1] + w_ref[0:1, :] * xs[2] + b_ref[...]

        xg, xv = shifted(g_ref, gh_ref), shifted(v_ref, vh_ref)
        gate, val = conv(xg, wg_ref, bg_ref), conv(xv, wv_ref, bv_ref)
        dav = da_ref[...].astype(F32)
        gl, dgl = _gelu_and_grad(gate)
        dgate, dval = dav * val * dgl, dav * gl

        def back(k, d, xs, w_ref, du_ref, dw_ref, db_ref):
            nxt = carry[k]
            du_ref[...] = (w_ref[2:3, :] * d + w_ref[1:2, :] * _shift_up(d, nxt, 1)
                           + w_ref[0:1, :] * _shift_up(d, nxt, 2)).astype(BF16)
            carry[k] = d[0:8]
            parts = [jnp.sum(d * xs[2 - r], axis=0, keepdims=True) for r in range(3)]
            bpart = jnp.sum(d, axis=0, keepdims=True)

            @pl.when(t == 0)
            def _():
                for r in range(3):
                    dw_ref[r:r + 1, :] = parts[r]
                db_ref[...] = bpart

            @pl.when(t > 0)
            def _():
                for r in range(3):
                    dw_ref[r:r + 1, :] += parts[r]
                db_ref[...] += bpart

        back(0, dgate, xg, wg_ref, dug_ref, dwg_ref, dbg_ref)
        back(1, dval, xv, wv_ref, duv_ref, dwv_ref, dbv_ref)

    def main(off):
        return pl.BlockSpec((tt, tc), lambda j, t: (nt - 1 - t, j + off))

    def halo(off):
        return pl.BlockSpec((HALO, tc), lambda j, t: (jnp.maximum((nt - 1 - t) * (tt // HALO) - 1, 0), j + off))

    def vec(rows, off):
        return pl.BlockSpec((rows, tc), lambda j, t: (0, j + off))

    dug, duv, dwg, dwv, dbg, dbv = _pcall(
        body, name="ffn_act_bwd",
        out_shape=(jax.ShapeDtypeStruct((s, f), BF16), jax.ShapeDtypeStruct((s, f), BF16),
                   jax.ShapeDtypeStruct((3, f), F32), jax.ShapeDtypeStruct((3, f), F32),
                   jax.ShapeDtypeStruct((1, f), F32), jax.ShapeDtypeStruct((1, f), F32)),
        grid=(nc, nt),
        in_specs=[main(0), halo(0), main(nc), halo(nc), main(0), vec(3, 0), vec(3, nc), vec(1, 0), vec(1, nc)],
        out_specs=(main(0), main(0), vec(3, 0), vec(3, 0), vec(1, 0), vec(1, 0)),
        scratch_shapes=[pltpu.VMEM((2, 8, tc), F32)],
        compiler_params=_params(("parallel", "arbitrary")))(u, u, u, u, da, cw, cw, cb, cb)
    return dug, duv, jnp.concatenate([dwg, dwv], axis=1), jnp.concatenate([dbg, dbv], axis=1)


def _chunk_cols(h, kd, vd, heads):
    dk, dv = kd // heads, vd // heads
    return (slice(h * dk, (h + 1) * dk), slice(kd + h * dk, kd + (h + 1) * dk),
            slice(2 * kd + h * dv, 2 * kd + (h + 1) * dv), slice(2 * kd + vd + h * dv, 2 * kd + vd + (h + 1) * dv))


def _rope(x, cos, sin):
    half = x.shape[1] // 2
    x1, x2 = x[:, :half], x[:, half:]
    return jnp.concatenate([x1 * cos - x2 * sin, x2 * cos + x1 * sin], axis=1)


def _unrope(d, cos, sin):
    half = d.shape[1] // 2
    d1, d2 = d[:, :half], d[:, half:]
    return jnp.concatenate([d1 * cos + d2 * sin, d2 * cos - d1 * sin], axis=1)


def _chunk_inputs(gla, h, heads, kd, vd, rows, proj_ref, aux):
    qc, kc, vc, gc = _chunk_cols(h, kd, vd, heads)
    dk = kd // heads
    q = proj_ref[rows, qc].astype(F32)
    k = proj_ref[rows, kc].astype(F32)
    v = proj_ref[rows, vc]
    g = proj_ref[rows, gc].astype(F32)
    c = {}
    if gla:
        z_ref, wgk_ref, bgk_ref = aux
        c['z'] = z_ref[rows, :]
        c['gk'] = _dot_nn(c['z'], wgk_ref[:, qc]) + bgk_ref[:, qc]
        la = (jnp.minimum(c['gk'], 0.0) - jnp.log(1.0 + jnp.exp(-jnp.abs(c['gk'])))) * (1.0 / GLA_GATE_TAU)
        b = _cumsum_rows(la)
        bl = jnp.sum(la, axis=0, keepdims=True)
        q = q * (dk ** -0.5)
    else:
        cos_ref, sin_ref = aux
        c['cos'], c['sin'] = cos_ref[rows, :], sin_ref[rows, :]
        q = _rope(q, c['cos'], c['sin'])
        k = _rope(k, c['cos'], c['sin']) * (dk ** -0.5)
        lg = math.log(1.0 - 2.0 ** (-5.0 - h))
        b = lg * (_row_iota((CHUNK, 1)).astype(F32) + 1.0)
        bl = jnp.full((1, 1), lg * CHUNK, F32)
    eb, enb = jnp.exp(b), jnp.exp(-b)
    c.update(q=q, k=k, v=v, g=g, b=b, bl=bl, eb=eb, enb=enb, ebl=jnp.exp(bl),
             qd=q * eb, kg=k * enb, qg=q * enb, kd=k * eb, ks=k * jnp.exp(bl - b))
    lower = _row_iota((CHUNK, CHUNK)) >= lax.broadcasted_iota(jnp.int32, (CHUNK, CHUNK), 1)
    c['lower'] = lower
    c['A'] = jnp.where(lower, _dot_nt(c['qd'], c['kg']), _dot_nt(c['qg'], c['kd']))
    return c


def _head_norm(gla, o):
    if not gla:
        o = o - jnp.mean(o, axis=-1, keepdims=True)
    r = lax.rsqrt(jnp.mean(o * o, axis=-1, keepdims=True) + RMS_EPS)
    return o * r, r


def _chunk_fwd(gla, proj, aux_arrays, nw, heads, kd, vd, cps):
    s, pw = proj.shape
    dk, dv = kd // heads, vd // heads
    rt = CHUNK * cps
    nb = s // rt
    n_aux = len(aux_arrays)

    def body(*refs):
        proj_ref, aux, nw_ref = refs[0], refs[1:1 + n_aux], refs[1 + n_aux]
        y_ref, o_ref, st_ref, state = refs[2 + n_aux:]

        @pl.when(pl.program_id(0) == 0)
        def _():
            state[...] = jnp.zeros_like(state)

        def chunk(ci, carry):
            rows = pl.ds(pl.multiple_of(ci * CHUNK, CHUNK), CHUNK)
            for h in range(heads):
                c = _chunk_inputs(gla, h, heads, kd, vd, rows, proj_ref, aux)
                vcols = slice(h * dv, (h + 1) * dv)
                st0 = state[h]
                st_ref[ci, h] = st0.astype(BF16)
                o = _dot_nn(c['A'], c['v']) + _dot_nt(c['qd'], st0)
                state[h] = st0 * c['ebl'] + _dot_tn(c['v'], c['ks'])
                oh, _ = _head_norm(gla, o)
                gv = c['g']
                y_ref[rows, vcols] = (oh * nw_ref[:, vcols] * (gv * _sigmoid(gv))).astype(BF16)
                o_ref[rows, vcols] = o.astype(BF16)
            return carry

        lax.fori_loop(0, cps, chunk, 0)

    row = lambda w: pl.BlockSpec((rt, w), lambda n: (n, 0))
    full = lambda a: pl.BlockSpec(a.shape, lambda n: (0,) * a.ndim)
    aux_specs = [row(a.shape[1]) if a.shape[0] == s else full(a) for a in aux_arrays]
    return _pcall(
        body, name="gla_fwd" if gla else "ret_fwd",
        out_shape=(jax.ShapeDtypeStruct((s, vd), BF16), jax.ShapeDtypeStruct((s, vd), BF16),
                   jax.ShapeDtypeStruct((s // CHUNK, heads, dv, dk), BF16)),
        grid=(nb,), in_specs=[row(pw)] + aux_specs + [full(nw)],
        out_specs=(row(vd), row(vd), pl.BlockSpec((cps, heads, dv, dk), lambda n: (n, 0, 0, 0))),
        scratch_shapes=[pltpu.VMEM((heads, dv, dk), F32)],
        compiler_params=_params(("arbitrary",)))(proj, *aux_arrays, nw)


def _chunk_bwd(gla, proj, aux_arrays, nw, o_st, st, dy, heads, kd, vd, cps):
    s, pw = proj.shape
    dk, dv = kd // heads, vd // heads
    rt = CHUNK * cps
    nb = s // rt
    n_aux = len(aux_arrays)

    def body(*refs):
        proj_ref, aux, nw_ref = refs[0], refs[1:1 + n_aux], refs[1 + n_aux]
        o_ref, st_ref, dy_ref = refs[2 + n_aux:5 + n_aux]
        outs = refs[5 + n_aux:]
        dp_ref, dnw_ref = outs[0], outs[1]
        if gla:
            dz_ref, dwgk_ref, dbgk_ref, dstate = outs[2:]
        else:
            dstate = outs[2]

        @pl.when(pl.program_id(0) == 0)
        def _():
            dstate[...] = jnp.zeros_like(dstate)
            dnw_ref[...] = jnp.zeros_like(dnw_ref)
            if gla:
                dwgk_ref[...] = jnp.zeros_like(dwgk_ref)
                dbgk_ref[...] = jnp.zeros_like(dbgk_ref)

        def chunk(i, carry):
            ci = cps - 1 - i
            rows = pl.ds(pl.multiple_of(ci * CHUNK, CHUNK), CHUNK)
            dz = jnp.zeros((CHUNK, GATE_PAD), F32)
            for h in range(heads):
                c = _chunk_inputs(gla, h, heads, kd, vd, rows, proj_ref, aux)
                qc, kc, vc, gc = _chunk_cols(h, kd, vd, heads)
                vcols = slice(h * dv, (h + 1) * dv)
                o = o_ref[rows, vcols].astype(F32)
                oh, r = _head_norm(gla, o)
                dyv = dy_ref[rows, vcols].astype(F32)
                gv = c['g']
                sg = _sigmoid(gv)
                nwv = nw_ref[:, vcols]
                dp_ref[rows, gc] = (dyv * oh * nwv * (sg * (1.0 + gv * (1.0 - sg)))).astype(BF16)
                dn = dyv * (gv * sg)
                dnw_ref[:, vcols] += jnp.sum(dn * oh, axis=0, keepdims=True)
                doh = dn * nwv
                do = doh - oh * jnp.mean(doh * oh, axis=-1, keepdims=True)
                if not gla:
                    do = do - jnp.mean(doh, axis=-1, keepdims=True)
                do = r * do
                st0 = st_ref[ci, h]
                dst1 = dstate[h]
                v = c['v']
                da = _dot_nt(do, v)
                dal = jnp.where(c['lower'], da, 0.0)
                dau = da - dal
                dp_ref[rows, vc] = (_dot_tn(c['A'], do) + _dot_nt(c['ks'], dst1)).astype(BF16)
                dqd = _dot_nn(dal, c['kg']) + _dot_nn(do, st0)
                dkg = _dot_tn(dal, c['qd'])
                dqg = _dot_nn(dau, c['kd'])
                dkd = _dot_tn(dau, c['qg'])
                dks = _dot_nn(v, dst1)
                dstate[h] = _dot_tn(do, c['qd']) + dst1 * c['ebl']
                dq = dqd * c['eb'] + dqg * c['enb']
                dkk = dkg * c['enb'] + dkd * c['eb'] + dks * jnp.exp(c['bl'] - c['b'])
                if gla:
                    db = dqd * c['qd'] - dkg * c['kg'] - dqg * c['qg'] + dkd * c['kd'] - dks * c['ks']
                    dbl = (jnp.sum(dks * c['ks'], axis=0, keepdims=True)
                           + c['ebl'] * jnp.sum(dst1 * st0.astype(F32), axis=0, keepdims=True))
                    db = db + jnp.where(_row_iota(db.shape) == CHUNK - 1, dbl, 0.0)
                    dgk = _rev_cumsum_rows(db) * (1.0 / GLA_GATE_TAU) / (1.0 + jnp.exp(c['gk']))
                    _, wgk_ref, _ = aux
                    dz = dz + _dot_nt(dgk, wgk_ref[:, qc])
                    dwgk_ref[:, qc] += _dot_tn(c['z'], dgk)
                    dbgk_ref[:, qc] += jnp.sum(dgk, axis=0, keepdims=True)
                    dp_ref[rows, qc] = (dq * (dk ** -0.5)).astype(BF16)
                    dp_ref[rows, kc] = dkk.astype(BF16)
                else:
                    dp_ref[rows, qc] = _unrope(dq, c['cos'], c['sin']).astype(BF16)
                    dp_ref[rows, kc] = (_unrope(dkk, c['cos'], c['sin']) * (dk ** -0.5)).astype(BF16)
            if gla:
                dz_ref[rows, :] = dz
            return carry

        lax.fori_loop(0, cps, chunk, 0)

    row = lambda w: pl.BlockSpec((rt, w), lambda n: (nb - 1 - n, 0))
    full = lambda a: pl.BlockSpec(a.shape, lambda n: (0,) * a.ndim)
    aux_specs = [row(a.shape[1]) if a.shape[0] == s else full(a) for a in aux_arrays]
    out_shape = [jax.ShapeDtypeStruct((s, pw), BF16), jax.ShapeDtypeStruct((1, vd), F32)]
    out_specs = [row(pw), full(nw)]
    if gla:
        wgk, bgk = aux_arrays[1], aux_arrays[2]
        out_shape += [jax.ShapeDtypeStruct((s, GATE_PAD), F32), jax.ShapeDtypeStruct(wgk.shape, F32),
                      jax.ShapeDtypeStruct(bgk.shape, F32)]
        out_specs += [row(GATE_PAD), full(wgk), full(bgk)]
    return _pcall(
        body, name="gla_bwd" if gla else "ret_bwd", out_shape=tuple(out_shape), grid=(nb,),
        in_specs=[row(pw)] + aux_specs + [full(nw), row(vd),
                                          pl.BlockSpec((cps, heads, dv, dk), lambda n: (nb - 1 - n, 0, 0, 0)), row(vd)],
        out_specs=tuple(out_specs), scratch_shapes=[pltpu.VMEM((heads, dv, dk), F32)],
        compiler_params=_params(("arbitrary",)))(proj, *aux_arrays, nw, o_st, st, dy)


def _lru_gates(xc, wga_ref, bga_ref, wgx_ref, bgx_ref, lam_ref):
    r = _sigmoid(_dot_nn(xc, wga_ref[0]) + bga_ref[...])
    i = _sigmoid(_dot_nn(xc, wgx_ref[0]) + bgx_ref[...])
    sp = _softplus(-lam_ref[...])
    la = -LRU_C * r * sp
    return r, i, sp, la, jnp.exp(la), jnp.sqrt(_neg_expm1(2.0 * la))


def _lru_specs(w, nbk, tt, tmap):
    main = lambda off: pl.BlockSpec((tt, LRU_BLOCK_W), lambda n, t: (tmap(t), n + off))
    halo = pl.BlockSpec((HALO, LRU_BLOCK_W), lambda n, t: (jnp.maximum(tmap(t) * (tt // HALO) - 1, 0), n))
    vec = lambda rows: pl.BlockSpec((rows, LRU_BLOCK_W), lambda n, t: (0, n))
    mat = pl.BlockSpec((1, LRU_BLOCK_W, LRU_BLOCK_W), lambda n, t: (n, 0, 0))
    return main, halo, vec, mat


def _lru_fwd(proj, cw, cb, wga, bga, wgx, bgx, lam):
    s, w2 = proj.shape
    w = w2 // 2
    nbk, tt = w // LRU_BLOCK_W, _tile(s, ROW_TILE, 16)
    main, halo, vec, mat = _lru_specs(w, nbk, tt, lambda t: t)

    def body(x_ref, xh_ref, y_ref, cw_ref, cb_ref, wga_ref, bga_ref, wgx_ref, bgx_ref, lam_ref,
             out_ref, hs_ref, hcar):
        t = pl.program_id(1)
        x = x_ref[...]
        hal = jnp.where(t > 0, xh_ref[8:16, :], 0.0)
        xc = (cw_ref[3:4, :] * x + cw_ref[2:3, :] * _shift_down(x, hal, 1) + cw_ref[1:2, :] * _shift_down(x, hal, 2)
              + cw_ref[0:1, :] * _shift_down(x, hal, 3) + cb_ref[...])
        r, i, sp, la, a, mlt = _lru_gates(xc, wga_ref, bga_ref, wgx_ref, bgx_ref, lam_ref)

        @pl.when(t == 0)
        def _():
            hcar[...] = jnp.zeros_like(hcar)

        h = _scan_fwd(a, xc * i * mlt, hcar[...])
        hcar[...] = _pick_row(h, tt - 1)
        hs_ref[...] = h
        out_ref[...] = (h * _gelu_and_grad(y_ref[...])[0]).astype(BF16)

    return _pcall(body, name="lru_fwd",
                  out_shape=(jax.ShapeDtypeStruct((s, w), BF16), jax.ShapeDtypeStruct((s, w), F32)),
                  grid=(nbk, s // tt),
                  in_specs=[main(0), halo, main(nbk), vec(4), vec(1), mat, vec(1), mat, vec(1), vec(1)],
                  out_specs=(main(0), main(0)), scratch_shapes=[pltpu.VMEM((1, LRU_BLOCK_W), F32)],
                  compiler_params=_params(("parallel", "arbitrary")))(proj, proj, proj, cw, cb, wga, bga, wgx, bgx, lam)


def _lru_bwd(proj, hs, dout, cw, cb, wga, bga, wgx, bgx, lam):
    s, w2 = proj.shape
    w = w2 // 2
    nbk, tt = w // LRU_BLOCK_W, _tile(s, ROW_TILE, 16)
    nt = s // tt
    main, halo, vec, mat = _lru_specs(w, nbk, tt, lambda t: nt - 1 - t)

    def body(x_ref, xh_ref, y_ref, hs_ref, hh_ref, do_ref, cw_ref, cb_ref, wga_ref, bga_ref, wgx_ref, bgx_ref,
             lam_ref, dx_ref, dy_ref, dcw_ref, dcb_ref, dbga_ref, dbgx_ref, dlam_ref, dwga_ref, dwgx_ref,
             dhcar, dxcar):
        t = pl.program_id(1)
        first_tile = t == nt - 1

        @pl.when(t == 0)
        def _():
            dhcar[...] = jnp.zeros_like(dhcar)
            dxcar[...] = jnp.zeros_like(dxcar)

        x = x_ref[...]
        hal = jnp.where(first_tile, 0.0, xh_ref[8:16, :])
        xs = [x, _shift_down(x, hal, 1), _shift_down(x, hal, 2), _shift_down(x, hal, 3)]
        xc = cw_ref[3:4, :] * xs[0] + cw_ref[2:3, :] * xs[1] + cw_ref[1:2, :] * xs[2] + cw_ref[0:1, :] * xs[3] + cb_ref[...]
        r, i, sp, la, a, mlt = _lru_gates(xc, wga_ref, bga_ref, wgx_ref, bgx_ref, lam_ref)
        h = hs_ref[...]
        hprev = _shift_down(h, jnp.where(first_tile, 0.0, hh_ref[8:16, :]), 1)
        gl, dgl = _gelu_and_grad(y_ref[...])
        dov = do_ref[...].astype(F32)
        dy_ref[...] = (dov * h * dgl).astype(BF16)
        row = _row_iota(a.shape)
        coef = jnp.where(row == tt - 1, 1.0, pltpu.roll(a, tt - 1, 0))
        dh = _scan_rev(coef, dov * gl, dhcar[...])
        dhcar[...] = _pick_row(a * dh, 0)
        dxc = dh * i * mlt
        di = dh * xc * mlt
        dm = dh * xc * i
        dla = dh * hprev * a - dm * jnp.exp(2.0 * la) / mlt
        dpa = dla * (-LRU_C * sp) * r * (1.0 - r)
        dpx = di * i * (1.0 - i)
        dxc = dxc + _dot_nt(dpa, wga_ref[0]) + _dot_nt(dpx, wgx_ref[0])
        nxt = dxcar[...]
        dx_ref[...] = (cw_ref[3:4, :] * dxc + cw_ref[2:3, :] * _shift_up(dxc, nxt, 1)
                       + cw_ref[1:2, :] * _shift_up(dxc, nxt, 2) + cw_ref[0:1, :] * _shift_up(dxc, nxt, 3)).astype(BF16)
        dxcar[...] = dxc[0:8]
        colsum = lambda v: jnp.sum(v, axis=0, keepdims=True)
        parts = [(dcb_ref, colsum(dxc)), (dbga_ref, colsum(dpa)), (dbgx_ref, colsum(dpx)),
                 (dlam_ref, colsum(dla * LRU_C * r) * _sigmoid(-lam_ref[...]))]
        wparts = [colsum(dxc * xs[3 - k]) for k in range(4)]
        dwa, dwx = _dot_tn(xc, dpa), _dot_tn(xc, dpx)

        @pl.when(t == 0)
        def _():
            for ref, val in parts:
                ref[...] = val
            for k in range(4):
                dcw_ref[k:k + 1, :] = wparts[k]
            dwga_ref[0] = dwa
            dwgx_ref[0] = dwx

        @pl.when(t > 0)
        def _():
            for ref, val in parts:
                ref[...] += val
            for k in range(4):
                dcw_ref[k:k + 1, :] += wparts[k]
            dwga_ref[0] += dwa
            dwgx_ref[0] += dwx

    sd = jax.ShapeDtypeStruct
    return _pcall(
        body, name="lru_bwd",
        out_shape=(sd((s, w), BF16), sd((s, w), BF16), sd((4, w), F32), sd((1, w), F32), sd((1, w), F32),
                   sd((1, w), F32), sd((1, w), F32), sd(wga.shape, F32), sd(wgx.shape, F32)),
        grid=(nbk, nt),
        in_specs=[main(0), halo, main(nbk), main(0), halo, main(0), vec(4), vec(1), mat, vec(1), mat, vec(1), vec(1)],
        out_specs=(main(0), main(0), vec(4), vec(1), vec(1), vec(1), vec(1), mat, mat),
        scratch_shapes=[pltpu.VMEM((1, LRU_BLOCK_W), F32), pltpu.VMEM((8, LRU_BLOCK_W), F32)],
        compiler_params=_params(("parallel", "arbitrary")))(proj, proj, proj, hs, hs, dout, cw, cb, wga, bga, wgx, bgx, lam)


def _adamw(g, w, m, v):
    shape = w.shape
    cols = shape[-1]
    g2, w2, m2, v2 = (a.reshape(-1, cols) for a in (g, w, m, v))
    rows = w2.shape[0]
    tr = _tile(rows, max(8, (1 << 18) // cols // 8 * 8), 8)

    def body(g_ref, w_ref, m_ref, v_ref, d_ref, nm_ref, nv_ref):
        gv = g_ref[...]
        nm = ADAM_B1 * m_ref[...] + (1.0 - ADAM_B1) * gv
        nv = ADAM_B2 * v_ref[...] + (1.0 - ADAM_B2) * (gv * gv)
        m_hat = nm / (1.0 - ADAM_B1 ** ADAM_STEP)
        v_hat = nv / (1.0 - ADAM_B2 ** ADAM_STEP)
        d_ref[...] = -ADAM_LR * (m_hat / (jnp.sqrt(v_hat) + ADAM_EPS) + ADAM_WD * w_ref[...])
        nm_ref[...] = nm
        nv_ref[...] = nv

    blk = pl.BlockSpec((tr, cols), lambda i: (i, 0))
    out = _pcall(body, name="adamw", out_shape=tuple(jax.ShapeDtypeStruct((rows, cols), F32) for _ in range(3)),
                 grid=(rows // tr,), in_specs=[blk] * 4, out_specs=(blk,) * 3,
                 compiler_params=_params(("parallel",)))(g2, w2, m2, v2)
    return tuple(o.reshape(shape) for o in out)


def _sum_blocks(parts, out_dtype):
    p, rows, cols = parts.shape
    tr = _tile(rows, max(16, 1024 // p), 16)

    def body(p_ref, o_ref):
        acc = p_ref[0].astype(F32)
        for k in range(1, p):
            acc = acc + p_ref[k].astype(F32)
        o_ref[...] = acc.astype(out_dtype)

    return _pcall(body, name="sum_blocks", out_shape=jax.ShapeDtypeStruct((rows, cols), out_dtype), grid=(rows // tr,),
                  in_specs=[pl.BlockSpec((p, tr, cols), lambda i: (0, i, 0))],
                  out_specs=pl.BlockSpec((tr, cols), lambda i: (i, 0)), compiler_params=_params(("parallel",)))(parts)


def _mesh_pos():
    return lax.axis_index("x"), lax.axis_index("y"), lax.axis_index("c")


def _all_gather(xs):
    rows, cols = xs.shape

    def body(x_ref, out_ref, send_sems, recv_sems, local_sem):
        x, y, c = _mesh_pos()
        me, sibling = (x, y, c), (x, y, 1 - c)
        chips = [(1 - x, y), (x, 1 - y), (1 - x, 1 - y)]

        def slot(px, py, pc):
            return out_ref.at[4 * px + 2 * py + pc]

        def copy(k, block, to, src=None):
            return pltpu.make_async_remote_copy(
                src_ref=slot(*block) if src is None else src, dst_ref=slot(*block),
                send_sem=send_sems.at[k], recv_sem=recv_sems.at[k], device_id=to, device_id_type=MESH)

        mine = pltpu.make_async_copy(x_ref, slot(*me), local_sem)
        mine.start()
        first = [copy(0, me, sibling, src=x_ref)]
        first += [copy(1 + j, me, (*chip, c), src=x_ref) for j, chip in enumerate(chips)]
        for cp in first:
            cp.start()
        passed = [copy(4 + j, (*chip, c), sibling) for j, chip in enumerate(chips)]
        for j, chip in enumerate(chips):
            copy(1 + j, (*chip, c), me).wait_recv()
            passed[j].start()
        copy(0, sibling, me).wait_recv()
        for j, chip in enumerate(chips):
            copy(4 + j, (*chip, 1 - c), me).wait_recv()
        for cp in first + passed:
            cp.wait_send()
        mine.wait()

    return _pcall(body, name="all_gather", out_shape=jax.ShapeDtypeStruct((N_DEV, rows, cols), xs.dtype),
                  in_specs=[pl.BlockSpec(memory_space=pl.ANY)], out_specs=pl.BlockSpec(memory_space=pl.ANY),
                  scratch_shapes=[pltpu.SemaphoreType.DMA((7,)), pltpu.SemaphoreType.DMA((7,)), pltpu.SemaphoreType.DMA],
                  compiler_params=pltpu.CompilerParams(has_side_effects=True))(xs)


def _sibling_exchange(xs):
    def body(x_ref, out_ref, send_sem, recv_sem):
        x, y, c = _mesh_pos()
        cp = pltpu.make_async_remote_copy(src_ref=x_ref, dst_ref=out_ref, send_sem=send_sem, recv_sem=recv_sem,
                                          device_id=(x, y, 1 - c), device_id_type=MESH)
        cp.start()
        cp.wait()

    return _pcall(body, name="sibling_exchange", out_shape=jax.ShapeDtypeStruct(xs.shape, xs.dtype),
                  in_specs=[pl.BlockSpec(memory_space=pl.ANY)], out_specs=pl.BlockSpec(memory_space=pl.ANY),
                  scratch_shapes=[pltpu.SemaphoreType.DMA, pltpu.SemaphoreType.DMA],
                  compiler_params=pltpu.CompilerParams(has_side_effects=True))(xs)


def _chip_exchange(xs):
    def body(x_ref, out_ref, send_sems, recv_sems):
        x, y, c = _mesh_pos()
        chips = [(1 - x, y), (x, 1 - y), (1 - x, 1 - y)]
        cps = [pltpu.make_async_remote_copy(src_ref=x_ref.at[2 * cx + cy], dst_ref=out_ref.at[j],
                                            send_sem=send_sems.at[j], recv_sem=recv_sems.at[j],
                                            device_id=(cx, cy, c), device_id_type=MESH)
               for j, (cx, cy) in enumerate(chips)]
        for cp in cps:
            cp.start()
        for cp in cps:
            cp.wait()

    return _pcall(body, name="chip_exchange", out_shape=jax.ShapeDtypeStruct((3,) + xs.shape[1:], xs.dtype),
                  in_specs=[pl.BlockSpec(memory_space=pl.ANY)], out_specs=pl.BlockSpec(memory_space=pl.ANY),
                  scratch_shapes=[pltpu.SemaphoreType.DMA((3,)), pltpu.SemaphoreType.DMA((3,))],
                  compiler_params=pltpu.CompilerParams(has_side_effects=True))(xs)


def _pack(arrays, dtype):
    lead = arrays[0].shape[:-1]
    flat = jnp.concatenate([a.astype(dtype) for a in arrays], axis=-1)
    n = flat.shape[-1]
    unit = 16 * COMM_LANES
    pad = (-n) % unit
    flat = jnp.pad(flat, [(0, 0)] * len(lead) + [(0, pad)])
    return flat.reshape(lead + ((n + pad) // COMM_LANES, COMM_LANES))


def _unpack(packed, shapes):
    lead = packed.shape[:-2]
    flat = packed.reshape(lead + (-1,))
    out, off = [], 0
    for shp in shapes:
        n = math.prod(shp)
        out.append(flat[..., off:off + n].reshape(lead + tuple(shp)))
        off += n
    return out


def _unshard(g, ax):
    shp = list(g.shape[1:])
    shp[ax] *= N_DEV
    return jnp.moveaxis(g, 0, ax).reshape(shp)


def _split(full, ax):
    shp = list(full.shape)
    r = full.reshape(shp[:ax] + [N_DEV, shp[ax] // N_DEV] + shp[ax + 1:])
    return jnp.moveaxis(r, ax, 0)


def _rope_tables(s, dk):
    half = dk // 2
    inv = ROPE_BASE ** (-jnp.arange(half, dtype=F32) / half)
    ang = jnp.arange(s, dtype=F32)[:, None] * inv[None, :]
    return jnp.cos(ang), jnp.sin(ang)


def _local_step(x, target, p):
    s, d = x.shape
    depth = p['norm_mix_w'].shape[0]
    gla_kd, gla_vd = d // 2, d
    ret_kd, ret_vd = d, 2 * d
    cos, sin = _rope_tables(s, ret_kd // RET_HEADS)
    row = lambda v: v.reshape(1, -1)
    saved = []

    def gla_weights(j):
        w_in = p['gla_w_in'][j]
        w_main, w_z = w_in[:, :2 * gla_kd + 2 * gla_vd], w_in[:, 2 * gla_kd + 2 * gla_vd:]
        w_z = jnp.pad(w_z, ((0, 0), (0, GATE_PAD - GLA_GATE_RANK)))
        w_gk = jnp.pad(p['gla_w_gk'][j], ((0, GATE_PAD - GLA_GATE_RANK), (0, 0)))
        return w_main, w_z, w_gk, row(p['gla_b_gk'][j]), row(p['gla_norm_w'][j]), p['gla_w_out'][j]

    def lru_weights(j):
        return (p['lru_w_in'][j], p['lru_conv_w'][j], row(p['lru_conv_b'][j]), p['lru_w_ga'][j], row(p['lru_b_ga'][j]),
                p['lru_w_gx'][j], row(p['lru_b_gx'][j]), row(p['lru_lambda'][j]), p['lru_w_out'][j])

    for i in range(depth):
        kind, j = i % 3, i // 3
        h = _rms_fwd(x, row(p['norm_mix_w'][i]))
        if kind == 0:
            w_main, w_z, w_gk, b_gk, nw, w_out = gla_weights(j)
            proj = _mm(h, w_main, 'nn', BF16, name="gla_in")
            z = _mm(h, w_z, 'nn', F32, name="gla_z")
            y, o_st, st = _chunk_fwd(True, proj, (z, w_gk, b_gk), nw, GLA_HEADS, gla_kd, gla_vd, 4)
            mix = (proj, z, o_st, st, y)
        elif kind == 1:
            w_in, cw, cb, wga, bga, wgx, bgx, lam, w_out = lru_weights(j)
            proj = _mm(h, w_in, 'nn', F32, name="lru_in")
            y, hs = _lru_fwd(proj, cw, cb, wga, bga, wgx, bgx, lam)
            mix = (proj, hs, y)
        else:
            nw, w_out = row(p['ret_norm_w'][j]), p['ret_w_out'][j]
            proj = _mm(h, p['ret_w_in'][j], 'nn', BF16, name="ret_in")
            y, o_st, st = _chunk_fwd(False, proj, (cos, sin), nw, RET_HEADS, ret_kd, ret_vd, 2)
            mix = (proj, o_st, st, y)
        x_mid = _mm(y, w_out, 'nn', F32, res=x, name="mix_out")
        h2 = _rms_fwd(x_mid, row(p['norm_ffn_w'][i]))
        u = _mm(h2, p['ffn_w_up'][i], 'nn', BF16, name="ffn_up")
        act = _ffn_act_fwd(u, p['ffn_conv_w'][i], row(p['ffn_conv_b'][i]))
        x_out = _mm(act, p['ffn_w_down'][i], 'nn', F32, res=x_mid, name="ffn_down")
        saved.append((x, h, mix, x_mid, h2, u, act))
        x = x_out

    loss, dx, dw = _final_loss(x, row(p['norm_out_w']), target)
    g = {n: [None] * v.shape[0] for n, v in p.items() if n != 'norm_out_w'}
    g['norm_out_w'] = dw.reshape(-1)

    for i in reversed(range(depth)):
        kind, j = i % 3, i // 3
        x_in, h, mix, x_mid, h2, u, act = saved[i]
        g['ffn_w_down'][i] = _mm(act, dx, 'tn', F32, name="ffn_down_dw")
        dact = _mm(dx, p['ffn_w_down'][i], 'nt', BF16, name="ffn_down_dx")
        dug, duv, dcw, dcb = _ffn_act_bwd(u, p['ffn_conv_w'][i], row(p['ffn_conv_b'][i]), dact)
        g['ffn_conv_w'][i], g['ffn_conv_b'][i] = dcw, dcb.reshape(-1)
        f = dug.shape[1]
        w_up = p['ffn_w_up'][i]
        g['ffn_w_up'][i] = jnp.concatenate([_mm(h2, dug, 'tn', F32, name="ffn_up_dw"),
                                            _mm(h2, duv, 'tn', F32, name="ffn_up_dw")], axis=1)
        dh2 = _mm(dug, w_up[:, :f], 'nt', F32, name="ffn_up_dx")
        dh2 = _mm(duv, w_up[:, f:], 'nt', BF16, res=dh2, name="ffn_up_dx")
        dx, dnw = _rms_bwd(x_mid, row(p['norm_ffn_w'][i]), dh2, dx)
        g['norm_ffn_w'][i] = dnw.reshape(-1)
        if kind == 0:
            w_main, w_z, w_gk, b_gk, nw, w_out = gla_weights(j)
            proj, z, o_st, st, y = mix
            g['gla_w_out'][j] = _mm(y, dx, 'tn', F32, name="mix_out_dw")
            dy = _mm(dx, w_out, 'nt', BF16, name="mix_out_dx")
            dproj, dnw, dz, dwgk, dbgk = _chunk_bwd(True, proj, (z, w_gk, b_gk), nw, o_st, st, dy,
                                                    GLA_HEADS, gla_kd, gla_vd, 4)
            g['gla_norm_w'][j], g['gla_b_gk'][j] = dnw.reshape(-1), dbgk.reshape(-1)
            g['gla_w_gk'][j] = dwgk[:GLA_GATE_RANK]
            dw_main = _mm(h, dproj, 'tn', F32, name="gla_in_dw")
            dw_z = _mm(h, dz, 'tn', F32, name="gla_z_dw")
            g['gla_w_in'][j] = jnp.concatenate([dw_main, dw_z[:, :GLA_GATE_RANK]], axis=1)
            dh = _mm(dproj, w_main, 'nt', F32, name="gla_in_dx")
            dh = _mm(dz, w_z, 'nt', BF16, res=dh, name="gla_z_dx")
        elif kind == 1:
            w_in, cw, cb, wga, bga, wgx, bgx, lam, w_out = lru_weights(j)
            proj, hs, y = mix
            g['lru_w_out'][j] = _mm(y, dx, 'tn', F32, name="mix_out_dw")
            dy = _mm(dx, w_out, 'nt', BF16, name="mix_out_dx")
            dxb, dyb, dcw, dcb, dbga, dbgx, dlam, dwga, dwgx = _lru_bwd(proj, hs, dy, cw, cb, wga, bga, wgx, bgx, lam)
            g['lru_conv_w'][j], g['lru_conv_b'][j] = dcw, dcb.reshape(-1)
            g['lru_b_ga'][j], g['lru_b_gx'][j], g['lru_lambda'][j] = dbga.reshape(-1), dbgx.reshape(-1), dlam.reshape(-1)
            g['lru_w_ga'][j], g['lru_w_gx'][j] = dwga, dwgx
            dproj = jnp.concatenate([dxb, dyb], axis=1)
            g['lru_w_in'][j] = _mm(h, dproj, 'tn', F32, name="lru_in_dw")
            dh = _mm(dproj, w_in, 'nt', BF16, name="lru_in_dx")
        else:
            nw, w_out = row(p['ret_norm_w'][j]), p['ret_w_out'][j]
            proj, o_st, st, y = mix
            g['ret_w_out'][j] = _mm(y, dx, 'tn', F32, name="mix_out_dw")
            dy = _mm(dx, w_out, 'nt', BF16, name="mix_out_dx")
            dproj, dnw = _chunk_bwd(False, proj, (cos, sin), nw, o_st, st, dy, RET_HEADS, ret_kd, ret_vd, 2)
            g['ret_norm_w'][j] = dnw.reshape(-1)
            g['ret_w_in'][j] = _mm(h, dproj, 'tn', F32, name="ret_in_dw")
            dh = _mm(dproj, p['ret_w_in'][j], 'nt', BF16, name="ret_in_dx")
        dx, dnw = _rms_bwd(x_in, row(p['norm_mix_w'][i]), dh, dx)
        g['norm_mix_w'][i] = dnw.reshape(-1)

    grads = {n: (v if n == 'norm_out_w' else jnp.stack(v)) for n, v in g.items()}
    return loss[0, 0], dx, grads


def kernel(x, norm_mix_w, norm_ffn_w, norm_out_w, gla_w_in, gla_w_gk, gla_b_gk, gla_norm_w, gla_w_out, lru_w_in, lru_conv_w, lru_conv_b, lru_w_ga, lru_b_ga, lru_w_gx, lru_b_gx, lru_lambda, lru_w_out, ret_w_in, ret_norm_w, ret_w_out, ffn_w_up, ffn_conv_w, ffn_conv_b, ffn_w_down, loss_target, m_norm_mix_w, m_norm_ffn_w, m_norm_out_w, m_gla_w_in, m_gla_w_gk, m_gla_b_gk, m_gla_norm_w, m_gla_w_out, m_lru_w_in, m_lru_conv_w, m_lru_conv_b, m_lru_w_ga, m_lru_b_ga, m_lru_w_gx, m_lru_b_gx, m_lru_lambda, m_lru_w_out, m_ret_w_in, m_ret_norm_w, m_ret_w_out, m_ffn_w_up, m_ffn_conv_w, m_ffn_conv_b, m_ffn_w_down, v_norm_mix_w, v_norm_ffn_w, v_norm_out_w, v_gla_w_in, v_gla_w_gk, v_gla_b_gk, v_gla_norm_w, v_gla_w_out, v_lru_w_in, v_lru_conv_w, v_lru_conv_b, v_lru_w_ga, v_lru_b_ga, v_lru_w_gx, v_lru_b_gx, v_lru_lambda, v_lru_w_out, v_ret_w_in, v_ret_norm_w, v_ret_w_out, v_ffn_w_up, v_ffn_conv_w, v_ffn_conv_b, v_ffn_w_down):
    given = dict(locals())
    w = {n: given[n] for n in WEIGHTS}
    me_x, me_y, me_c = _mesh_pos()
    me = 4 * me_x + 2 * me_y + me_c

    def gather(names, dtype):
        got = _all_gather(_pack([w[n].reshape(-1) for n in names], dtype))
        return {n: _unshard(blk, SHARD_AX[n]) for n, blk in zip(names, _unpack(got, [w[n].shape for n in names]))}

    p = {n: w[n] for n in REPLICATED}
    p.update(gather(BIG, BF16))
    p.update(gather(SMALL_SHARDED, F32))

    loss, grad_x, grads = _local_step(x[0], loss_target[0], p)
    loss = lax.psum(loss, ("x", "y", "c"))

    small = REPLICATED + SMALL_SHARDED
    parts = _all_gather(_pack([grads[n].reshape(-1) for n in small], F32))
    summed = _unpack(_sum_blocks(parts, F32), [grads[n].shape for n in small])
    gw = {}
    for n, gs in zip(small, summed):
        if SHARD_AX[n] is None:
            gw[n] = gs
        else:
            gw[n] = lax.dynamic_index_in_dim(_split(gs, SHARD_AX[n]), me, 0, keepdims=False)

    blocks = _pack([_split(grads[n], SHARD_AX[n]).reshape(N_DEV, -1) for n in BIG], BF16)
    by_core = blocks.reshape((4, 2) + blocks.shape[1:])
    keep = lax.dynamic_index_in_dim(by_core, me_c, 1, keepdims=False)
    give = lax.dynamic_index_in_dim(by_core, 1 - me_c, 1, keepdims=False)
    got = _sibling_exchange(give)
    chip_sum = jnp.stack([_sum_blocks(jnp.stack([keep[k], got[k]]), BF16) for k in range(4)])
    others = _chip_exchange(chip_sum)
    mine = lax.dynamic_index_in_dim(chip_sum, 2 * me_x + me_y, 0, keepdims=True)
    total = _sum_blocks(jnp.concatenate([mine, others], axis=0), F32)
    for n, gs in zip(BIG, _unpack(total, [w[n].shape for n in BIG])):
        gw[n] = gs

    delta, new_m, new_v = {}, {}, {}
    for n in WEIGHTS:
        delta[n], new_m[n], new_v[n] = _adamw(gw[n], w[n], given["m_" + n], given["v_" + n])
    return (loss, grad_x[None], *[gw[n] for n in WEIGHTS], *[delta[n] for n in WEIGHTS],
            *[new_m[n] for n in WEIGHTS], *[new_v[n] for n in WEIGHTS])
```

```python
import collections
import math

import jax
import jax.numpy as jnp
from jax import lax
from jax.experimental import pallas as pl
from jax.experimental.pallas import tpu as pltpu

F32 = jnp.float32
BF16 = jnp.bfloat16

N_DEV = 8
CHUNK = 64
RMS_EPS = 1e-6
GLA_HEADS = 4
GLA_GATE_RANK = 16
GLA_GATE_TAU = 16.0
GATE_PAD = 128
LRU_BLOCK_W = 256
LRU_C = 8.0
RET_HEADS = 8
ROPE_BASE = 10000.0
ADAM_LR, ADAM_B1, ADAM_B2, ADAM_EPS, ADAM_WD, ADAM_STEP = 0.001, 0.9, 0.999, 1e-08, 0.01, 10

HALO = 16
VMEM_LIMIT = 56 * 1024 * 1024
ROW_TILE = 256
COMM_LANES = 1024

MESH = pl.DeviceIdType.MESH

WEIGHTS = ['norm_mix_w', 'norm_ffn_w', 'norm_out_w', 'gla_w_in', 'gla_w_gk', 'gla_b_gk', 'gla_norm_w',
           'gla_w_out', 'lru_w_in', 'lru_conv_w', 'lru_conv_b', 'lru_w_ga', 'lru_b_ga', 'lru_w_gx',
           'lru_b_gx', 'lru_lambda', 'lru_w_out', 'ret_w_in', 'ret_norm_w', 'ret_w_out', 'ffn_w_up',
           'ffn_conv_w', 'ffn_conv_b', 'ffn_w_down']
SHARD_AX = {'norm_mix_w': None, 'norm_ffn_w': None, 'norm_out_w': None, 'gla_w_in': 2, 'gla_w_gk': 2,
            'gla_b_gk': 1, 'gla_norm_w': 1, 'gla_w_out': 1, 'lru_w_in': 2, 'lru_conv_w': 2,
            'lru_conv_b': None, 'lru_w_ga': 2, 'lru_b_ga': None, 'lru_w_gx': 2, 'lru_b_gx': None,
            'lru_lambda': None, 'lru_w_out': 1, 'ret_w_in': 2, 'ret_norm_w': 1, 'ret_w_out': 1,
            'ffn_w_up': 2, 'ffn_conv_w': 2, 'ffn_conv_b': None, 'ffn_w_down': 1}
BIG = ['gla_w_in', 'gla_w_out', 'lru_w_in', 'lru_w_ga', 'lru_w_gx', 'lru_w_out', 'ret_w_in', 'ret_w_out',
       'ffn_w_up', 'ffn_w_down']
GATHERED = ['gla_w_out', 'lru_w_in', 'lru_w_out', 'ret_w_in', 'ret_w_out', 'ffn_w_up', 'ffn_w_down']
SMALL_SHARDED = ['gla_w_gk', 'gla_b_gk', 'gla_norm_w', 'lru_conv_w', 'ret_norm_w', 'ffn_conv_w']
REPLICATED = [n for n in WEIGHTS if SHARD_AX[n] is None]


def _pcall(body, **kw):
    return pl.pallas_call(body, **kw)


def _params(sem=None, **kw):
    return pltpu.CompilerParams(dimension_semantics=sem, vmem_limit_bytes=VMEM_LIMIT, **kw)


def _tile(n, pref, align=128):
    if n <= pref:
        return n
    t = (pref // align) * align
    while t >= align:
        if n % t == 0:
            return t
        t -= align
    return n


def _row_iota(shape):
    return lax.broadcasted_iota(jnp.int32, shape, 0)


def _shift_down(x, halo, s):
    t, c = x.shape
    r = pltpu.roll(x.reshape(t // 8, 8, c), s, 1)
    prev = jnp.concatenate([pltpu.roll(halo, s, 0)[None], r[:-1]], axis=0)
    sub = lax.broadcasted_iota(jnp.int32, r.shape, 1)
    return jnp.where(sub < s, prev, r).reshape(t, c)


def _shift_up(x, nxt, s):
    t, c = x.shape
    r = pltpu.roll(x.reshape(t // 8, 8, c), 8 - s, 1)
    follow = jnp.concatenate([r[1:], pltpu.roll(nxt, 8 - s, 0)[None]], axis=0)
    sub = lax.broadcasted_iota(jnp.int32, r.shape, 1)
    return jnp.where(sub >= 8 - s, follow, r).reshape(t, c)


def _cumsum_rows(x):
    t, row, s = x.shape[0], _row_iota(x.shape), 1
    while s < t:
        x = x + jnp.where(row >= s, pltpu.roll(x, s, 0), 0.0)
        s *= 2
    return x


def _rev_cumsum_rows(x):
    t, row, s = x.shape[0], _row_iota(x.shape), 1
    while s < t:
        x = x + jnp.where(row < t - s, pltpu.roll(x, t - s, 0), 0.0)
        s *= 2
    return x


def _scan_fwd(a, u, h0):
    t, row, s = a.shape[0], _row_iota(a.shape), 1
    while s < t:
        keep = row >= s
        u = u + a * jnp.where(keep, pltpu.roll(u, s, 0), 0.0)
        a = a * jnp.where(keep, pltpu.roll(a, s, 0), 1.0)
        s *= 2
    return u + a * h0


def _scan_rev(c, g, d_end):
    t, row, s = c.shape[0], _row_iota(c.shape), 1
    while s < t:
        keep = row < t - s
        g = g + c * jnp.where(keep, pltpu.roll(g, t - s, 0), 0.0)
        c = c * jnp.where(keep, pltpu.roll(c, t - s, 0), 1.0)
        s *= 2
    return g + c * d_end


def _pick_row(x, r):
    return jnp.sum(jnp.where(_row_iota(x.shape) == r, x, 0.0), axis=0, keepdims=True)


def _sigmoid(x):
    return 1.0 / (1.0 + jnp.exp(-x))


def _softplus(x):
    return jnp.maximum(x, 0.0) + jnp.log(1.0 + jnp.exp(-jnp.abs(x)))


_GELU_C = math.sqrt(2.0 / math.pi)


def _gelu_and_grad(x):
    x2 = x * x
    th = jnp.tanh(_GELU_C * (x + 0.044715 * x * x2))
    g = 0.5 * x * (1.0 + th)
    dg = 0.5 * (1.0 + th) + 0.5 * x * (1.0 - th * th) * _GELU_C * (1.0 + 3.0 * 0.044715 * x2)
    return g, dg


def _neg_expm1(y):
    small = -(y * (1.0 + y * (0.5 + y * (1.0 / 6.0 + y * (1.0 / 24.0)))))
    return jnp.where(y > -0.01, small, 1.0 - jnp.exp(y))


def _dot(a, b, dims):
    return lax.dot_general(a.astype(BF16), b.astype(BF16), (dims, ((), ())), preferred_element_type=F32)


def _dot_nn(a, b):
    return _dot(a, b, ((1,), (0,)))


def _dot_nt(a, b):
    return _dot(a, b, ((1,), (1,)))


def _dot_tn(a, b):
    return _dot(a, b, ((0,), (0,)))


_View = collections.namedtuple("_View", "arr kind layer dev0 ndev")


def _view_shape(v):
    r, c = v.arr.shape[2:]
    return (r, v.ndev * c) if v.kind == 'col' else (N_DEV * r, c)


def _view_spec(v, tr, tc, rc_of):
    r, c = v.arr.shape[2:]
    if v.kind == 'col':
        per = c // tc

        def imap(*g):
            ri, ci = rc_of(*g)
            return (v.dev0 + ci // per, v.layer, ri, ci % per)
    else:
        per = r // tr

        def imap(*g):
            ri, ci = rc_of(*g)
            return (ri // per, v.layer, ri % per, ci)
    return pl.BlockSpec((None, None, tr, tc), imap)


def _mm(a, b, mode, out_dtype=F32, res=None, name="mm", out=None):
    bshape = _view_shape(b) if isinstance(b, _View) else b.shape
    if mode == 'nn':
        (m, k), n = a.shape, bshape[1]
    elif mode == 'nt':
        (m, k), n = a.shape, bshape[0]
    else:
        (k, m), n = a.shape, bshape[1]
    um, un, uk = m, n, k
    if isinstance(b, _View):
        br, bc = b.arr.shape[2:]
        if b.kind == 'col':
            un, uk = (un, bc) if mode == 'nt' else (bc, uk)
        else:
            un, uk = (br, uk) if mode == 'nt' else (un, br)
    if out is not None:
        orr, occ = out.arr.shape[2:]
        um, un = (um, occ) if out.kind == 'col' else (orr, un)
    tm, tn, tk = _tile(um, 1024), _tile(un, 1024), _tile(uk, 768 if uk % 768 == 0 else 512)
    nk = k // tk
    a_spec = pl.BlockSpec((tk, tm), lambda i, j, kk: (kk, i)) if mode == 'tn' else pl.BlockSpec((tm, tk), lambda i, j, kk: (i, kk))
    if isinstance(b, _View):
        b_spec = (_view_spec(b, tn, tk, lambda i, j, kk: (j, kk)) if mode == 'nt'
                  else _view_spec(b, tk, tn, lambda i, j, kk: (kk, j)))
    else:
        b_spec = pl.BlockSpec((tn, tk), lambda i, j, kk: (j, kk)) if mode == 'nt' else pl.BlockSpec((tk, tn), lambda i, j, kk: (kk, j))
    r_spec = pl.BlockSpec((tm, tn), lambda i, j, kk: (i, j))
    o_spec = r_spec if out is None else _view_spec(out, tm, tn, lambda i, j, kk: (i, j))
    dot = {'nn': _dot_nn, 'nt': _dot_nt, 'tn': _dot_tn}[mode]

    def body(*refs):
        refs = list(refs)
        if out is not None:
            del refs[-3]
        if res is None:
            a_ref, b_ref, o_ref, acc = refs
        else:
            a_ref, b_ref, r_ref, o_ref, acc = refs
        kk = pl.program_id(2)

        @pl.when(kk == 0)
        def _():
            acc[...] = jnp.zeros_like(acc)

        acc[...] += dot(a_ref[...], b_ref[...])

        @pl.when(kk == nk - 1)
        def _():
            out = acc[...]
            if res is not None:
                out = out + r_ref[...].astype(F32)
            o_ref[...] = out.astype(out_dtype)

    args, specs = [a, b.arr if isinstance(b, _View) else b], [a_spec, b_spec]
    if res is not None:
        args.append(res)
        specs.append(r_spec)
    aliases, out_shape = {}, jax.ShapeDtypeStruct((m, n), out_dtype)
    if out is not None:
        aliases, out_shape = {len(args): 0}, jax.ShapeDtypeStruct(out.arr.shape, out.arr.dtype)
        args.append(out.arr)
        specs.append(pl.BlockSpec(memory_space=pl.ANY))
    return _pcall(body, name=name, out_shape=out_shape,
                  grid=(m // tm, n // tn, nk), in_specs=specs, out_specs=o_spec,
                  scratch_shapes=[pltpu.VMEM((tm, tn), F32)], input_output_aliases=aliases,
                  compiler_params=_params(("parallel", "parallel", "arbitrary")))(*args)


def _rms_fwd(x, w):
    s, d = x.shape
    tr = _tile(s, ROW_TILE, 16)

    def body(x_ref, w_ref, o_ref):
        xv = x_ref[...]
        r = lax.rsqrt(jnp.mean(xv * xv, axis=-1, keepdims=True) + RMS_EPS)
        o_ref[...] = (xv * r * w_ref[...]).astype(BF16)

    return _pcall(body, name="rms_fwd", out_shape=jax.ShapeDtypeStruct((s, d), BF16), grid=(s // tr,),
                  in_specs=[pl.BlockSpec((tr, d), lambda i: (i, 0)), pl.BlockSpec((1, d), lambda i: (0, 0))],
                  out_specs=pl.BlockSpec((tr, d), lambda i: (i, 0)), compiler_params=_params(("parallel",)))(x, w)


def _rms_bwd(x, w, dh, dres):
    s, d = x.shape
    tr = _tile(s, ROW_TILE, 16)

    def body(x_ref, w_ref, dh_ref, dr_ref, dx_ref, dw_ref):
        i = pl.program_id(0)
        xv = x_ref[...]
        r = lax.rsqrt(jnp.mean(xv * xv, axis=-1, keepdims=True) + RMS_EPS)
        xh = xv * r
        dhv = dh_ref[...].astype(F32)
        dxh = dhv * w_ref[...]
        dx_ref[...] = dr_ref[...] + r * (dxh - xh * jnp.mean(dxh * xh, axis=-1, keepdims=True))
        part = jnp.sum(dhv * xh, axis=0, keepdims=True)

        @pl.when(i == 0)
        def _():
            dw_ref[...] = part

        @pl.when(i > 0)
        def _():
            dw_ref[...] += part

    row = pl.BlockSpec((tr, d), lambda i: (i, 0))
    vec = pl.BlockSpec((1, d), lambda i: (0, 0))
    return _pcall(body, name="rms_bwd",
                  out_shape=(jax.ShapeDtypeStruct((s, d), F32), jax.ShapeDtypeStruct((1, d), F32)),
                  grid=(s // tr,), in_specs=[row, vec, row, row], out_specs=(row, vec),
                  compiler_params=_params(("arbitrary",)))(x, w, dh, dres)


def _final_loss(x, w, target):
    s, d = x.shape
    tr = _tile(s, ROW_TILE, 16)

    def body(x_ref, w_ref, t_ref, l_ref, dx_ref, dw_ref):
        i = pl.program_id(0)
        xv = x_ref[...]
        r = lax.rsqrt(jnp.mean(xv * xv, axis=-1, keepdims=True) + RMS_EPS)
        xh = xv * r
        err = xh * w_ref[...] - t_ref[...]
        lpart = 0.5 * jnp.sum(jnp.mean(err * err, axis=-1, keepdims=True), axis=0, keepdims=True)
        dy = err * (1.0 / d)
        dxh = dy * w_ref[...]
        dx_ref[...] = r * (dxh - xh * jnp.mean(dxh * xh, axis=-1, keepdims=True))
        part = jnp.sum(dy * xh, axis=0, keepdims=True)

        @pl.when(i == 0)
        def _():
            dw_ref[...] = part
            l_ref[...] = jnp.broadcast_to(lpart, l_ref.shape)

        @pl.when(i > 0)
        def _():
            dw_ref[...] += part
            l_ref[...] += jnp.broadcast_to(lpart, l_ref.shape)

    row = pl.BlockSpec((tr, d), lambda i: (i, 0))
    vec = pl.BlockSpec((1, d), lambda i: (0, 0))
    return _pcall(body, name="final_loss",
                  out_shape=(jax.ShapeDtypeStruct((8, 128), F32), jax.ShapeDtypeStruct((s, d), F32),
                             jax.ShapeDtypeStruct((1, d), F32)),
                  grid=(s // tr,), in_specs=[row, vec, row],
                  out_specs=(pl.BlockSpec((8, 128), lambda i: (0, 0)), row, vec),
                  compiler_params=_params(("arbitrary",)))(x, w, target)


def _halo_prev(tt, cmap):
    return lambda *g: (jnp.maximum(g[-1] * (tt // HALO) - 1, 0), cmap(*g))


def _ffn_act_fwd(u, cw, cb):
    s, f2 = u.shape
    f = f2 // 2
    tt, tc = _tile(s, ROW_TILE, 16), _tile(f, 512)
    nc = f // tc

    def body(g_ref, gh_ref, v_ref, vh_ref, wg_ref, wv_ref, bg_ref, bv_ref, a_ref):
        t = pl.program_id(1)

        def conv(x_ref, h_ref, w_ref, b_ref):
            x = x_ref[...].astype(F32)
            hal = jnp.where(t > 0, h_ref[...].astype(F32)[8:16], 0.0)
            return (w_ref[2:3, :] * x + w_ref[1:2, :] * _shift_down(x, hal, 1)
                    + w_ref[0:1, :] * _shift_down(x, hal, 2) + b_ref[...])

        gate = conv(g_ref, gh_ref, wg_ref, bg_ref)
        val = conv(v_ref, vh_ref, wv_ref, bv_ref)
        a_ref[...] = (_gelu_and_grad(gate)[0] * val).astype(BF16)

    def main(off):
        return pl.BlockSpec((tt, tc), lambda j, t: (t, j + off))

    def halo(off):
        return pl.BlockSpec((HALO, tc), _halo_prev(tt, lambda j, t: j + off))

    def vec(rows, off):
        return pl.BlockSpec((rows, tc), lambda j, t: (0, j + off))

    return _pcall(body, name="ffn_act_fwd", out_shape=jax.ShapeDtypeStruct((s, f), BF16), grid=(nc, s // tt),
                  in_specs=[main(0), halo(0), main(nc), halo(nc), vec(3, 0), vec(3, nc), vec(1, 0), vec(1, nc)],
                  out_specs=pl.BlockSpec((tt, tc), lambda j, t: (t, j)),
                  compiler_params=_params(("parallel", "parallel")))(u, u, u, u, cw, cw, cb, cb)


def _ffn_act_bwd(u, cw, cb, da):
    s, f2 = u.shape
    f = f2 // 2
    tt, tc = _tile(s, ROW_TILE, 16), _tile(f, 512)
    nc, nt = f // tc, s // tt

    def body(g_ref, gh_ref, v_ref, vh_ref, da_ref, wg_ref, wv_ref, bg_ref, bv_ref,
             dug_ref, duv_ref, dwg_ref, dwv_ref, dbg_ref, dbv_ref, carry):
        t = pl.program_id(1)
        first_tile = t == nt - 1

        @pl.when(t == 0)
        def _():
            carry[...] = jnp.zeros_like(carry)

        def shifted(x_ref, h_ref):
            x = x_ref[...].astype(F32)
            hal = jnp.where(first_tile, 0.0, h_ref[...].astype(F32)[8:16])
            return x, _shift_down(x, hal, 1), _shift_down(x, hal, 2)

        def conv(xs, w_ref, b_ref):
            return w_ref[2:3, :] * xs[0] + w_ref[1:2, :] * xs[1] + w_ref[0:1, :] * xs[2] + b_ref[...]

        xg, xv = shifted(g_ref, gh_ref), shifted(v_ref, vh_ref)
        gate, val = conv(xg, wg_ref, bg_ref), conv(xv, wv_ref, bv_ref)
        dav = da_ref[...].astype(F32)
        gl, dgl = _gelu_and_grad(gate)
        dgate, dval = dav * val * dgl, dav * gl

        def back(k, d, xs, w_ref, du_ref, dw_ref, db_ref):
            nxt = carry[k]
            du_ref[...] = (w_ref[2:3, :] * d + w_ref[1:2, :] * _shift_up(d, nxt, 1)
                           + w_ref[0:1, :] * _shift_up(d, nxt, 2)).astype(BF16)
            carry[k] = d[0:8]
            parts = [jnp.sum(d * xs[2 - r], axis=0, keepdims=True) for r in range(3)]
            bpart = jnp.sum(d, axis=0, keepdims=True)

            @pl.when(t == 0)
            def _():
                for r in range(3):
                    dw_ref[r:r + 1, :] = parts[r]
                db_ref[...] = bpart

            @pl.when(t > 0)
            def _():
                for r in range(3):
                    dw_ref[r:r + 1, :] += parts[r]
                db_ref[...] += bpart

        back(0, dgate, xg, wg_ref, dug_ref, dwg_ref, dbg_ref)
        back(1, dval, xv, wv_ref, duv_ref, dwv_ref, dbv_ref)

    def main(off):
        return pl.BlockSpec((tt, tc), lambda j, t: (nt - 1 - t, j + off))

    def halo(off):
        return pl.BlockSpec((HALO, tc), lambda j, t: (jnp.maximum((nt - 1 - t) * (tt // HALO) - 1, 0), j + off))

    def vec(rows, off):
        return pl.BlockSpec((rows, tc), lambda j, t: (0, j + off))

    dug, duv, dwg, dwv, dbg, dbv = _pcall(
        body, name="ffn_act_bwd",
        out_shape=(jax.ShapeDtypeStruct((s, f), BF16), jax.ShapeDtypeStruct((s, f), BF16),
                   jax.ShapeDtypeStruct((3, f), F32), jax.ShapeDtypeStruct((3, f), F32),
                   jax.ShapeDtypeStruct((1, f), F32), jax.ShapeDtypeStruct((1, f), F32)),
        grid=(nc, nt),
        in_specs=[main(0), halo(0), main(nc), halo(nc), main(0), vec(3, 0), vec(3, nc), vec(1, 0), vec(1, nc)],
        out_specs=(main(0), main(0), vec(3, 0), vec(3, 0), vec(1, 0), vec(1, 0)),
        scratch_shapes=[pltpu.VMEM((2, 8, tc), F32)],
        compiler_params=_params(("parallel", "arbitrary")))(u, u, u, u, da, cw, cw, cb, cb)
    return dug, duv, jnp.concatenate([dwg, dwv], axis=1), jnp.concatenate([dbg, dbv], axis=1)


def _chunk_cols(h, kd, vd, heads):
    dk, dv = kd // heads, vd // heads
    return (slice(h * dk, (h + 1) * dk), slice(kd + h * dk, kd + (h + 1) * dk),
            slice(2 * kd + h * dv, 2 * kd + (h + 1) * dv), slice(2 * kd + vd + h * dv, 2 * kd + vd + (h + 1) * dv))


def _rope(x, cos, sin):
    half = x.shape[1] // 2
    x1, x2 = x[:, :half], x[:, half:]
    return jnp.concatenate([x1 * cos - x2 * sin, x2 * cos + x1 * sin], axis=1)


def _unrope(d, cos, sin):
    half = d.shape[1] // 2
    d1, d2 = d[:, :half], d[:, half:]
    return jnp.concatenate([d1 * cos + d2 * sin, d2 * cos - d1 * sin], axis=1)


def _chunk_inputs(gla, h, heads, kd, vd, rows, proj_ref, aux):
    qc, kc, vc, gc = _chunk_cols(h, kd, vd, heads)
    dk = kd // heads
    q = proj_ref[rows, qc].astype(F32)
    k = proj_ref[rows, kc].astype(F32)
    v = proj_ref[rows, vc]
    g = proj_ref[rows, gc].astype(F32)
    c = {}
    if gla:
        z_ref, wgk_ref, bgk_ref = aux
        c['z'] = z_ref[rows, :]
        c['gk'] = _dot_nn(c['z'], wgk_ref[:, qc]) + bgk_ref[:, qc]
        la = (jnp.minimum(c['gk'], 0.0) - jnp.log(1.0 + jnp.exp(-jnp.abs(c['gk'])))) * (1.0 / GLA_GATE_TAU)
        b = _cumsum_rows(la)
        bl = jnp.sum(la, axis=0, keepdims=True)
        q = q * (dk ** -0.5)
    else:
        cos_ref, sin_ref = aux
        c['cos'], c['sin'] = cos_ref[rows, :], sin_ref[rows, :]
        q = _rope(q, c['cos'], c['sin'])
        k = _rope(k, c['cos'], c['sin']) * (dk ** -0.5)
        lg = math.log(1.0 - 2.0 ** (-5.0 - h))
        b = lg * (_row_iota((CHUNK, 1)).astype(F32) + 1.0)
        bl = jnp.full((1, 1), lg * CHUNK, F32)
    eb, enb = jnp.exp(b), jnp.exp(-b)
    c.update(q=q, k=k, v=v, g=g, b=b, bl=bl, eb=eb, enb=enb, ebl=jnp.exp(bl),
             qd=q * eb, kg=k * enb, qg=q * enb, kd=k * eb, ks=k * jnp.exp(bl - b))
    lower = _row_iota((CHUNK, CHUNK)) >= lax.broadcasted_iota(jnp.int32, (CHUNK, CHUNK), 1)
    c['lower'] = lower
    c['A'] = jnp.where(lower, _dot_nt(c['qd'], c['kg']), _dot_nt(c['qg'], c['kd']))
    return c


def _head_norm(gla, o):
    if not gla:
        o = o - jnp.mean(o, axis=-1, keepdims=True)
    r = lax.rsqrt(jnp.mean(o * o, axis=-1, keepdims=True) + RMS_EPS)
    return o * r, r


def _chunk_fwd(gla, proj, aux_arrays, nw, heads, kd, vd, cps):
    s, pw = proj.shape
    dk, dv = kd // heads, vd // heads
    rt = CHUNK * cps
    nb = s // rt
    n_aux = len(aux_arrays)

    def body(*refs):
        proj_ref, aux, nw_ref = refs[0], refs[1:1 + n_aux], refs[1 + n_aux]
        y_ref, o_ref, st_ref, state = refs[2 + n_aux:]

        @pl.when(pl.program_id(0) == 0)
        def _():
            state[...] = jnp.zeros_like(state)

        def chunk(ci, carry):
            rows = pl.ds(pl.multiple_of(ci * CHUNK, CHUNK), CHUNK)
            for h in range(heads):
                c = _chunk_inputs(gla, h, heads, kd, vd, rows, proj_ref, aux)
                vcols = slice(h * dv, (h + 1) * dv)
                st0 = state[h]
                st_ref[ci, h] = st0.astype(BF16)
                o = _dot_nn(c['A'], c['v']) + _dot_nt(c['qd'], st0)
                state[h] = st0 * c['ebl'] + _dot_tn(c['v'], c['ks'])
                oh, _ = _head_norm(gla, o)
                gv = c['g']
                y_ref[rows, vcols] = (oh * nw_ref[:, vcols] * (gv * _sigmoid(gv))).astype(BF16)
                o_ref[rows, vcols] = o.astype(BF16)
            return carry

        lax.fori_loop(0, cps, chunk, 0)

    row = lambda w: pl.BlockSpec((rt, w), lambda n: (n, 0))
    full = lambda a: pl.BlockSpec(a.shape, lambda n: (0,) * a.ndim)
    aux_specs = [row(a.shape[1]) if a.shape[0] == s else full(a) for a in aux_arrays]
    return _pcall(
        body, name="gla_fwd" if gla else "ret_fwd",
        out_shape=(jax.ShapeDtypeStruct((s, vd), BF16), jax.ShapeDtypeStruct((s, vd), BF16),
                   jax.ShapeDtypeStruct((s // CHUNK, heads, dv, dk), BF16)),
        grid=(nb,), in_specs=[row(pw)] + aux_specs + [full(nw)],
        out_specs=(row(vd), row(vd), pl.BlockSpec((cps, heads, dv, dk), lambda n: (n, 0, 0, 0))),
        scratch_shapes=[pltpu.VMEM((heads, dv, dk), F32)],
        compiler_params=_params(("arbitrary",)))(proj, *aux_arrays, nw)


def _chunk_bwd(gla, proj, aux_arrays, nw, o_st, st, dy, heads, kd, vd, cps):
    s, pw = proj.shape
    dk, dv = kd // heads, vd // heads
    rt = CHUNK * cps
    nb = s // rt
    n_aux = len(aux_arrays)

    def body(*refs):
        proj_ref, aux, nw_ref = refs[0], refs[1:1 + n_aux], refs[1 + n_aux]
        o_ref, st_ref, dy_ref = refs[2 + n_aux:5 + n_aux]
        outs = refs[5 + n_aux:]
        dp_ref, dnw_ref = outs[0], outs[1]
        if gla:
            dz_ref, dwgk_ref, dbgk_ref, dstate = outs[2:]
        else:
            dstate = outs[2]

        @pl.when(pl.program_id(0) == 0)
        def _():
            dstate[...] = jnp.zeros_like(dstate)
            dnw_ref[...] = jnp.zeros_like(dnw_ref)
            if gla:
                dwgk_ref[...] = jnp.zeros_like(dwgk_ref)
                dbgk_ref[...] = jnp.zeros_like(dbgk_ref)

        def chunk(i, carry):
            ci = cps - 1 - i
            rows = pl.ds(pl.multiple_of(ci * CHUNK, CHUNK), CHUNK)
            dz = jnp.zeros((CHUNK, GATE_PAD), F32)
            for h in range(heads):
                c = _chunk_inputs(gla, h, heads, kd, vd, rows, proj_ref, aux)
                qc, kc, vc, gc = _chunk_cols(h, kd, vd, heads)
                vcols = slice(h * dv, (h + 1) * dv)
                o = o_ref[rows, vcols].astype(F32)
                oh, r = _head_norm(gla, o)
                dyv = dy_ref[rows, vcols].astype(F32)
                gv = c['g']
                sg = _sigmoid(gv)
                nwv = nw_ref[:, vcols]
                dp_ref[rows, gc] = (dyv * oh * nwv * (sg * (1.0 + gv * (1.0 - sg)))).astype(BF16)
                dn = dyv * (gv * sg)
                dnw_ref[:, vcols] += jnp.sum(dn * oh, axis=0, keepdims=True)
                doh = dn * nwv
                do = doh - oh * jnp.mean(doh * oh, axis=-1, keepdims=True)
                if not gla:
                    do = do - jnp.mean(doh, axis=-1, keepdims=True)
                do = r * do
                st0 = st_ref[ci, h]
                dst1 = dstate[h]
                v = c['v']
                da = _dot_nt(do, v)
                dal = jnp.where(c['lower'], da, 0.0)
                dau = da - dal
                dp_ref[rows, vc] = (_dot_tn(c['A'], do) + _dot_nt(c['ks'], dst1)).astype(BF16)
                dqd = _dot_nn(dal, c['kg']) + _dot_nn(do, st0)
                dkg = _dot_tn(dal, c['qd'])
                dqg = _dot_nn(dau, c['kd'])
                dkd = _dot_tn(dau, c['qg'])
                dks = _dot_nn(v, dst1)
                dstate[h] = _dot_tn(do, c['qd']) + dst1 * c['ebl']
                dq = dqd * c['eb'] + dqg * c['enb']
                dkk = dkg * c['enb'] + dkd * c['eb'] + dks * jnp.exp(c['bl'] - c['b'])
                if gla:
                    db = dqd * c['qd'] - dkg * c['kg'] - dqg * c['qg'] + dkd * c['kd'] - dks * c['ks']
                    dbl = (jnp.sum(dks * c['ks'], axis=0, keepdims=True)
                           + c['ebl'] * jnp.sum(dst1 * st0.astype(F32), axis=0, keepdims=True))
                    db = db + jnp.where(_row_iota(db.shape) == CHUNK - 1, dbl, 0.0)
                    dgk = _rev_cumsum_rows(db) * (1.0 / GLA_GATE_TAU) / (1.0 + jnp.exp(c['gk']))
                    _, wgk_ref, _ = aux
                    dz = dz + _dot_nt(dgk, wgk_ref[:, qc])
                    dwgk_ref[:, qc] += _dot_tn(c['z'], dgk)
                    dbgk_ref[:, qc] += jnp.sum(dgk, axis=0, keepdims=True)
                    dp_ref[rows, qc] = (dq * (dk ** -0.5)).astype(BF16)
                    dp_ref[rows, kc] = dkk.astype(BF16)
                else:
                    dp_ref[rows, qc] = _unrope(dq, c['cos'], c['sin']).astype(BF16)
                    dp_ref[rows, kc] = (_unrope(dkk, c['cos'], c['sin']) * (dk ** -0.5)).astype(BF16)
            if gla:
                dz_ref[rows, :] = dz
            return carry

        lax.fori_loop(0, cps, chunk, 0)

    row = lambda w: pl.BlockSpec((rt, w), lambda n: (nb - 1 - n, 0))
    full = lambda a: pl.BlockSpec(a.shape, lambda n: (0,) * a.ndim)
    aux_specs = [row(a.shape[1]) if a.shape[0] == s else full(a) for a in aux_arrays]
    out_shape = [jax.ShapeDtypeStruct((s, pw), BF16), jax.ShapeDtypeStruct((1, vd), F32)]
    out_specs = [row(pw), full(nw)]
    if gla:
        wgk, bgk = aux_arrays[1], aux_arrays[2]
        out_shape += [jax.ShapeDtypeStruct((s, GATE_PAD), F32), jax.ShapeDtypeStruct(wgk.shape, F32),
                      jax.ShapeDtypeStruct(bgk.shape, F32)]
        out_specs += [row(GATE_PAD), full(wgk), full(bgk)]
    return _pcall(
        body, name="gla_bwd" if gla else "ret_bwd", out_shape=tuple(out_shape), grid=(nb,),
        in_specs=[row(pw)] + aux_specs + [full(nw), row(vd),
                                          pl.BlockSpec((cps, heads, dv, dk), lambda n: (nb - 1 - n, 0, 0, 0)), row(vd)],
        out_specs=tuple(out_specs), scratch_shapes=[pltpu.VMEM((heads, dv, dk), F32)],
        compiler_params=_params(("arbitrary",)))(proj, *aux_arrays, nw, o_st, st, dy)


def _lru_gates(xc, wga_ref, bga_ref, wgx_ref, bgx_ref, lam_ref):
    r = _sigmoid(_dot_nn(xc, wga_ref[0]) + bga_ref[...])
    i = _sigmoid(_dot_nn(xc, wgx_ref[0]) + bgx_ref[...])
    sp = _softplus(-lam_ref[...])
    la = -LRU_C * r * sp
    return r, i, sp, la, jnp.exp(la), jnp.sqrt(_neg_expm1(2.0 * la))


def _lru_specs(w, nbk, tt, tmap):
    main = lambda off: pl.BlockSpec((tt, LRU_BLOCK_W), lambda n, t: (tmap(t), n + off))
    halo = pl.BlockSpec((HALO, LRU_BLOCK_W), lambda n, t: (jnp.maximum(tmap(t) * (tt // HALO) - 1, 0), n))
    vec = lambda rows: pl.BlockSpec((rows, LRU_BLOCK_W), lambda n, t: (0, n))
    mat = pl.BlockSpec((1, LRU_BLOCK_W, LRU_BLOCK_W), lambda n, t: (n, 0, 0))
    return main, halo, vec, mat


def _lru_fwd(proj, cw, cb, wga, bga, wgx, bgx, lam):
    s, w2 = proj.shape
    w = w2 // 2
    nbk, tt = w // LRU_BLOCK_W, _tile(s, ROW_TILE, 16)
    main, halo, vec, mat = _lru_specs(w, nbk, tt, lambda t: t)

    def body(x_ref, xh_ref, y_ref, cw_ref, cb_ref, wga_ref, bga_ref, wgx_ref, bgx_ref, lam_ref,
             out_ref, hs_ref, hcar):
        t = pl.program_id(1)
        x = x_ref[...]
        hal = jnp.where(t > 0, xh_ref[8:16, :], 0.0)
        xc = (cw_ref[3:4, :] * x + cw_ref[2:3, :] * _shift_down(x, hal, 1) + cw_ref[1:2, :] * _shift_down(x, hal, 2)
              + cw_ref[0:1, :] * _shift_down(x, hal, 3) + cb_ref[...])
        r, i, sp, la, a, mlt = _lru_gates(xc, wga_ref, bga_ref, wgx_ref, bgx_ref, lam_ref)

        @pl.when(t == 0)
        def _():
            hcar[...] = jnp.zeros_like(hcar)

        h = _scan_fwd(a, xc * i * mlt, hcar[...])
        hcar[...] = _pick_row(h, tt - 1)
        hs_ref[...] = h
        out_ref[...] = (h * _gelu_and_grad(y_ref[...])[0]).astype(BF16)

    return _pcall(body, name="lru_fwd",
                  out_shape=(jax.ShapeDtypeStruct((s, w), BF16), jax.ShapeDtypeStruct((s, w), F32)),
                  grid=(nbk, s // tt),
                  in_specs=[main(0), halo, main(nbk), vec(4), vec(1), mat, vec(1), mat, vec(1), vec(1)],
                  out_specs=(main(0), main(0)), scratch_shapes=[pltpu.VMEM((1, LRU_BLOCK_W), F32)],
                  compiler_params=_params(("parallel", "arbitrary")))(proj, proj, proj, cw, cb, wga, bga, wgx, bgx, lam)


def _lru_bwd(proj, hs, dout, cw, cb, wga, bga, wgx, bgx, lam):
    s, w2 = proj.shape
    w = w2 // 2
    nbk, tt = w // LRU_BLOCK_W, _tile(s, ROW_TILE, 16)
    nt = s // tt
    main, halo, vec, mat = _lru_specs(w, nbk, tt, lambda t: nt - 1 - t)

    def body(x_ref, xh_ref, y_ref, hs_ref, hh_ref, do_ref, cw_ref, cb_ref, wga_ref, bga_ref, wgx_ref, bgx_ref,
             lam_ref, dx_ref, dy_ref, dcw_ref, dcb_ref, dbga_ref, dbgx_ref, dlam_ref, dwga_ref, dwgx_ref,
             dhcar, dxcar):
        t = pl.program_id(1)
        first_tile = t == nt - 1

        @pl.when(t == 0)
        def _():
            dhcar[...] = jnp.zeros_like(dhcar)
            dxcar[...] = jnp.zeros_like(dxcar)

        x = x_ref[...]
        hal = jnp.where(first_tile, 0.0, xh_ref[8:16, :])
        xs = [x, _shift_down(x, hal, 1), _shift_down(x, hal, 2), _shift_down(x, hal, 3)]
        xc = cw_ref[3:4, :] * xs[0] + cw_ref[2:3, :] * xs[1] + cw_ref[1:2, :] * xs[2] + cw_ref[0:1, :] * xs[3] + cb_ref[...]
        r, i, sp, la, a, mlt = _lru_gates(xc, wga_ref, bga_ref, wgx_ref, bgx_ref, lam_ref)
        h = hs_ref[...]
        hprev = _shift_down(h, jnp.where(first_tile, 0.0, hh_ref[8:16, :]), 1)
        gl, dgl = _gelu_and_grad(y_ref[...])
        dov = do_ref[...].astype(F32)
        dy_ref[...] = (dov * h * dgl).astype(BF16)
        row = _row_iota(a.shape)
        coef = jnp.where(row == tt - 1, 1.0, pltpu.roll(a, tt - 1, 0))
        dh = _scan_rev(coef, dov * gl, dhcar[...])
        dhcar[...] = _pick_row(a * dh, 0)
        dxc = dh * i * mlt
        di = dh * xc * mlt
        dm = dh * xc * i
        dla = dh * hprev * a - dm * jnp.exp(2.0 * la) / mlt
        dpa = dla * (-LRU_C * sp) * r * (1.0 - r)
        dpx = di * i * (1.0 - i)
        dxc = dxc + _dot_nt(dpa, wga_ref[0]) + _dot_nt(dpx, wgx_ref[0])
        nxt = dxcar[...]
        dx_ref[...] = (cw_ref[3:4, :] * dxc + cw_ref[2:3, :] * _shift_up(dxc, nxt, 1)
                       + cw_ref[1:2, :] * _shift_up(dxc, nxt, 2) + cw_ref[0:1, :] * _shift_up(dxc, nxt, 3)).astype(BF16)
        dxcar[...] = dxc[0:8]
        colsum = lambda v: jnp.sum(v, axis=0, keepdims=True)
        parts = [(dcb_ref, colsum(dxc)), (dbga_ref, colsum(dpa)), (dbgx_ref, colsum(dpx)),
                 (dlam_ref, colsum(dla * LRU_C * r) * _sigmoid(-lam_ref[...]))]
        wparts = [colsum(dxc * xs[3 - k]) for k in range(4)]
        dwa, dwx = _dot_tn(xc, dpa), _dot_tn(xc, dpx)

        @pl.when(t == 0)
        def _():
            for ref, val in parts:
                ref[...] = val
            for k in range(4):
                dcw_ref[k:k + 1, :] = wparts[k]
            dwga_ref[0] = dwa
            dwgx_ref[0] = dwx

        @pl.when(t > 0)
        def _():
            for ref, val in parts:
                ref[...] += val
            for k in range(4):
                dcw_ref[k:k + 1, :] += wparts[k]
            dwga_ref[0] += dwa
            dwgx_ref[0] += dwx

    sd = jax.ShapeDtypeStruct
    return _pcall(
        body, name="lru_bwd",
        out_shape=(sd((s, w), BF16), sd((s, w), BF16), sd((4, w), F32), sd((1, w), F32), sd((1, w), F32),
                   sd((1, w), F32), sd((1, w), F32), sd(wga.shape, F32), sd(wgx.shape, F32)),
        grid=(nbk, nt),
        in_specs=[main(0), halo, main(nbk), main(0), halo, main(0), vec(4), vec(1), mat, vec(1), mat, vec(1), vec(1)],
        out_specs=(main(0), main(0), vec(4), vec(1), vec(1), vec(1), vec(1), mat, mat),
        scratch_shapes=[pltpu.VMEM((1, LRU_BLOCK_W), F32), pltpu.VMEM((8, LRU_BLOCK_W), F32)],
        compiler_params=_params(("parallel", "arbitrary")))(proj, proj, proj, hs, hs, dout, cw, cb, wga, bga, wgx, bgx, lam)


def _adamw(parts, w, m, v):
    shape = w.shape
    cols = shape[-1]
    w2, m2, v2 = (a.reshape(-1, cols) for a in (w, m, v))
    rows = w2.shape[0]
    tr = _tile(rows, max(16, (1 << 17) // cols // 16 * 16), 16)
    blk = pl.BlockSpec((tr, cols), lambda i: (i, 0))
    p_args, p_specs = [], []
    for part in parts:
        if isinstance(part, tuple):
            stack, idx = part
            p_args.append(stack.reshape(stack.shape[0], rows, cols))
            p_specs.append(pl.BlockSpec((None, tr, cols), lambda i, idx=idx: (idx, i, 0)))
        else:
            p_args.append(part.reshape(rows, cols))
            p_specs.append(blk)
    n_parts = len(parts)

    def body(*refs):
        w_ref, m_ref, v_ref, g_ref, d_ref, nm_ref, nv_ref = refs[n_parts:]
        gv = refs[0][...].astype(F32)
        for p_ref in refs[1:n_parts]:
            gv = gv + p_ref[...].astype(F32)
        g_ref[...] = gv
        nm = ADAM_B1 * m_ref[...] + (1.0 - ADAM_B1) * gv
        nv = ADAM_B2 * v_ref[...] + (1.0 - ADAM_B2) * (gv * gv)
        m_hat = nm / (1.0 - ADAM_B1 ** ADAM_STEP)
        v_hat = nv / (1.0 - ADAM_B2 ** ADAM_STEP)
        d_ref[...] = -ADAM_LR * (m_hat / (jnp.sqrt(v_hat) + ADAM_EPS) + ADAM_WD * w_ref[...])
        nm_ref[...] = nm
        nv_ref[...] = nv

    out = _pcall(body, name="adamw", out_shape=tuple(jax.ShapeDtypeStruct((rows, cols), F32) for _ in range(4)),
                 grid=(rows // tr,), in_specs=p_specs + [blk] * 3, out_specs=(blk,) * 4,
                 compiler_params=_params(("parallel",)))(*p_args, w2, m2, v2)
    return tuple(o.reshape(shape) for o in out)


def _sum_list(arrs, out_dtype):
    shape = arrs[0].shape
    cols = shape[-1]
    flat = [a.reshape(-1, cols) for a in arrs]
    rows = flat[0].shape[0]
    tr = _tile(rows, max(16, (1 << 18) // cols // 16 * 16), 16)

    def body(*refs):
        acc = refs[0][...].astype(F32)
        for p_ref in refs[1:-1]:
            acc = acc + p_ref[...].astype(F32)
        refs[-1][...] = acc.astype(out_dtype)

    blk = pl.BlockSpec((tr, cols), lambda i: (i, 0))
    out = _pcall(body, name="sum_list", out_shape=jax.ShapeDtypeStruct((rows, cols), out_dtype), grid=(rows // tr,),
                 in_specs=[blk] * len(flat), out_specs=blk, compiler_params=_params(("parallel",)))(*flat)
    return out.reshape(shape)


def _mesh_pos():
    return lax.axis_index("x"), lax.axis_index("y"), lax.axis_index("c")


def _all_gather(arrs):
    n = len(arrs)

    def body(*refs):
        x_refs, out_refs = refs[:n], refs[n:2 * n]
        send_sems, recv_sems, local_sems = refs[2 * n:]
        x, y, c = _mesh_pos()
        me, sibling = (x, y, c), (x, y, 1 - c)
        chips = [(1 - x, y), (x, 1 - y), (1 - x, 1 - y)]

        def slot(a, px, py, pc):
            return out_refs[a].at[4 * px + 2 * py + pc]

        def copy(a, k, block, to, src=None):
            return pltpu.make_async_remote_copy(
                src_ref=slot(a, *block) if src is None else src, dst_ref=slot(a, *block),
                send_sem=send_sems.at[7 * a + k], recv_sem=recv_sems.at[7 * a + k], device_id=to, device_id_type=MESH)

        mine = [pltpu.make_async_copy(x_refs[a], slot(a, *me), local_sems.at[a]) for a in range(n)]
        for cp in mine:
            cp.start()
        first = []
        for j, chip in enumerate(chips):
            first += [copy(a, 1 + j, me, (*chip, c), src=x_refs[a]) for a in range(n)]
        first += [copy(a, 0, me, sibling, src=x_refs[a]) for a in range(n)]
        for cp in first:
            cp.start()
        passed = []
        for j, chip in enumerate(chips):
            for a in range(n):
                copy(a, 1 + j, (*chip, c), me).wait_recv()
                passed.append(copy(a, 4 + j, (*chip, c), sibling))
                passed[-1].start()
        for a in range(n):
            copy(a, 0, sibling, me).wait_recv()
        for j, chip in enumerate(chips):
            for a in range(n):
                copy(a, 4 + j, (*chip, 1 - c), me).wait_recv()
        for cp in first + passed:
            cp.wait_send()
        for cp in mine:
            cp.wait()

    hbm = pl.BlockSpec(memory_space=pl.ANY)
    return _pcall(body, name="all_gather",
                  out_shape=tuple(jax.ShapeDtypeStruct((N_DEV,) + a.shape, a.dtype) for a in arrs),
                  in_specs=[hbm] * n, out_specs=(hbm,) * n,
                  scratch_shapes=[pltpu.SemaphoreType.DMA((7 * n,)), pltpu.SemaphoreType.DMA((7 * n,)),
                                  pltpu.SemaphoreType.DMA((n,))],
                  compiler_params=pltpu.CompilerParams(has_side_effects=True))(*arrs)


def _sibling_exchange(arrs):
    n = len(arrs)

    def body(*refs):
        x_refs, out_refs, send_sems, recv_sems = refs[:n], refs[n:2 * n], refs[2 * n], refs[2 * n + 1]
        x, y, c = _mesh_pos()
        for a in range(n):
            for k in range(4):
                pltpu.make_async_remote_copy(src_ref=x_refs[a].at[2 * k + (1 - c)], dst_ref=out_refs[a].at[k],
                                             send_sem=send_sems.at[a], recv_sem=recv_sems.at[a],
                                             device_id=(x, y, 1 - c), device_id_type=MESH).start()
        for a in range(n):
            pltpu.make_async_remote_copy(src_ref=out_refs[a], dst_ref=out_refs[a], send_sem=send_sems.at[a],
                                         recv_sem=recv_sems.at[a], device_id=(x, y, 1 - c), device_id_type=MESH).wait()

    hbm = pl.BlockSpec(memory_space=pl.ANY)
    return _pcall(body, name="sibling_exchange",
                  out_shape=tuple(jax.ShapeDtypeStruct((4,) + a.shape[1:], a.dtype) for a in arrs),
                  in_specs=[hbm] * n, out_specs=(hbm,) * n,
                  scratch_shapes=[pltpu.SemaphoreType.DMA((n,)), pltpu.SemaphoreType.DMA((n,))],
                  compiler_params=pltpu.CompilerParams(has_side_effects=True))(*arrs)


def _chip_exchange(arrs):
    n = len(arrs)

    def body(*refs):
        x_refs, out_refs, send_sems, recv_sems = refs[:n], refs[n:2 * n], refs[2 * n], refs[2 * n + 1]
        x, y, c = _mesh_pos()
        chips = [(1 - x, y), (x, 1 - y), (1 - x, 1 - y)]
        cps = [pltpu.make_async_remote_copy(src_ref=x_refs[a].at[2 * cx + cy], dst_ref=out_refs[a].at[j],
                                            send_sem=send_sems.at[3 * a + j], recv_sem=recv_sems.at[3 * a + j],
                                            device_id=(cx, cy, c), device_id_type=MESH)
               for j, (cx, cy) in enumerate(chips) for a in range(n)]
        for cp in cps:
            cp.start()
        for cp in cps:
            cp.wait()

    hbm = pl.BlockSpec(memory_space=pl.ANY)
    return _pcall(body, name="chip_exchange",
                  out_shape=tuple(jax.ShapeDtypeStruct((3,) + a.shape[1:], a.dtype) for a in arrs),
                  in_specs=[hbm] * n, out_specs=(hbm,) * n,
                  scratch_shapes=[pltpu.SemaphoreType.DMA((3 * n,)), pltpu.SemaphoreType.DMA((3 * n,))],
                  compiler_params=pltpu.CompilerParams(has_side_effects=True))(*arrs)


def _pack(arrays, dtype):
    lead = arrays[0].shape[:-1]
    flat = jnp.concatenate([a.astype(dtype) for a in arrays], axis=-1)
    n = flat.shape[-1]
    unit = 16 * COMM_LANES
    pad = (-n) % unit
    flat = jnp.pad(flat, [(0, 0)] * len(lead) + [(0, pad)])
    return flat.reshape(lead + ((n + pad) // COMM_LANES, COMM_LANES))


def _unpack(packed, shapes):
    lead = packed.shape[:-2]
    flat = packed.reshape(lead + (-1,))
    out, off = [], 0
    for shp in shapes:
        n = math.prod(shp)
        out.append(flat[..., off:off + n].reshape(lead + tuple(shp)))
        off += n
    return out


def _unshard(g, ax):
    shp = list(g.shape[1:])
    shp[ax] *= N_DEV
    return jnp.moveaxis(g, 0, ax).reshape(shp)


def _split(full, ax):
    shp = list(full.shape)
    r = full.reshape(shp[:ax] + [N_DEV, shp[ax] // N_DEV] + shp[ax + 1:])
    return jnp.moveaxis(r, ax, 0)


def _rope_tables(s, dk):
    half = dk // 2
    inv = ROPE_BASE ** (-jnp.arange(half, dtype=F32) / half)
    ang = jnp.arange(s, dtype=F32)[:, None] * inv[None, :]
    return jnp.cos(ang), jnp.sin(ang)


def _local_step(x, target, p):
    s, d = x.shape
    depth = p['norm_mix_w'].shape[0]
    gla_kd, gla_vd = d // 2, d
    ret_kd, ret_vd = d, 2 * d
    cos, sin = _rope_tables(s, ret_kd // RET_HEADS)
    row = lambda v: v.reshape(1, -1)
    saved = []
    gb = {n: jnp.zeros(p[n].shape, BF16) for n in GATHERED}

    def view(name, layer, dev0=0, ndev=N_DEV, arr=None):
        return _View(p[name] if arr is None else arr, 'col' if SHARD_AX[name] == 2 else 'row', layer, dev0, ndev)

    def dw_into(name, layer, a, b, label, dev0=0, ndev=N_DEV):
        gb[name] = _mm(a, b, 'tn', BF16, name=label, out=view(name, layer, dev0, ndev, gb[name]))

    def gla_weights(j):
        w_in = p['gla_w_in'][j]
        w_main, w_z = w_in[:, :2 * gla_kd + 2 * gla_vd], w_in[:, 2 * gla_kd + 2 * gla_vd:]
        w_z = jnp.pad(w_z, ((0, 0), (0, GATE_PAD - GLA_GATE_RANK)))
        w_gk = jnp.pad(p['gla_w_gk'][j], ((0, GATE_PAD - GLA_GATE_RANK), (0, 0)))
        return w_main, w_z, w_gk, row(p['gla_b_gk'][j]), row(p['gla_norm_w'][j]), view('gla_w_out', j)

    def lru_weights(j):
        return (view('lru_w_in', j), p['lru_conv_w'][j], row(p['lru_conv_b'][j]), p['lru_w_ga'][j], row(p['lru_b_ga'][j]),
                p['lru_w_gx'][j], row(p['lru_b_gx'][j]), row(p['lru_lambda'][j]), view('lru_w_out', j))

    for i in range(depth):
        kind, j = i % 3, i // 3
        h = _rms_fwd(x, row(p['norm_mix_w'][i]))
        if kind == 0:
            w_main, w_z, w_gk, b_gk, nw, w_out = gla_weights(j)
            proj = _mm(h, w_main, 'nn', BF16, name="gla_in")
            z = _mm(h, w_z, 'nn', F32, name="gla_z")
            y, o_st, st = _chunk_fwd(True, proj, (z, w_gk, b_gk), nw, GLA_HEADS, gla_kd, gla_vd, 4)
            mix = (proj, z, o_st, st, y)
        elif kind == 1:
            w_in, cw, cb, wga, bga, wgx, bgx, lam, w_out = lru_weights(j)
            proj = _mm(h, w_in, 'nn', F32, name="lru_in")
            y, hs = _lru_fwd(proj, cw, cb, wga, bga, wgx, bgx, lam)
            mix = (proj, hs, y)
        else:
            nw, w_out = row(p['ret_norm_w'][j]), view('ret_w_out', j)
            proj = _mm(h, view('ret_w_in', j), 'nn', BF16, name="ret_in")
            y, o_st, st = _chunk_fwd(False, proj, (cos, sin), nw, RET_HEADS, ret_kd, ret_vd, 2)
            mix = (proj, o_st, st, y)
        x_mid = _mm(y, w_out, 'nn', F32, res=x, name="mix_out")
        h2 = _rms_fwd(x_mid, row(p['norm_ffn_w'][i]))
        u = _mm(h2, view('ffn_w_up', i), 'nn', BF16, name="ffn_up")
        act = _ffn_act_fwd(u, p['ffn_conv_w'][i], row(p['ffn_conv_b'][i]))
        x_out = _mm(act, view('ffn_w_down', i), 'nn', F32, res=x_mid, name="ffn_down")
        saved.append((x, h, mix, x_mid, h2, u, act))
        x = x_out

    loss, dx, dw = _final_loss(x, row(p['norm_out_w']), target)
    g = {n: [None] * v.shape[0] for n, v in p.items() if n != 'norm_out_w' and n not in GATHERED}
    g['norm_out_w'] = dw.reshape(-1)
    half = N_DEV // 2

    for i in reversed(range(depth)):
        kind, j = i % 3, i // 3
        x_in, h, mix, x_mid, h2, u, act = saved[i]
        dw_into('ffn_w_down', i, act, dx, "ffn_down_dw")
        dact = _mm(dx, view('ffn_w_down', i), 'nt', BF16, name="ffn_down_dx")
        dug, duv, dcw, dcb = _ffn_act_bwd(u, p['ffn_conv_w'][i], row(p['ffn_conv_b'][i]), dact)
        g['ffn_conv_w'][i], g['ffn_conv_b'][i] = dcw, dcb.reshape(-1)
        dw_into('ffn_w_up', i, h2, dug, "ffn_up_dw", 0, half)
        dw_into('ffn_w_up', i, h2, duv, "ffn_up_dw", half, half)
        dh2 = _mm(dug, view('ffn_w_up', i, 0, half), 'nt', F32, name="ffn_up_dx")
        dh2 = _mm(duv, view('ffn_w_up', i, half, half), 'nt', BF16, res=dh2, name="ffn_up_dx")
        dx, dnw = _rms_bwd(x_mid, row(p['norm_ffn_w'][i]), dh2, dx)
        g['norm_ffn_w'][i] = dnw.reshape(-1)
        if kind == 0:
            w_main, w_z, w_gk, b_gk, nw, w_out = gla_weights(j)
            proj, z, o_st, st, y = mix
            dw_into('gla_w_out', j, y, dx, "mix_out_dw")
            dy = _mm(dx, w_out, 'nt', BF16, name="mix_out_dx")
            dproj, dnw, dz, dwgk, dbgk = _chunk_bwd(True, proj, (z, w_gk, b_gk), nw, o_st, st, dy,
                                                    GLA_HEADS, gla_kd, gla_vd, 4)
            g['gla_norm_w'][j], g['gla_b_gk'][j] = dnw.reshape(-1), dbgk.reshape(-1)
            g['gla_w_gk'][j] = dwgk[:GLA_GATE_RANK]
            dw_main = _mm(h, dproj, 'tn', F32, name="gla_in_dw")
            dw_z = _mm(h, dz, 'tn', F32, name="gla_z_dw")
            g['gla_w_in'][j] = jnp.concatenate([dw_main, dw_z[:, :GLA_GATE_RANK]], axis=1)
            dh = _mm(dproj, w_main, 'nt', F32, name="gla_in_dx")
            dh = _mm(dz, w_z, 'nt', BF16, res=dh, name="gla_z_dx")
        elif kind == 1:
            w_in, cw, cb, wga, bga, wgx, bgx, lam, w_out = lru_weights(j)
            proj, hs, y = mix
            dw_into('lru_w_out', j, y, dx, "mix_out_dw")
            dy = _mm(dx, w_out, 'nt', BF16, name="mix_out_dx")
            dxb, dyb, dcw, dcb, dbga, dbgx, dlam, dwga, dwgx = _lru_bwd(proj, hs, dy, cw, cb, wga, bga, wgx, bgx, lam)
            g['lru_conv_w'][j], g['lru_conv_b'][j] = dcw, dcb.reshape(-1)
            g['lru_b_ga'][j], g['lru_b_gx'][j], g['lru_lambda'][j] = dbga.reshape(-1), dbgx.reshape(-1), dlam.reshape(-1)
            g['lru_w_ga'][j], g['lru_w_gx'][j] = dwga, dwgx
            dproj = jnp.concatenate([dxb, dyb], axis=1)
            dw_into('lru_w_in', j, h, dproj, "lru_in_dw")
            dh = _mm(dproj, w_in, 'nt', BF16, name="lru_in_dx")
        else:
            nw, w_out = row(p['ret_norm_w'][j]), view('ret_w_out', j)
            proj, o_st, st, y = mix
            dw_into('ret_w_out', j, y, dx, "mix_out_dw")
            dy = _mm(dx, w_out, 'nt', BF16, name="mix_out_dx")
            dproj, dnw = _chunk_bwd(False, proj, (cos, sin), nw, o_st, st, dy, RET_HEADS, ret_kd, ret_vd, 2)
            g['ret_norm_w'][j] = dnw.reshape(-1)
            dw_into('ret_w_in', j, h, dproj, "ret_in_dw")
            dh = _mm(dproj, view('ret_w_in', j), 'nt', BF16, name="ret_in_dx")
        dx, dnw = _rms_bwd(x_in, row(p['norm_mix_w'][i]), dh, dx)
        g['norm_mix_w'][i] = dnw.reshape(-1)

    grads = {n: (v if n == 'norm_out_w' else jnp.stack(v)) for n, v in g.items()}
    for n in BIG:
        grads[n] = gb[n] if n in GATHERED else _split(grads[n], SHARD_AX[n]).astype(BF16)
    return loss[0, 0], dx, grads


def kernel(x, norm_mix_w, norm_ffn_w, norm_out_w, gla_w_in, gla_w_gk, gla_b_gk, gla_norm_w, gla_w_out, lru_w_in, lru_conv_w, lru_conv_b, lru_w_ga, lru_b_ga, lru_w_gx, lru_b_gx, lru_lambda, lru_w_out, ret_w_in, ret_norm_w, ret_w_out, ffn_w_up, ffn_conv_w, ffn_conv_b, ffn_w_down, loss_target, m_norm_mix_w, m_norm_ffn_w, m_norm_out_w, m_gla_w_in, m_gla_w_gk, m_gla_b_gk, m_gla_norm_w, m_gla_w_out, m_lru_w_in, m_lru_conv_w, m_lru_conv_b, m_lru_w_ga, m_lru_b_ga, m_lru_w_gx, m_lru_b_gx, m_lru_lambda, m_lru_w_out, m_ret_w_in, m_ret_norm_w, m_ret_w_out, m_ffn_w_up, m_ffn_conv_w, m_ffn_conv_b, m_ffn_w_down, v_norm_mix_w, v_norm_ffn_w, v_norm_out_w, v_gla_w_in, v_gla_w_gk, v_gla_b_gk, v_gla_norm_w, v_gla_w_out, v_lru_w_in, v_lru_conv_w, v_lru_conv_b, v_lru_w_ga, v_lru_b_ga, v_lru_w_gx, v_lru_b_gx, v_lru_lambda, v_lru_w_out, v_ret_w_in, v_ret_norm_w, v_ret_w_out, v_ffn_w_up, v_ffn_conv_w, v_ffn_conv_b, v_ffn_w_down):
    given = dict(locals())
    w = {n: given[n] for n in WEIGHTS}
    me_x, me_y, me_c = _mesh_pos()
    me = 4 * me_x + 2 * me_y + me_c

    got = _all_gather([w[n].astype(BF16) for n in BIG] + [_pack([w[n].reshape(-1) for n in SMALL_SHARDED], F32)])
    p = {n: w[n] for n in REPLICATED}
    for n, blk in zip(BIG, got):
        p[n] = blk if n in GATHERED else _unshard(blk, SHARD_AX[n])
    for n, blk in zip(SMALL_SHARDED, _unpack(got[-1], [w[n].shape for n in SMALL_SHARDED])):
        p[n] = _unshard(blk, SHARD_AX[n])

    loss, grad_x, grads = _local_step(x[0], loss_target[0], p)
    loss = lax.psum(loss, ("x", "y", "c"))
    gw, delta, new_m, new_v = {}, {}, {}, {}

    small = REPLICATED + SMALL_SHARDED
    (parts,) = _all_gather([_pack([grads[n].reshape(-1) for n in small], F32)])
    summed = _unpack(_sum_list([parts[dev] for dev in range(N_DEV)], F32), [grads[n].shape for n in small])
    for n, gs in zip(small, summed):
        if SHARD_AX[n] is not None:
            gs = lax.dynamic_index_in_dim(_split(gs, SHARD_AX[n]), me, 0, keepdims=False)
        gw[n], delta[n], new_m[n], new_v[n] = _adamw([gs], w[n], given["m_" + n], given["v_" + n])

    blocks = [grads[n] for n in BIG]
    from_sibling = _sibling_exchange(blocks)
    chip_sums = []
    for blk, got_blk in zip(blocks, from_sibling):
        kept = lax.dynamic_index_in_dim(blk.reshape((4, 2) + blk.shape[1:]), me_c, 1, keepdims=False)
        chip_sums.append(_sum_list([kept, got_blk], BF16))
    from_chips = _chip_exchange(chip_sums)
    for n, cs, oth in zip(BIG, chip_sums, from_chips):
        mine = lax.dynamic_index_in_dim(cs, 2 * me_x + me_y, 0, keepdims=False)
        gw[n], delta[n], new_m[n], new_v[n] = _adamw([mine, (oth, 0), (oth, 1), (oth, 2)], w[n],
                                                     given["m_" + n], given["v_" + n])

    return (loss, grad_x[None], *[gw[n] for n in WEIGHTS], *[delta[n] for n in WEIGHTS],
            *[new_m[n] for n in WEIGHTS], *[new_v[n] for n in WEIGHTS])
```

```python
import collections
import math

import jax
import jax.numpy as jnp
from jax import lax
from jax.experimental import pallas as pl
from jax.experimental.pallas import tpu as pltpu

F32 = jnp.float32
BF16 = jnp.bfloat16

N_DEV = 8
CHUNK = 64
RMS_EPS = 1e-6
GLA_HEADS = 4
GLA_GATE_RANK = 16
GLA_GATE_TAU = 16.0
GATE_PAD = 128
LRU_BLOCK_W = 256
LRU_C = 8.0
RET_HEADS = 8
ROPE_BASE = 10000.0
ADAM_LR, ADAM_B1, ADAM_B2, ADAM_EPS, ADAM_WD, ADAM_STEP = 0.001, 0.9, 0.999, 1e-08, 0.01, 10

HALO = 16
VMEM_LIMIT = 56 * 1024 * 1024
ROW_TILE = 256
COMM_LANES = 1024
MM_TM, MM_TN, MM_TK = 1024, 1024, 2048

MESH = pl.DeviceIdType.MESH

WEIGHTS = ['norm_mix_w', 'norm_ffn_w', 'norm_out_w', 'gla_w_in', 'gla_w_gk', 'gla_b_gk', 'gla_norm_w',
           'gla_w_out', 'lru_w_in', 'lru_conv_w', 'lru_conv_b', 'lru_w_ga', 'lru_b_ga', 'lru_w_gx',
           'lru_b_gx', 'lru_lambda', 'lru_w_out', 'ret_w_in', 'ret_norm_w', 'ret_w_out', 'ffn_w_up',
           'ffn_conv_w', 'ffn_conv_b', 'ffn_w_down']
SHARD_AX = {'norm_mix_w': None, 'norm_ffn_w': None, 'norm_out_w': None, 'gla_w_in': 2, 'gla_w_gk': 2,
            'gla_b_gk': 1, 'gla_norm_w': 1, 'gla_w_out': 1, 'lru_w_in': 2, 'lru_conv_w': 2,
            'lru_conv_b': None, 'lru_w_ga': 2, 'lru_b_ga': None, 'lru_w_gx': 2, 'lru_b_gx': None,
            'lru_lambda': None, 'lru_w_out': 1, 'ret_w_in': 2, 'ret_norm_w': 1, 'ret_w_out': 1,
            'ffn_w_up': 2, 'ffn_conv_w': 2, 'ffn_conv_b': None, 'ffn_w_down': 1}
BIG = ['gla_w_in', 'gla_w_out', 'lru_w_in', 'lru_w_ga', 'lru_w_gx', 'lru_w_out', 'ret_w_in', 'ret_w_out',
       'ffn_w_up', 'ffn_w_down']
GATHERED = ['gla_w_out', 'lru_w_in', 'lru_w_out', 'ret_w_in', 'ret_w_out', 'ffn_w_up', 'ffn_w_down']
SMALL_SHARDED = ['gla_w_gk', 'gla_b_gk', 'gla_norm_w', 'lru_conv_w', 'ret_norm_w', 'ffn_conv_w']
REPLICATED = [n for n in WEIGHTS if SHARD_AX[n] is None]


def _pcall(body, **kw):
    return pl.pallas_call(body, **kw)


def _params(sem=None, **kw):
    return pltpu.CompilerParams(dimension_semantics=sem, vmem_limit_bytes=VMEM_LIMIT, **kw)


def _tile(n, pref, align=128):
    if n <= pref:
        return n
    t = (pref // align) * align
    while t >= align:
        if n % t == 0:
            return t
        t -= align
    return n


def _row_iota(shape):
    return lax.broadcasted_iota(jnp.int32, shape, 0)


def _shift_down(x, halo, s):
    t, c = x.shape
    r = pltpu.roll(x.reshape(t // 8, 8, c), s, 1)
    prev = jnp.concatenate([pltpu.roll(halo, s, 0)[None], r[:-1]], axis=0)
    sub = lax.broadcasted_iota(jnp.int32, r.shape, 1)
    return jnp.where(sub < s, prev, r).reshape(t, c)


def _shift_up(x, nxt, s):
    t, c = x.shape
    r = pltpu.roll(x.reshape(t // 8, 8, c), 8 - s, 1)
    follow = jnp.concatenate([r[1:], pltpu.roll(nxt, 8 - s, 0)[None]], axis=0)
    sub = lax.broadcasted_iota(jnp.int32, r.shape, 1)
    return jnp.where(sub >= 8 - s, follow, r).reshape(t, c)


def _cumsum_rows(x):
    t, row, s = x.shape[0], _row_iota(x.shape), 1
    while s < t:
        x = x + jnp.where(row >= s, pltpu.roll(x, s, 0), 0.0)
        s *= 2
    return x


def _rev_cumsum_rows(x):
    t, row, s = x.shape[0], _row_iota(x.shape), 1
    while s < t:
        x = x + jnp.where(row < t - s, pltpu.roll(x, t - s, 0), 0.0)
        s *= 2
    return x


def _scan_fwd(a, u, h0):
    t, row, s = a.shape[0], _row_iota(a.shape), 1
    while s < t:
        keep = row >= s
        u = u + a * jnp.where(keep, pltpu.roll(u, s, 0), 0.0)
        a = a * jnp.where(keep, pltpu.roll(a, s, 0), 1.0)
        s *= 2
    return u + a * h0


def _scan_rev(c, g, d_end):
    t, row, s = c.shape[0], _row_iota(c.shape), 1
    while s < t:
        keep = row < t - s
        g = g + c * jnp.where(keep, pltpu.roll(g, t - s, 0), 0.0)
        c = c * jnp.where(keep, pltpu.roll(c, t - s, 0), 1.0)
        s *= 2
    return g + c * d_end


def _pick_row(x, r):
    return jnp.sum(jnp.where(_row_iota(x.shape) == r, x, 0.0), axis=0, keepdims=True)


def _sigmoid(x):
    return 1.0 / (1.0 + jnp.exp(-x))


def _softplus(x):
    return jnp.maximum(x, 0.0) + jnp.log(1.0 + jnp.exp(-jnp.abs(x)))


_GELU_C = math.sqrt(2.0 / math.pi)


def _gelu_and_grad(x):
    x2 = x * x
    th = jnp.tanh(_GELU_C * (x + 0.044715 * x * x2))
    g = 0.5 * x * (1.0 + th)
    dg = 0.5 * (1.0 + th) + 0.5 * x * (1.0 - th * th) * _GELU_C * (1.0 + 3.0 * 0.044715 * x2)
    return g, dg


def _neg_expm1(y):
    small = -(y * (1.0 + y * (0.5 + y * (1.0 / 6.0 + y * (1.0 / 24.0)))))
    return jnp.where(y > -0.01, small, 1.0 - jnp.exp(y))


def _dot(a, b, dims):
    return lax.dot_general(a.astype(BF16), b.astype(BF16), (dims, ((), ())), preferred_element_type=F32)


def _dot_nn(a, b):
    return _dot(a, b, ((1,), (0,)))


def _dot_nt(a, b):
    return _dot(a, b, ((1,), (1,)))


def _dot_tn(a, b):
    return _dot(a, b, ((0,), (0,)))


_View = collections.namedtuple("_View", "arr kind layer dev0 ndev")


def _view_shape(v):
    r, c = v.arr.shape[2:]
    return (r, v.ndev * c) if v.kind == 'col' else (N_DEV * r, c)


def _view_spec(v, tr, tc, rc_of):
    r, c = v.arr.shape[2:]
    if v.kind == 'col':
        per = c // tc

        def imap(*g):
            ri, ci = rc_of(*g)
            return (v.dev0 + ci // per, v.layer, ri, ci % per)
    elif tr > r:
        def imap_blocks(*g):
            ri, ci = rc_of(*g)
            return (ri, v.layer, 0, ci)
        return pl.BlockSpec((tr // r, None, r, tc), imap_blocks)
    else:
        per = r // tr

        def imap(*g):
            ri, ci = rc_of(*g)
            return (ri // per, v.layer, ri % per, ci)
    return pl.BlockSpec((None, None, tr, tc), imap)


def _row_tile(v, pref):
    r = v.arr.shape[2]
    if r >= pref:
        return _tile(r, pref)
    q = max(q for q in (1, 2, 4, 8) if r * q <= pref)
    return r * q


def _rows2d(val):
    return val.reshape(-1, val.shape[-1]) if val.ndim == 3 else val


def _mm(a, b, mode, out_dtype=F32, res=None, name="mm", out=None):
    bshape = _view_shape(b) if isinstance(b, _View) else b.shape
    if mode == 'nn':
        (m, k), n = a.shape, bshape[1]
    elif mode == 'nt':
        (m, k), n = a.shape, bshape[0]
    else:
        (k, m), n = a.shape, bshape[1]
    tm, tn, tk = _tile(m, MM_TM), _tile(n, MM_TN), _tile(k, MM_TK)
    if isinstance(b, _View):
        if b.kind == 'col' and mode == 'nt':
            tk = _tile(b.arr.shape[3], MM_TK)
        elif b.kind == 'col':
            tn = _tile(b.arr.shape[3], MM_TN)
        elif mode == 'nt':
            tn = _row_tile(b, MM_TN)
        else:
            tk = _row_tile(b, MM_TK)
    if out is not None:
        if out.kind == 'col':
            tn = _tile(out.arr.shape[3], MM_TN)
        else:
            tm = _row_tile(out, MM_TM)
    nk = k // tk
    a_spec = pl.BlockSpec((tk, tm), lambda i, j, kk: (kk, i)) if mode == 'tn' else pl.BlockSpec((tm, tk), lambda i, j, kk: (i, kk))
    if isinstance(b, _View):
        b_spec = (_view_spec(b, tn, tk, lambda i, j, kk: (j, kk)) if mode == 'nt'
                  else _view_spec(b, tk, tn, lambda i, j, kk: (kk, j)))
    else:
        b_spec = pl.BlockSpec((tn, tk), lambda i, j, kk: (j, kk)) if mode == 'nt' else pl.BlockSpec((tk, tn), lambda i, j, kk: (kk, j))
    r_spec = pl.BlockSpec((tm, tn), lambda i, j, kk: (i, j))
    o_spec = r_spec if out is None else _view_spec(out, tm, tn, lambda i, j, kk: (i, j))
    dot = {'nn': _dot_nn, 'nt': _dot_nt, 'tn': _dot_tn}[mode]

    def body(*refs):
        refs = list(refs)
        if out is not None:
            del refs[2 + (res is not None)]
        a_ref, b_ref = refs[:2]
        r_ref = None if res is None else refs[2]
        o_ref = refs[2 if res is None else 3]

        def finish(total):
            if res is not None:
                total = total + r_ref[...].astype(F32)
            o_ref[...] = total.astype(out_dtype).reshape(o_ref.shape)

        part = dot(a_ref[...], _rows2d(b_ref[...]))
        if nk == 1:
            finish(part)
            return
        acc = refs[-1]
        kk = pl.program_id(2)

        @pl.when(kk == 0)
        def _():
            acc[...] = part

        @pl.when(kk > 0)
        def _():
            acc[...] += part

        @pl.when(kk == nk - 1)
        def _():
            finish(acc[...])

    args, specs = [a, b.arr if isinstance(b, _View) else b], [a_spec, b_spec]
    if res is not None:
        args.append(res)
        specs.append(r_spec)
    aliases, out_shape = {}, jax.ShapeDtypeStruct((m, n), out_dtype)
    if out is not None:
        aliases, out_shape = {len(args): 0}, jax.ShapeDtypeStruct(out.arr.shape, out.arr.dtype)
        args.append(out.arr)
        specs.append(pl.BlockSpec(memory_space=pl.ANY))
    return _pcall(body, name=name, out_shape=out_shape,
                  grid=(m // tm, n // tn, nk), in_specs=specs, out_specs=o_spec,
                  scratch_shapes=[] if nk == 1 else [pltpu.VMEM((tm, tn), F32)], input_output_aliases=aliases,
                  compiler_params=_params(("parallel", "parallel", "arbitrary")))(*args)


def _rms_fwd(x, w):
    s, d = x.shape
    tr = _tile(s, ROW_TILE, 16)

    def body(x_ref, w_ref, o_ref):
        xv = x_ref[...]
        r = lax.rsqrt(jnp.mean(xv * xv, axis=-1, keepdims=True) + RMS_EPS)
        o_ref[...] = (xv * r * w_ref[...]).astype(BF16)

    return _pcall(body, name="rms_fwd", out_shape=jax.ShapeDtypeStruct((s, d), BF16), grid=(s // tr,),
                  in_specs=[pl.BlockSpec((tr, d), lambda i: (i, 0)), pl.BlockSpec((1, d), lambda i: (0, 0))],
                  out_specs=pl.BlockSpec((tr, d), lambda i: (i, 0)), compiler_params=_params(("parallel",)))(x, w)


def _rms_bwd(x, w, dh, dres):
    s, d = x.shape
    tr = _tile(s, ROW_TILE, 16)

    def body(x_ref, w_ref, dh_ref, dr_ref, dx_ref, dxb_ref, dw_ref):
        i = pl.program_id(0)
        xv = x_ref[...]
        r = lax.rsqrt(jnp.mean(xv * xv, axis=-1, keepdims=True) + RMS_EPS)
        xh = xv * r
        dhv = dh_ref[...].astype(F32)
        dxh = dhv * w_ref[...]
        dxv = dr_ref[...] + r * (dxh - xh * jnp.mean(dxh * xh, axis=-1, keepdims=True))
        dx_ref[...] = dxv
        dxb_ref[...] = dxv.astype(BF16)
        part = jnp.sum(dhv * xh, axis=0, keepdims=True)

        @pl.when(i == 0)
        def _():
            dw_ref[...] = part

        @pl.when(i > 0)
        def _():
            dw_ref[...] += part

    row = pl.BlockSpec((tr, d), lambda i: (i, 0))
    vec = pl.BlockSpec((1, d), lambda i: (0, 0))
    return _pcall(body, name="rms_bwd",
                  out_shape=(jax.ShapeDtypeStruct((s, d), F32), jax.ShapeDtypeStruct((s, d), BF16),
                             jax.ShapeDtypeStruct((1, d), F32)),
                  grid=(s // tr,), in_specs=[row, vec, row, row], out_specs=(row, row, vec),
                  compiler_params=_params(("arbitrary",)))(x, w, dh, dres)


def _final_loss(x, w, target):
    s, d = x.shape
    tr = _tile(s, ROW_TILE, 16)

    def body(x_ref, w_ref, t_ref, l_ref, dx_ref, dxb_ref, dw_ref):
        i = pl.program_id(0)
        xv = x_ref[...]
        r = lax.rsqrt(jnp.mean(xv * xv, axis=-1, keepdims=True) + RMS_EPS)
        xh = xv * r
        err = xh * w_ref[...] - t_ref[...]
        lpart = 0.5 * jnp.sum(jnp.mean(err * err, axis=-1, keepdims=True), axis=0, keepdims=True)
        dy = err * (1.0 / d)
        dxh = dy * w_ref[...]
        dxv = r * (dxh - xh * jnp.mean(dxh * xh, axis=-1, keepdims=True))
        dx_ref[...] = dxv
        dxb_ref[...] = dxv.astype(BF16)
        part = jnp.sum(dy * xh, axis=0, keepdims=True)

        @pl.when(i == 0)
        def _():
            dw_ref[...] = part
            l_ref[...] = jnp.broadcast_to(lpart, l_ref.shape)

        @pl.when(i > 0)
        def _():
            dw_ref[...] += part
            l_ref[...] += jnp.broadcast_to(lpart, l_ref.shape)

    row = pl.BlockSpec((tr, d), lambda i: (i, 0))
    vec = pl.BlockSpec((1, d), lambda i: (0, 0))
    return _pcall(body, name="final_loss",
                  out_shape=(jax.ShapeDtypeStruct((8, 128), F32), jax.ShapeDtypeStruct((s, d), F32),
                             jax.ShapeDtypeStruct((s, d), BF16), jax.ShapeDtypeStruct((1, d), F32)),
                  grid=(s // tr,), in_specs=[row, vec, row],
                  out_specs=(pl.BlockSpec((8, 128), lambda i: (0, 0)), row, row, vec),
                  compiler_params=_params(("arbitrary",)))(x, w, target)


def _halo_prev(tt, cmap):
    return lambda *g: (jnp.maximum(g[-1] * (tt // HALO) - 1, 0), cmap(*g))


def _ffn_act_fwd(u, cw, cb):
    s, f2 = u.shape
    f = f2 // 2
    tt, tc = _tile(s, ROW_TILE, 16), _tile(f, 512)
    nc = f // tc

    def body(g_ref, gh_ref, v_ref, vh_ref, wg_ref, wv_ref, bg_ref, bv_ref, a_ref):
        t = pl.program_id(1)

        def conv(x_ref, h_ref, w_ref, b_ref):
            x = x_ref[...].astype(F32)
            hal = jnp.where(t > 0, h_ref[...].astype(F32)[8:16], 0.0)
            return (w_ref[2:3, :] * x + w_ref[1:2, :] * _shift_down(x, hal, 1)
                    + w_ref[0:1, :] * _shift_down(x, hal, 2) + b_ref[...])

        gate = conv(g_ref, gh_ref, wg_ref, bg_ref)
        val = conv(v_ref, vh_ref, wv_ref, bv_ref)
        a_ref[...] = (_gelu_and_grad(gate)[0] * val).astype(BF16)

    def main(off):
        return pl.BlockSpec((tt, tc), lambda j, t: (t, j + off))

    def halo(off):
        return pl.BlockSpec((HALO, tc), _halo_prev(tt, lambda j, t: j + off))

    def vec(rows, off):
        return pl.BlockSpec((rows, tc), lambda j, t: (0, j + off))

    return _pcall(body, name="ffn_act_fwd", out_shape=jax.ShapeDtypeStruct((s, f), BF16), grid=(nc, s // tt),
                  in_specs=[main(0), halo(0), main(nc), halo(nc), vec(3, 0), vec(3, nc), vec(1, 0), vec(1, nc)],
                  out_specs=pl.BlockSpec((tt, tc), lambda j, t: (t, j)),
                  compiler_params=_params(("parallel", "parallel")))(u, u, u, u, cw, cw, cb, cb)


def _ffn_act_bwd(u, cw, cb, da):
    s, f2 = u.shape
    f = f2 // 2
    tt, tc = _tile(s, ROW_TILE, 16), _tile(f, 512)
    nc, nt = f // tc, s // tt

    def body(g_ref, gh_ref, v_ref, vh_ref, da_ref, wg_ref, wv_ref, bg_ref, bv_ref,
             dug_ref, duv_ref, dwg_ref, dwv_ref, dbg_ref, dbv_ref, carry):
        t = pl.program_id(1)
        first_tile = t == nt - 1

        @pl.when(t == 0)
        def _():
            carry[...] = jnp.zeros_like(carry)

        def shifted(x_ref, h_ref):
            x = x_ref[...].astype(F32)
            hal = jnp.where(first_tile, 0.0, h_ref[...].astype(F32)[8:16])
            return x, _shift_down(x, hal, 1), _shift_down(x, hal, 2)

        def conv(xs, w_ref, b_ref):
            return w_ref[2:3, :] * xs[0] + w_ref[1:2, :] * xs[1] + w_ref[0:1, :] * xs[2] + b_ref[...]

        xg, xv = shifted(g_ref, gh_ref), shifted(v_ref, vh_ref)
        gate, val = conv(xg, wg_ref, bg_ref), conv(xv, wv_ref, bv_ref)
        dav = da_ref[...].astype(F32)
        gl, dgl = _gelu_and_grad(gate)
        dgate, dval = dav * val * dgl, dav * gl

        def back(k, d, xs, w_ref, du_ref, dw_ref, db_ref):
            nxt = carry[k]
            du_ref[...] = (w_ref[2:3, :] * d + w_ref[1:2, :] * _shift_up(d, nxt, 1)
                           + w_ref[0:1, :] * _shift_up(d, nxt, 2)).astype(BF16)
            carry[k] = d[0:8]
            parts = [jnp.sum(d * xs[2 - r], axis=0, keepdims=True) for r in range(3)]
            bpart = jnp.sum(d, axis=0, keepdims=True)

            @pl.when(t == 0)
            def _():
                for r in range(3):
                    dw_ref[r:r + 1, :] = parts[r]
                db_ref[...] = bpart

            @pl.when(t > 0)
            def _():
                for r in range(3):
                    dw_ref[r:r + 1, :] += parts[r]
                db_ref[...] += bpart

        back(0, dgate, xg, wg_ref, dug_ref, dwg_ref, dbg_ref)
        back(1, dval, xv, wv_ref, duv_ref, dwv_ref, dbv_ref)

    def main(off):
        return pl.BlockSpec((tt, tc), lambda j, t: (nt - 1 - t, j + off))

    def halo(off):
        return pl.BlockSpec((HALO, tc), lambda j, t: (jnp.maximum((nt - 1 - t) * (tt // HALO) - 1, 0), j + off))

    def vec(rows, off):
        return pl.BlockSpec((rows, tc), lambda j, t: (0, j + off))

    dug, duv, dwg, dwv, dbg, dbv = _pcall(
        body, name="ffn_act_bwd",
        out_shape=(jax.ShapeDtypeStruct((s, f), BF16), jax.ShapeDtypeStruct((s, f), BF16),
                   jax.ShapeDtypeStruct((3, f), F32), jax.ShapeDtypeStruct((3, f), F32),
                   jax.ShapeDtypeStruct((1, f), F32), jax.ShapeDtypeStruct((1, f), F32)),
        grid=(nc, nt),
        in_specs=[main(0), halo(0), main(nc), halo(nc), main(0), vec(3, 0), vec(3, nc), vec(1, 0), vec(1, nc)],
        out_specs=(main(0), main(0), vec(3, 0), vec(3, 0), vec(1, 0), vec(1, 0)),
        scratch_shapes=[pltpu.VMEM((2, 8, tc), F32)],
        compiler_params=_params(("parallel", "arbitrary")))(u, u, u, u, da, cw, cw, cb, cb)
    return dug, duv, jnp.concatenate([dwg, dwv], axis=1), jnp.concatenate([dbg, dbv], axis=1)


def _chunk_cols(h, kd, vd, heads):
    dk, dv = kd // heads, vd // heads
    return (slice(h * dk, (h + 1) * dk), slice(kd + h * dk, kd + (h + 1) * dk),
            slice(2 * kd + h * dv, 2 * kd + (h + 1) * dv), slice(2 * kd + vd + h * dv, 2 * kd + vd + (h + 1) * dv))


def _rope(x, cos, sin):
    half = x.shape[1] // 2
    x1, x2 = x[:, :half], x[:, half:]
    return jnp.concatenate([x1 * cos - x2 * sin, x2 * cos + x1 * sin], axis=1)


def _unrope(d, cos, sin):
    half = d.shape[1] // 2
    d1, d2 = d[:, :half], d[:, half:]
    return jnp.concatenate([d1 * cos + d2 * sin, d2 * cos - d1 * sin], axis=1)


def _chunk_inputs(gla, h, heads, kd, vd, rows, proj_ref, aux):
    qc, kc, vc, gc = _chunk_cols(h, kd, vd, heads)
    dk = kd // heads
    q = proj_ref[rows, qc].astype(F32)
    k = proj_ref[rows, kc].astype(F32)
    v = proj_ref[rows, vc]
    g = proj_ref[rows, gc].astype(F32)
    c = {}
    if gla:
        z_ref, wgk_ref, bgk_ref = aux
        c['z'] = z_ref[rows, :]
        c['gk'] = _dot_nn(c['z'], wgk_ref[:, qc]) + bgk_ref[:, qc]
        la = (jnp.minimum(c['gk'], 0.0) - jnp.log(1.0 + jnp.exp(-jnp.abs(c['gk'])))) * (1.0 / GLA_GATE_TAU)
        b = _cumsum_rows(la)
        bl = jnp.sum(la, axis=0, keepdims=True)
        q = q * (dk ** -0.5)
    else:
        cos_ref, sin_ref = aux
        c['cos'], c['sin'] = cos_ref[rows, :], sin_ref[rows, :]
        q = _rope(q, c['cos'], c['sin'])
        k = _rope(k, c['cos'], c['sin']) * (dk ** -0.5)
        lg = math.log(1.0 - 2.0 ** (-5.0 - h))
        b = lg * (_row_iota((CHUNK, 1)).astype(F32) + 1.0)
        bl = jnp.full((1, 1), lg * CHUNK, F32)
    eb, enb = jnp.exp(b), jnp.exp(-b)
    c.update(q=q, k=k, v=v, g=g, b=b, bl=bl, eb=eb, enb=enb, ebl=jnp.exp(bl),
             qd=q * eb, kg=k * enb, qg=q * enb, kd=k * eb, ks=k * jnp.exp(bl - b))
    lower = _row_iota((CHUNK, CHUNK)) >= lax.broadcasted_iota(jnp.int32, (CHUNK, CHUNK), 1)
    c['lower'] = lower
    c['A'] = jnp.where(lower, _dot_nt(c['qd'], c['kg']), _dot_nt(c['qg'], c['kd']))
    return c


def _head_norm(gla, o):
    if not gla:
        o = o - jnp.mean(o, axis=-1, keepdims=True)
    r = lax.rsqrt(jnp.mean(o * o, axis=-1, keepdims=True) + RMS_EPS)
    return o * r, r


def _chunk_fwd(gla, proj, aux_arrays, nw, heads, kd, vd, cps):
    s, pw = proj.shape
    dk, dv = kd // heads, vd // heads
    rt = CHUNK * cps
    nb = s // rt
    n_aux = len(aux_arrays)

    def body(*refs):
        proj_ref, aux, nw_ref = refs[0], refs[1:1 + n_aux], refs[1 + n_aux]
        y_ref, o_ref, st_ref, state = refs[2 + n_aux:]

        @pl.when(pl.program_id(0) == 0)
        def _():
            state[...] = jnp.zeros_like(state)

        def chunk(ci, carry):
            rows = pl.ds(pl.multiple_of(ci * CHUNK, CHUNK), CHUNK)
            for h in range(heads):
                c = _chunk_inputs(gla, h, heads, kd, vd, rows, proj_ref, aux)
                vcols = slice(h * dv, (h + 1) * dv)
                st0 = state[h]
                st_ref[ci, h] = st0.astype(BF16)
                o = _dot_nn(c['A'], c['v']) + _dot_nt(c['qd'], st0)
                state[h] = st0 * c['ebl'] + _dot_tn(c['v'], c['ks'])
                oh, _ = _head_norm(gla, o)
                gv = c['g']
                y_ref[rows, vcols] = (oh * nw_ref[:, vcols] * (gv * _sigmoid(gv))).astype(BF16)
                o_ref[rows, vcols] = o.astype(BF16)
            return carry

        lax.fori_loop(0, cps, chunk, 0)

    row = lambda w: pl.BlockSpec((rt, w), lambda n: (n, 0))
    full = lambda a: pl.BlockSpec(a.shape, lambda n: (0,) * a.ndim)
    aux_specs = [row(a.shape[1]) if a.shape[0] == s else full(a) for a in aux_arrays]
    return _pcall(
        body, name="gla_fwd" if gla else "ret_fwd",
        out_shape=(jax.ShapeDtypeStruct((s, vd), BF16), jax.ShapeDtypeStruct((s, vd), BF16),
                   jax.ShapeDtypeStruct((s // CHUNK, heads, dv, dk), BF16)),
        grid=(nb,), in_specs=[row(pw)] + aux_specs + [full(nw)],
        out_specs=(row(vd), row(vd), pl.BlockSpec((cps, heads, dv, dk), lambda n: (n, 0, 0, 0))),
        scratch_shapes=[pltpu.VMEM((heads, dv, dk), F32)],
        compiler_params=_params(("arbitrary",)))(proj, *aux_arrays, nw)


def _chunk_bwd(gla, proj, aux_arrays, nw, o_st, st, dy, heads, kd, vd, cps):
    s, pw = proj.shape
    dk, dv = kd // heads, vd // heads
    rt = CHUNK * cps
    nb = s // rt
    n_aux = len(aux_arrays)

    def body(*refs):
        proj_ref, aux, nw_ref = refs[0], refs[1:1 + n_aux], refs[1 + n_aux]
        o_ref, st_ref, dy_ref = refs[2 + n_aux:5 + n_aux]
        outs = refs[5 + n_aux:]
        dp_ref, dnw_ref = outs[0], outs[1]
        if gla:
            dz_ref, dwgk_ref, dbgk_ref, dstate = outs[2:]
        else:
            dstate = outs[2]

        @pl.when(pl.program_id(0) == 0)
        def _():
            dstate[...] = jnp.zeros_like(dstate)
            dnw_ref[...] = jnp.zeros_like(dnw_ref)
            if gla:
                dwgk_ref[...] = jnp.zeros_like(dwgk_ref)
                dbgk_ref[...] = jnp.zeros_like(dbgk_ref)

        def chunk(i, carry):
            ci = cps - 1 - i
            rows = pl.ds(pl.multiple_of(ci * CHUNK, CHUNK), CHUNK)
            dz = jnp.zeros((CHUNK, GATE_PAD), F32)
            for h in range(heads):
                c = _chunk_inputs(gla, h, heads, kd, vd, rows, proj_ref, aux)
                qc, kc, vc, gc = _chunk_cols(h, kd, vd, heads)
                vcols = slice(h * dv, (h + 1) * dv)
                o = o_ref[rows, vcols].astype(F32)
                oh, r = _head_norm(gla, o)
                dyv = dy_ref[rows, vcols].astype(F32)
                gv = c['g']
                sg = _sigmoid(gv)
                nwv = nw_ref[:, vcols]
                dp_ref[rows, gc] = (dyv * oh * nwv * (sg * (1.0 + gv * (1.0 - sg)))).astype(BF16)
                dn = dyv * (gv * sg)
                dnw_ref[:, vcols] += jnp.sum(dn * oh, axis=0, keepdims=True)
                doh = dn * nwv
                do = doh - oh * jnp.mean(doh * oh, axis=-1, keepdims=True)
                if not gla:
                    do = do - jnp.mean(doh, axis=-1, keepdims=True)
                do = r * do
                st0 = st_ref[ci, h]
                dst1 = dstate[h]
                v = c['v']
                da = _dot_nt(do, v)
                dal = jnp.where(c['lower'], da, 0.0)
                dau = da - dal
                dp_ref[rows, vc] = (_dot_tn(c['A'], do) + _dot_nt(c['ks'], dst1)).astype(BF16)
                dqd = _dot_nn(dal, c['kg']) + _dot_nn(do, st0)
                dkg = _dot_tn(dal, c['qd'])
                dqg = _dot_nn(dau, c['kd'])
                dkd = _dot_tn(dau, c['qg'])
                dks = _dot_nn(v, dst1)
                dstate[h] = _dot_tn(do, c['qd']) + dst1 * c['ebl']
                dq = dqd * c['eb'] + dqg * c['enb']
                dkk = dkg * c['enb'] + dkd * c['eb'] + dks * jnp.exp(c['bl'] - c['b'])
                if gla:
                    db = dqd * c['qd'] - dkg * c['kg'] - dqg * c['qg'] + dkd * c['kd'] - dks * c['ks']
                    dbl = (jnp.sum(dks * c['ks'], axis=0, keepdims=True)
                           + c['ebl'] * jnp.sum(dst1 * st0.astype(F32), axis=0, keepdims=True))
                    db = db + jnp.where(_row_iota(db.shape) == CHUNK - 1, dbl, 0.0)
                    dgk = _rev_cumsum_rows(db) * (1.0 / GLA_GATE_TAU) / (1.0 + jnp.exp(c['gk']))
                    _, wgk_ref, _ = aux
                    dz = dz + _dot_nt(dgk, wgk_ref[:, qc])
                    dwgk_ref[:, qc] += _dot_tn(c['z'], dgk)
                    dbgk_ref[:, qc] += jnp.sum(dgk, axis=0, keepdims=True)
                    dp_ref[rows, qc] = (dq * (dk ** -0.5)).astype(BF16)
                    dp_ref[rows, kc] = dkk.astype(BF16)
                else:
                    dp_ref[rows, qc] = _unrope(dq, c['cos'], c['sin']).astype(BF16)
                    dp_ref[rows, kc] = (_unrope(dkk, c['cos'], c['sin']) * (dk ** -0.5)).astype(BF16)
            if gla:
                dz_ref[rows, :] = dz
            return carry

        lax.fori_loop(0, cps, chunk, 0)

    row = lambda w: pl.BlockSpec((rt, w), lambda n: (nb - 1 - n, 0))
    full = lambda a: pl.BlockSpec(a.shape, lambda n: (0,) * a.ndim)
    aux_specs = [row(a.shape[1]) if a.shape[0] == s else full(a) for a in aux_arrays]
    out_shape = [jax.ShapeDtypeStruct((s, pw), BF16), jax.ShapeDtypeStruct((1, vd), F32)]
    out_specs = [row(pw), full(nw)]
    if gla:
        wgk, bgk = aux_arrays[1], aux_arrays[2]
        out_shape += [jax.ShapeDtypeStruct((s, GATE_PAD), F32), jax.ShapeDtypeStruct(wgk.shape, F32),
                      jax.ShapeDtypeStruct(bgk.shape, F32)]
        out_specs += [row(GATE_PAD), full(wgk), full(bgk)]
    return _pcall(
        body, name="gla_bwd" if gla else "ret_bwd", out_shape=tuple(out_shape), grid=(nb,),
        in_specs=[row(pw)] + aux_specs + [full(nw), row(vd),
                                          pl.BlockSpec((cps, heads, dv, dk), lambda n: (nb - 1 - n, 0, 0, 0)), row(vd)],
        out_specs=tuple(out_specs), scratch_shapes=[pltpu.VMEM((heads, dv, dk), F32)],
        compiler_params=_params(("arbitrary",)))(proj, *aux_arrays, nw, o_st, st, dy)


def _lru_gates(xc, wga_ref, bga_ref, wgx_ref, bgx_ref, lam_ref):
    r = _sigmoid(_dot_nn(xc, wga_ref[0]) + bga_ref[...])
    i = _sigmoid(_dot_nn(xc, wgx_ref[0]) + bgx_ref[...])
    sp = _softplus(-lam_ref[...])
    la = -LRU_C * r * sp
    return r, i, sp, la, jnp.exp(la), jnp.sqrt(_neg_expm1(2.0 * la))


def _lru_specs(w, nbk, tt, tmap):
    main = lambda off: pl.BlockSpec((tt, LRU_BLOCK_W), lambda n, t: (tmap(t), n + off))
    halo = pl.BlockSpec((HALO, LRU_BLOCK_W), lambda n, t: (jnp.maximum(tmap(t) * (tt // HALO) - 1, 0), n))
    vec = lambda rows: pl.BlockSpec((rows, LRU_BLOCK_W), lambda n, t: (0, n))
    mat = pl.BlockSpec((1, LRU_BLOCK_W, LRU_BLOCK_W), lambda n, t: (n, 0, 0))
    return main, halo, vec, mat


def _lru_fwd(proj, cw, cb, wga, bga, wgx, bgx, lam):
    s, w2 = proj.shape
    w = w2 // 2
    nbk, tt = w // LRU_BLOCK_W, _tile(s, ROW_TILE, 16)
    main, halo, vec, mat = _lru_specs(w, nbk, tt, lambda t: t)

    def body(x_ref, xh_ref, y_ref, cw_ref, cb_ref, wga_ref, bga_ref, wgx_ref, bgx_ref, lam_ref,
             out_ref, hs_ref, hcar):
        t = pl.program_id(1)
        x = x_ref[...]
        hal = jnp.where(t > 0, xh_ref[8:16, :], 0.0)
        xc = (cw_ref[3:4, :] * x + cw_ref[2:3, :] * _shift_down(x, hal, 1) + cw_ref[1:2, :] * _shift_down(x, hal, 2)
              + cw_ref[0:1, :] * _shift_down(x, hal, 3) + cb_ref[...])
        r, i, sp, la, a, mlt = _lru_gates(xc, wga_ref, bga_ref, wgx_ref, bgx_ref, lam_ref)

        @pl.when(t == 0)
        def _():
            hcar[...] = jnp.zeros_like(hcar)

        h = _scan_fwd(a, xc * i * mlt, hcar[...])
        hcar[...] = _pick_row(h, tt - 1)
        hs_ref[...] = h
        out_ref[...] = (h * _gelu_and_grad(y_ref[...])[0]).astype(BF16)

    return _pcall(body, name="lru_fwd",
                  out_shape=(jax.ShapeDtypeStruct((s, w), BF16), jax.ShapeDtypeStruct((s, w), F32)),
                  grid=(nbk, s // tt),
                  in_specs=[main(0), halo, main(nbk), vec(4), vec(1), mat, vec(1), mat, vec(1), vec(1)],
                  out_specs=(main(0), main(0)), scratch_shapes=[pltpu.VMEM((1, LRU_BLOCK_W), F32)],
                  compiler_params=_params(("parallel", "arbitrary")))(proj, proj, proj, cw, cb, wga, bga, wgx, bgx, lam)


def _lru_bwd(proj, hs, dout, cw, cb, wga, bga, wgx, bgx, lam):
    s, w2 = proj.shape
    w = w2 // 2
    nbk, tt = w // LRU_BLOCK_W, _tile(s, ROW_TILE, 16)
    nt = s // tt
    main, halo, vec, mat = _lru_specs(w, nbk, tt, lambda t: nt - 1 - t)

    def body(x_ref, xh_ref, y_ref, hs_ref, hh_ref, do_ref, cw_ref, cb_ref, wga_ref, bga_ref, wgx_ref, bgx_ref,
             lam_ref, dx_ref, dy_ref, dcw_ref, dcb_ref, dbga_ref, dbgx_ref, dlam_ref, dwga_ref, dwgx_ref,
             dhcar, dxcar):
        t = pl.program_id(1)
        first_tile = t == nt - 1

        @pl.when(t == 0)
        def _():
            dhcar[...] = jnp.zeros_like(dhcar)
            dxcar[...] = jnp.zeros_like(dxcar)

        x = x_ref[...]
        hal = jnp.where(first_tile, 0.0, xh_ref[8:16, :])
        xs = [x, _shift_down(x, hal, 1), _shift_down(x, hal, 2), _shift_down(x, hal, 3)]
        xc = cw_ref[3:4, :] * xs[0] + cw_ref[2:3, :] * xs[1] + cw_ref[1:2, :] * xs[2] + cw_ref[0:1, :] * xs[3] + cb_ref[...]
        r, i, sp, la, a, mlt = _lru_gates(xc, wga_ref, bga_ref, wgx_ref, bgx_ref, lam_ref)
        h = hs_ref[...]
        hprev = _shift_down(h, jnp.where(first_tile, 0.0, hh_ref[8:16, :]), 1)
        gl, dgl = _gelu_and_grad(y_ref[...])
        dov = do_ref[...].astype(F32)
        dy_ref[...] = (dov * h * dgl).astype(BF16)
        row = _row_iota(a.shape)
        coef = jnp.where(row == tt - 1, 1.0, pltpu.roll(a, tt - 1, 0))
        dh = _scan_rev(coef, dov * gl, dhcar[...])
        dhcar[...] = _pick_row(a * dh, 0)
        dxc = dh * i * mlt
        di = dh * xc * mlt
        dm = dh * xc * i
        dla = dh * hprev * a - dm * jnp.exp(2.0 * la) / mlt
        dpa = dla * (-LRU_C * sp) * r * (1.0 - r)
        dpx = di * i * (1.0 - i)
        dxc = dxc + _dot_nt(dpa, wga_ref[0]) + _dot_nt(dpx, wgx_ref[0])
        nxt = dxcar[...]
        dx_ref[...] = (cw_ref[3:4, :] * dxc + cw_ref[2:3, :] * _shift_up(dxc, nxt, 1)
                       + cw_ref[1:2, :] * _shift_up(dxc, nxt, 2) + cw_ref[0:1, :] * _shift_up(dxc, nxt, 3)).astype(BF16)
        dxcar[...] = dxc[0:8]
        colsum = lambda v: jnp.sum(v, axis=0, keepdims=True)
        parts = [(dcb_ref, colsum(dxc)), (dbga_ref, colsum(dpa)), (dbgx_ref, colsum(dpx)),
                 (dlam_ref, colsum(dla * LRU_C * r) * _sigmoid(-lam_ref[...]))]
        wparts = [colsum(dxc * xs[3 - k]) for k in range(4)]
        dwa, dwx = _dot_tn(xc, dpa), _dot_tn(xc, dpx)

        @pl.when(t == 0)
        def _():
            for ref, val in parts:
                ref[...] = val
            for k in range(4):
                dcw_ref[k:k + 1, :] = wparts[k]
            dwga_ref[0] = dwa
            dwgx_ref[0] = dwx

        @pl.when(t > 0)
        def _():
            for ref, val in parts:
                ref[...] += val
            for k in range(4):
                dcw_ref[k:k + 1, :] += wparts[k]
            dwga_ref[0] += dwa
            dwgx_ref[0] += dwx

    sd = jax.ShapeDtypeStruct
    return _pcall(
        body, name="lru_bwd",
        out_shape=(sd((s, w), BF16), sd((s, w), BF16), sd((4, w), F32), sd((1, w), F32), sd((1, w), F32),
                   sd((1, w), F32), sd((1, w), F32), sd(wga.shape, F32), sd(wgx.shape, F32)),
        grid=(nbk, nt),
        in_specs=[main(0), halo, main(nbk), main(0), halo, main(0), vec(4), vec(1), mat, vec(1), mat, vec(1), vec(1)],
        out_specs=(main(0), main(0), vec(4), vec(1), vec(1), vec(1), vec(1), mat, mat),
        scratch_shapes=[pltpu.VMEM((1, LRU_BLOCK_W), F32), pltpu.VMEM((8, LRU_BLOCK_W), F32)],
        compiler_params=_params(("parallel", "arbitrary")))(proj, proj, proj, hs, hs, dout, cw, cb, wga, bga, wgx, bgx, lam)


def _adamw(parts, w, m, v, layer=None, prev=None):
    shape = w.shape
    cols = shape[-1]
    layers = 1 if layer is None else shape[0]
    w3, m3, v3 = (a.reshape(layers, -1, cols) for a in (w, m, v))
    rows = w3.shape[1]
    tr = _tile(rows, max(16, (1 << 17) // cols // 16 * 16), 16)
    blk = pl.BlockSpec((None, tr, cols), lambda i: (layer or 0, i, 0))
    p_args, p_specs = [], []
    for part in parts:
        if isinstance(part, tuple):
            stack, idx = part
            p_args.append(stack.reshape(stack.shape[0], rows, cols))
            p_specs.append(pl.BlockSpec((None, tr, cols), lambda i, idx=idx: (idx, i, 0)))
        else:
            p_args.append(part.reshape(rows, cols))
            p_specs.append(pl.BlockSpec((tr, cols), lambda i: (i, 0)))
    n_parts = len(parts)
    n_prev = 0 if prev is None else 4

    def body(*refs):
        w_ref, m_ref, v_ref = refs[n_parts:n_parts + 3]
        g_ref, d_ref, nm_ref, nv_ref = refs[n_parts + 3 + n_prev:]
        gv = refs[0][...].astype(F32)
        for p_ref in refs[1:n_parts]:
            gv = gv + p_ref[...].astype(F32)
        g_ref[...] = gv
        nm = ADAM_B1 * m_ref[...] + (1.0 - ADAM_B1) * gv
        nv = ADAM_B2 * v_ref[...] + (1.0 - ADAM_B2) * (gv * gv)
        m_hat = nm / (1.0 - ADAM_B1 ** ADAM_STEP)
        v_hat = nv / (1.0 - ADAM_B2 ** ADAM_STEP)
        d_ref[...] = -ADAM_LR * (m_hat / (jnp.sqrt(v_hat) + ADAM_EPS) + ADAM_WD * w_ref[...])
        nm_ref[...] = nm
        nv_ref[...] = nv

    prev_args = [] if prev is None else [a.reshape(layers, rows, cols) for a in prev]
    n_in = n_parts + 3
    out = _pcall(body, name="adamw", out_shape=tuple(jax.ShapeDtypeStruct((layers, rows, cols), F32) for _ in range(4)),
                 grid=(rows // tr,), in_specs=p_specs + [blk] * 3 + [pl.BlockSpec(memory_space=pl.ANY)] * n_prev,
                 out_specs=(blk,) * 4, input_output_aliases={n_in + q: q for q in range(n_prev)},
                 compiler_params=_params(("parallel",)))(*p_args, w3, m3, v3, *prev_args)
    return tuple(o.reshape(shape) for o in out)


def _sum_list(arrs, out_dtype):
    shape = arrs[0].shape
    cols = shape[-1]
    flat = [a.reshape(-1, cols) for a in arrs]
    rows = flat[0].shape[0]
    tr = _tile(rows, max(16, (1 << 18) // cols // 16 * 16), 16)

    def body(*refs):
        acc = refs[0][...].astype(F32)
        for p_ref in refs[1:-1]:
            acc = acc + p_ref[...].astype(F32)
        refs[-1][...] = acc.astype(out_dtype)

    blk = pl.BlockSpec((tr, cols), lambda i: (i, 0))
    out = _pcall(body, name="sum_list", out_shape=jax.ShapeDtypeStruct((rows, cols), out_dtype), grid=(rows // tr,),
                 in_specs=[blk] * len(flat), out_specs=blk, compiler_params=_params(("parallel",)))(*flat)
    return out.reshape(shape)


def _mesh_pos():
    return lax.axis_index("x"), lax.axis_index("y"), lax.axis_index("c")


def _all_gather(arrs):
    n = len(arrs)

    def body(*refs):
        x_refs, out_refs = refs[:n], refs[n:2 * n]
        send_sems, recv_sems, local_sems = refs[2 * n:]
        x, y, c = _mesh_pos()
        me, sibling = (x, y, c), (x, y, 1 - c)
        chips = [(1 - x, y), (x, 1 - y), (1 - x, 1 - y)]

        def slot(a, px, py, pc):
            return out_refs[a].at[4 * px + 2 * py + pc]

        def copy(a, k, block, to, src=None):
            return pltpu.make_async_remote_copy(
                src_ref=slot(a, *block) if src is None else src, dst_ref=slot(a, *block),
                send_sem=send_sems.at[7 * a + k], recv_sem=recv_sems.at[7 * a + k], device_id=to, device_id_type=MESH)

        mine = [pltpu.make_async_copy(x_refs[a], slot(a, *me), local_sems.at[a]) for a in range(n)]
        for cp in mine:
            cp.start()
        first = []
        for j, chip in enumerate(chips):
            first += [copy(a, 1 + j, me, (*chip, c), src=x_refs[a]) for a in range(n)]
        first += [copy(a, 0, me, sibling, src=x_refs[a]) for a in range(n)]
        for cp in first:
            cp.start()
        passed = []
        for j, chip in enumerate(chips):
            for a in range(n):
                copy(a, 1 + j, (*chip, c), me).wait_recv()
                passed.append(copy(a, 4 + j, (*chip, c), sibling))
                passed[-1].start()
        for a in range(n):
            copy(a, 0, sibling, me).wait_recv()
        for j, chip in enumerate(chips):
            for a in range(n):
                copy(a, 4 + j, (*chip, 1 - c), me).wait_recv()
        for cp in first + passed:
            cp.wait_send()
        for cp in mine:
            cp.wait()

    hbm = pl.BlockSpec(memory_space=pl.ANY)
    return _pcall(body, name="all_gather",
                  out_shape=tuple(jax.ShapeDtypeStruct((N_DEV,) + a.shape, a.dtype) for a in arrs),
                  in_specs=[hbm] * n, out_specs=(hbm,) * n,
                  scratch_shapes=[pltpu.SemaphoreType.DMA((7 * n,)), pltpu.SemaphoreType.DMA((7 * n,)),
                                  pltpu.SemaphoreType.DMA((n,))],
                  compiler_params=pltpu.CompilerParams(has_side_effects=True))(*arrs)


def _peers():
    x, y, c = _mesh_pos()
    out = []
    for k in (1, 2, 4, 3, 5, 6, 7):
        px, py, pc = (x + (k >> 2)) % 2, (y + ((k >> 1) & 1)) % 2, (c + (k & 1)) % 2
        out.append((k, (px, py, pc), 4 * px + 2 * py + pc))
    return out, 4 * x + 2 * y + c


def _peer_copies(n, scatter, src_refs, land_refs, send_sems, recv_sems):
    peers, me = _peers()
    copies = []
    for k, peer, peer_id in peers:
        for a in range(n):
            copies.append(pltpu.make_async_remote_copy(
                src_ref=src_refs[a].at[peer_id] if scatter else src_refs[a],
                dst_ref=land_refs[a].at[k - 1] if scatter else land_refs[a].at[me],
                send_sem=send_sems.at[7 * a + k - 1], recv_sem=recv_sems.at[7 * a + k - 1],
                device_id=peer, device_id_type=MESH))
    return copies


_HBM = pl.BlockSpec(memory_space=pltpu.HBM)
_SEM = pl.BlockSpec(memory_space=pltpu.SEMAPHORE)
_EFFECT = pltpu.SideEffectType.DATAFLOW_SIDE_EFFECTING


def _in_hbm(a):
    return pltpu.with_memory_space_constraint(a, pltpu.HBM)


def _exchange_start(srcs, lands, scatter, name):
    n = len(srcs)

    def body(*refs):
        send_sems, recv_sems, token = refs[2 * n], refs[2 * n + 1], refs[-1]
        for cp in _peer_copies(n, scatter, refs[:n], refs[n:2 * n], send_sems, recv_sems):
            cp.start()
        token[...] = jnp.zeros_like(token)

    thru = [pltpu.HBM(a.shape, a.dtype) for a in list(srcs) + list(lands)]
    out = _pcall(body, name=name,
                 out_shape=(pltpu.SemaphoreType.DMA((7 * n,)), pltpu.SemaphoreType.DMA((7 * n,)), *thru,
                            jax.ShapeDtypeStruct((8, 128), F32)),
                 in_specs=[_HBM] * (2 * n), out_specs=(_SEM, _SEM, *([_HBM] * (2 * n)), pl.BlockSpec(memory_space=pltpu.VMEM)),
                 input_output_aliases={i: 2 + i for i in range(2 * n)},
                 compiler_params=pltpu.CompilerParams(has_side_effects=_EFFECT))(*[_in_hbm(a) for a in list(srcs) + list(lands)])
    return out[0], out[1], out[2:2 + n], out[2 + n:2 + 2 * n], out[-1]


def _exchange_wait(send_sems, recv_sems, srcs, lands, after, scatter, name):
    n = len(srcs)

    def body(*refs):
        for cp in _peer_copies(n, scatter, refs[:n], refs[n:2 * n], refs[2 * n], refs[2 * n + 1]):
            cp.wait_send()
            cp.wait_recv()

    thru = tuple(pltpu.HBM(a.shape, a.dtype) for a in list(srcs) + list(lands))
    out = _pcall(body, name=name, out_shape=thru,
                 in_specs=[_HBM] * (2 * n) + [_SEM, _SEM, pl.BlockSpec(memory_space=pl.ANY)], out_specs=(_HBM,) * (2 * n),
                 input_output_aliases={i: i for i in range(2 * n)},
                 compiler_params=pltpu.CompilerParams(has_side_effects=_EFFECT))(*srcs, *lands, send_sems, recv_sems, after)
    return out[:n], out[n:]


def _pack(arrays, dtype):
    lead = arrays[0].shape[:-1]
    flat = jnp.concatenate([a.astype(dtype) for a in arrays], axis=-1)
    n = flat.shape[-1]
    unit = 16 * COMM_LANES
    pad = (-n) % unit
    flat = jnp.pad(flat, [(0, 0)] * len(lead) + [(0, pad)])
    return flat.reshape(lead + ((n + pad) // COMM_LANES, COMM_LANES))


def _unpack(packed, shapes):
    lead = packed.shape[:-2]
    flat = packed.reshape(lead + (-1,))
    out, off = [], 0
    for shp in shapes:
        n = math.prod(shp)
        out.append(flat[..., off:off + n].reshape(lead + tuple(shp)))
        off += n
    return out


def _unshard(g, ax):
    shp = list(g.shape[1:])
    shp[ax] *= N_DEV
    return jnp.moveaxis(g, 0, ax).reshape(shp)


def _split(full, ax):
    shp = list(full.shape)
    r = full.reshape(shp[:ax] + [N_DEV, shp[ax] // N_DEV] + shp[ax + 1:])
    return jnp.moveaxis(r, ax, 0)


def _rope_tables(s, dk):
    half = dk // 2
    inv = ROPE_BASE ** (-jnp.arange(half, dtype=F32) / half)
    ang = jnp.arange(s, dtype=F32)[:, None] * inv[None, :]
    return jnp.cos(ang), jnp.sin(ang)


def _layer_big(i):
    kind, j = i % 3, i // 3
    mixer = [['gla_w_in', 'gla_w_out'], ['lru_w_in', 'lru_w_ga', 'lru_w_gx', 'lru_w_out'], ['ret_w_in', 'ret_w_out']][kind]
    return [(n, j) for n in mixer] + [('ffn_w_up', i), ('ffn_w_down', i)]


def _local_step(x, target, p, layer_weights, layer_grads, token):
    s, d = x.shape
    depth = p['norm_mix_w'].shape[0]
    gla_kd, gla_vd = d // 2, d
    ret_kd, ret_vd = d, 2 * d
    cos, sin = _rope_tables(s, ret_kd // RET_HEADS)
    row = lambda v: v.reshape(1, -1)
    saved = []
    cur, gb = {}, {}

    def view(name, layer=0, dev0=0, ndev=N_DEV, arr=None):
        return _View(cur[name] if arr is None else arr, 'col' if SHARD_AX[name] == 2 else 'row', 0, dev0, ndev)

    def dw_into(name, layer, a, b, label, dev0=0, ndev=N_DEV):
        gb[name] = _mm(a, b, 'tn', BF16, name=label, out=view(name, 0, dev0, ndev, gb[name]))

    def gla_weights(j):
        w_in = cur['gla_w_in']
        w_main, w_z = w_in[:, :2 * gla_kd + 2 * gla_vd], w_in[:, 2 * gla_kd + 2 * gla_vd:]
        w_z = jnp.pad(w_z, ((0, 0), (0, GATE_PAD - GLA_GATE_RANK)))
        w_gk = jnp.pad(p['gla_w_gk'][j], ((0, GATE_PAD - GLA_GATE_RANK), (0, 0)))
        return w_main, w_z, w_gk, row(p['gla_b_gk'][j]), row(p['gla_norm_w'][j]), view('gla_w_out', j)

    def lru_weights(j):
        return (view('lru_w_in', j), p['lru_conv_w'][j], row(p['lru_conv_b'][j]), cur['lru_w_ga'], row(p['lru_b_ga'][j]),
                cur['lru_w_gx'], row(p['lru_b_gx'][j]), row(p['lru_lambda'][j]), view('lru_w_out', j))

    for i in range(depth):
        kind, j = i % 3, i // 3
        cur = layer_weights(i, x)
        h = _rms_fwd(x, row(p['norm_mix_w'][i]) + (token if i == 0 else 0.0))
        if kind == 0:
            w_main, w_z, w_gk, b_gk, nw, w_out = gla_weights(j)
            proj = _mm(h, w_main, 'nn', BF16, name="gla_in")
            z = _mm(h, w_z, 'nn', F32, name="gla_z")
            y, o_st, st = _chunk_fwd(True, proj, (z, w_gk, b_gk), nw, GLA_HEADS, gla_kd, gla_vd, 4)
            mix = (proj, z, o_st, st, y)
        elif kind == 1:
            w_in, cw, cb, wga, bga, wgx, bgx, lam, w_out = lru_weights(j)
            proj = _mm(h, w_in, 'nn', F32, name="lru_in")
            y, hs = _lru_fwd(proj, cw, cb, wga, bga, wgx, bgx, lam)
            mix = (proj, hs, y)
        else:
            nw, w_out = row(p['ret_norm_w'][j]), view('ret_w_out', j)
            proj = _mm(h, view('ret_w_in', j), 'nn', BF16, name="ret_in")
            y, o_st, st = _chunk_fwd(False, proj, (cos, sin), nw, RET_HEADS, ret_kd, ret_vd, 2)
            mix = (proj, o_st, st, y)
        x_mid = _mm(y, w_out, 'nn', F32, res=x, name="mix_out")
        h2 = _rms_fwd(x_mid, row(p['norm_ffn_w'][i]))
        u = _mm(h2, view('ffn_w_up', i), 'nn', BF16, name="ffn_up")
        act = _ffn_act_fwd(u, p['ffn_conv_w'][i], row(p['ffn_conv_b'][i]))
        x_out = _mm(act, view('ffn_w_down', i), 'nn', F32, res=x_mid, name="ffn_down")
        saved.append((x, h, mix, x_mid, h2, u, act, cur))
        x = x_out

    loss, dx, dxb, dw = _final_loss(x, row(p['norm_out_w']), target)
    g = {n: [None] * v.shape[0] for n, v in p.items() if n != 'norm_out_w'}
    g['norm_out_w'] = dw.reshape(-1)
    half = N_DEV // 2
    token = 0.0

    for i in reversed(range(depth)):
        kind, j = i % 3, i // 3
        x_in, h, mix, x_mid, h2, u, act, cur = saved[i]
        gb = {n: jnp.zeros(cur[n].shape, BF16) for n, _ in _layer_big(i) if n in GATHERED}
        dw_into('ffn_w_down', i, act, dxb, "ffn_down_dw")
        dact = _mm(dxb, view('ffn_w_down', i), 'nt', BF16, name="ffn_down_dx")
        dug, duv, dcw, dcb = _ffn_act_bwd(u, p['ffn_conv_w'][i], row(p['ffn_conv_b'][i]), dact)
        g['ffn_conv_w'][i], g['ffn_conv_b'][i] = dcw, dcb.reshape(-1)
        dw_into('ffn_w_up', i, h2, dug, "ffn_up_dw", 0, half)
        dw_into('ffn_w_up', i, h2, duv, "ffn_up_dw", half, half)
        dh2 = _mm(dug, view('ffn_w_up', i, 0, half), 'nt', F32, name="ffn_up_dx")
        dh2 = _mm(duv, view('ffn_w_up', i, half, half), 'nt', BF16, res=dh2, name="ffn_up_dx")
        dx, dxb, dnw = _rms_bwd(x_mid, row(p['norm_ffn_w'][i]) + token, dh2, dx)
        g['norm_ffn_w'][i] = dnw.reshape(-1)
        if kind == 0:
            w_main, w_z, w_gk, b_gk, nw, w_out = gla_weights(j)
            proj, z, o_st, st, y = mix
            dw_into('gla_w_out', j, y, dxb, "mix_out_dw")
            dy = _mm(dxb, w_out, 'nt', BF16, name="mix_out_dx")
            dproj, dnw, dz, dwgk, dbgk = _chunk_bwd(True, proj, (z, w_gk, b_gk), nw, o_st, st, dy,
                                                    GLA_HEADS, gla_kd, gla_vd, 4)
            g['gla_norm_w'][j], g['gla_b_gk'][j] = dnw.reshape(-1), dbgk.reshape(-1)
            g['gla_w_gk'][j] = dwgk[:GLA_GATE_RANK]
            dw_main = _mm(h, dproj, 'tn', F32, name="gla_in_dw")
            dw_z = _mm(h, dz, 'tn', F32, name="gla_z_dw")
            dw_in = jnp.concatenate([dw_main, dw_z[:, :GLA_GATE_RANK]], axis=1)
            gb['gla_w_in'] = _split(dw_in, 1).astype(BF16)
            dh = _mm(dproj, w_main, 'nt', F32, name="gla_in_dx")
            dh = _mm(dz, w_z, 'nt', BF16, res=dh, name="gla_z_dx")
        elif kind == 1:
            w_in, cw, cb, wga, bga, wgx, bgx, lam, w_out = lru_weights(j)
            proj, hs, y = mix
            dw_into('lru_w_out', j, y, dxb, "mix_out_dw")
            dy = _mm(dxb, w_out, 'nt', BF16, name="mix_out_dx")
            dxb, dyb, dcw, dcb, dbga, dbgx, dlam, dwga, dwgx = _lru_bwd(proj, hs, dy, cw, cb, wga, bga, wgx, bgx, lam)
            g['lru_conv_w'][j], g['lru_conv_b'][j] = dcw, dcb.reshape(-1)
            g['lru_b_ga'][j], g['lru_b_gx'][j], g['lru_lambda'][j] = dbga.reshape(-1), dbgx.reshape(-1), dlam.reshape(-1)
            gb['lru_w_ga'], gb['lru_w_gx'] = _split(dwga, 1).astype(BF16), _split(dwgx, 1).astype(BF16)
            dproj = jnp.concatenate([dxb, dyb], axis=1)
            dw_into('lru_w_in', j, h, dproj, "lru_in_dw")
            dh = _mm(dproj, w_in, 'nt', BF16, name="lru_in_dx")
        else:
            nw, w_out = row(p['ret_norm_w'][j]), view('ret_w_out', j)
            proj, o_st, st, y = mix
            dw_into('ret_w_out', j, y, dxb, "mix_out_dw")
            dy = _mm(dxb, w_out, 'nt', BF16, name="mix_out_dx")
            dproj, dnw = _chunk_bwd(False, proj, (cos, sin), nw, o_st, st, dy, RET_HEADS, ret_kd, ret_vd, 2)
            g['ret_norm_w'][j] = dnw.reshape(-1)
            dw_into('ret_w_in', j, h, dproj, "ret_in_dw")
            dh = _mm(dproj, view('ret_w_in', j), 'nt', BF16, name="ret_in_dx")
        dx, dxb, dnw = _rms_bwd(x_in, row(p['norm_mix_w'][i]), dh, dx)
        g['norm_mix_w'][i] = dnw.reshape(-1)
        token = layer_grads(i, gb)

    grads = {n: (v if n == 'norm_out_w' else jnp.stack(v)) for n, v in g.items()}
    return loss[0, 0], dx, grads


def kernel(x, norm_mix_w, norm_ffn_w, norm_out_w, gla_w_in, gla_w_gk, gla_b_gk, gla_norm_w, gla_w_out, lru_w_in, lru_conv_w, lru_conv_b, lru_w_ga, lru_b_ga, lru_w_gx, lru_b_gx, lru_lambda, lru_w_out, ret_w_in, ret_norm_w, ret_w_out, ffn_w_up, ffn_conv_w, ffn_conv_b, ffn_w_down, loss_target, m_norm_mix_w, m_norm_ffn_w, m_norm_out_w, m_gla_w_in, m_gla_w_gk, m_gla_b_gk, m_gla_norm_w, m_gla_w_out, m_lru_w_in, m_lru_conv_w, m_lru_conv_b, m_lru_w_ga, m_lru_b_ga, m_lru_w_gx, m_lru_b_gx, m_lru_lambda, m_lru_w_out, m_ret_w_in, m_ret_norm_w, m_ret_w_out, m_ffn_w_up, m_ffn_conv_w, m_ffn_conv_b, m_ffn_w_down, v_norm_mix_w, v_norm_ffn_w, v_norm_out_w, v_gla_w_in, v_gla_w_gk, v_gla_b_gk, v_gla_norm_w, v_gla_w_out, v_lru_w_in, v_lru_conv_w, v_lru_conv_b, v_lru_w_ga, v_lru_b_ga, v_lru_w_gx, v_lru_b_gx, v_lru_lambda, v_lru_w_out, v_ret_w_in, v_ret_norm_w, v_ret_w_out, v_ffn_w_up, v_ffn_conv_w, v_ffn_conv_b, v_ffn_w_down):
    given = dict(locals())
    w = {n: given[n] for n in WEIGHTS}
    me_x, me_y, me_c = _mesh_pos()
    me = 4 * me_x + 2 * me_y + me_c

    depth = norm_mix_w.shape[0]

    def shards(i):
        return [w[n][j].astype(BF16) for n, j in _layer_big(i)]

    got0 = _all_gather(shards(0) + [_pack([w[n].reshape(-1) for n in SMALL_SHARDED], F32)])
    p = {n: w[n] for n in REPLICATED}
    for n, blk in zip(SMALL_SHARDED, _unpack(got0[-1], [w[n].shape for n in SMALL_SHARDED])):
        p[n] = _unshard(blk, SHARD_AX[n])
    gathers, token = {}, 0.0
    for i in range(1, depth):
        srcs = shards(i)
        lands = [lax.dynamic_update_index_in_dim(jnp.zeros((N_DEV,) + a.shape, BF16), a, me, 0) for a in srcs]
        *gathers[i], tok = _exchange_start(srcs, lands, False, "gather_start_%d" % i)
        token = token + tok[0, 0]

    def layer_weights(i, after):
        blocks = got0[:-1] if i == 0 else _exchange_wait(*gathers[i], after, False, "gather_wait_%d" % i)[1]
        out = {}
        for (n, _), blk in zip(_layer_big(i), blocks):
            out[n] = blk.reshape((N_DEV, 1) + blk.shape[1:]) if n in GATHERED else _unshard(blk, SHARD_AX[n] - 1)
        return out

    scatters = {}

    def layer_grads(i, gb):
        srcs = [gb[n].reshape((N_DEV,) + w[n].shape[1:]) for n, _ in _layer_big(i)]
        lands = [jnp.zeros((N_DEV - 1,) + a.shape[1:], BF16) for a in srcs]
        *scatters[i], tok = _exchange_start(srcs, lands, True, "scatter_start_%d" % i)
        return tok[0, 0]

    loss, grad_x, grads = _local_step(x[0], loss_target[0], p, layer_weights, layer_grads, token)
    loss = lax.psum(loss, ("x", "y", "c"))
    gw, delta, new_m, new_v = {}, {}, {}, {}

    small = REPLICATED + SMALL_SHARDED
    (parts,) = _all_gather([_pack([grads[n].reshape(-1) for n in small], F32)])
    summed = _unpack(_sum_list([parts[dev] for dev in range(N_DEV)], F32), [grads[n].shape for n in small])
    for n, gs in zip(small, summed):
        if SHARD_AX[n] is not None:
            gs = lax.dynamic_index_in_dim(_split(gs, SHARD_AX[n]), me, 0, keepdims=False)
        gw[n], delta[n], new_m[n], new_v[n] = _adamw([gs], w[n], given["m_" + n], given["v_" + n])

    after, big = grad_x, {}
    for i in reversed(range(depth)):
        srcs, lands = _exchange_wait(*scatters[i], after, True, "scatter_wait_%d" % i)
        for (n, j), src, land in zip(_layer_big(i), srcs, lands):
            own = lax.dynamic_index_in_dim(src, me, 0, keepdims=False)
            big[n] = _adamw([own] + [(land, k) for k in range(N_DEV - 1)], w[n], given["m_" + n], given["v_" + n],
                            layer=j, prev=big.get(n))
            after = big[n][0]
    for n in BIG:
        gw[n], delta[n], new_m[n], new_v[n] = big[n]

    return (loss, grad_x[None], *[gw[n] for n in WEIGHTS], *[delta[n] for n in WEIGHTS],
            *[new_m[n] for n in WEIGHTS], *[new_v[n] for n in WEIGHTS])
```

```python
import collections
import math

import jax
import jax.numpy as jnp
from jax import lax
from jax.experimental import pallas as pl
from jax.experimental.pallas import tpu as pltpu

F32 = jnp.float32
BF16 = jnp.bfloat16

N_DEV = 8
CHUNK = 64
RMS_EPS = 1e-6
GLA_HEADS = 4
GLA_GATE_RANK = 16
GLA_GATE_TAU = 16.0
GATE_PAD = 128
LRU_BLOCK_W = 256
LRU_C = 8.0
RET_HEADS = 8
ROPE_BASE = 10000.0
ADAM_LR, ADAM_B1, ADAM_B2, ADAM_EPS, ADAM_WD, ADAM_STEP = 0.001, 0.9, 0.999, 1e-08, 0.01, 10

HALO = 16
VMEM_LIMIT = 56 * 1024 * 1024
ROW_TILE = 256
COMM_LANES = 1024
MM_TM, MM_TN, MM_TK = 1024, 1024, 2048

MESH = pl.DeviceIdType.MESH

WEIGHTS = ['norm_mix_w', 'norm_ffn_w', 'norm_out_w', 'gla_w_in', 'gla_w_gk', 'gla_b_gk', 'gla_norm_w',
           'gla_w_out', 'lru_w_in', 'lru_conv_w', 'lru_conv_b', 'lru_w_ga', 'lru_b_ga', 'lru_w_gx',
           'lru_b_gx', 'lru_lambda', 'lru_w_out', 'ret_w_in', 'ret_norm_w', 'ret_w_out', 'ffn_w_up',
           'ffn_conv_w', 'ffn_conv_b', 'ffn_w_down']
SHARD_AX = {'norm_mix_w': None, 'norm_ffn_w': None, 'norm_out_w': None, 'gla_w_in': 2, 'gla_w_gk': 2,
            'gla_b_gk': 1, 'gla_norm_w': 1, 'gla_w_out': 1, 'lru_w_in': 2, 'lru_conv_w': 2,
            'lru_conv_b': None, 'lru_w_ga': 2, 'lru_b_ga': None, 'lru_w_gx': 2, 'lru_b_gx': None,
            'lru_lambda': None, 'lru_w_out': 1, 'ret_w_in': 2, 'ret_norm_w': 1, 'ret_w_out': 1,
            'ffn_w_up': 2, 'ffn_conv_w': 2, 'ffn_conv_b': None, 'ffn_w_down': 1}
BIG = ['gla_w_in', 'gla_w_out', 'lru_w_in', 'lru_w_ga', 'lru_w_gx', 'lru_w_out', 'ret_w_in', 'ret_w_out',
       'ffn_w_up', 'ffn_w_down']
GATHERED = ['gla_w_out', 'lru_w_in', 'lru_w_out', 'ret_w_in', 'ret_w_out', 'ffn_w_up', 'ffn_w_down']
SMALL_SHARDED = ['gla_w_gk', 'gla_b_gk', 'gla_norm_w', 'lru_conv_w', 'ret_norm_w', 'ffn_conv_w']
REPLICATED = [n for n in WEIGHTS if SHARD_AX[n] is None]


def _pcall(body, **kw):
    return pl.pallas_call(body, **kw)


def _params(sem=None, **kw):
    return pltpu.CompilerParams(dimension_semantics=sem, vmem_limit_bytes=VMEM_LIMIT, **kw)


def _tile(n, pref, align=128):
    if n <= pref:
        return n
    t = (pref // align) * align
    while t >= align:
        if n % t == 0:
            return t
        t -= align
    return n


def _row_iota(shape):
    return lax.broadcasted_iota(jnp.int32, shape, 0)


def _shift_down(x, halo, s):
    t, c = x.shape
    r = pltpu.roll(x.reshape(t // 8, 8, c), s, 1)
    prev = jnp.concatenate([pltpu.roll(halo, s, 0)[None], r[:-1]], axis=0)
    sub = lax.broadcasted_iota(jnp.int32, r.shape, 1)
    return jnp.where(sub < s, prev, r).reshape(t, c)


def _shift_up(x, nxt, s):
    t, c = x.shape
    r = pltpu.roll(x.reshape(t // 8, 8, c), 8 - s, 1)
    follow = jnp.concatenate([r[1:], pltpu.roll(nxt, 8 - s, 0)[None]], axis=0)
    sub = lax.broadcasted_iota(jnp.int32, r.shape, 1)
    return jnp.where(sub >= 8 - s, follow, r).reshape(t, c)


def _cumsum_rows(x):
    t, row, s = x.shape[0], _row_iota(x.shape), 1
    while s < t:
        x = x + jnp.where(row >= s, pltpu.roll(x, s, 0), 0.0)
        s *= 2
    return x


def _rev_cumsum_rows(x):
    t, row, s = x.shape[0], _row_iota(x.shape), 1
    while s < t:
        x = x + jnp.where(row < t - s, pltpu.roll(x, t - s, 0), 0.0)
        s *= 2
    return x


def _scan_fwd(a, u, h0):
    t, row, s = a.shape[0], _row_iota(a.shape), 1
    while s < t:
        keep = row >= s
        u = u + a * jnp.where(keep, pltpu.roll(u, s, 0), 0.0)
        a = a * jnp.where(keep, pltpu.roll(a, s, 0), 1.0)
        s *= 2
    return u + a * h0


def _scan_rev(c, g, d_end):
    t, row, s = c.shape[0], _row_iota(c.shape), 1
    while s < t:
        keep = row < t - s
        g = g + c * jnp.where(keep, pltpu.roll(g, t - s, 0), 0.0)
        c = c * jnp.where(keep, pltpu.roll(c, t - s, 0), 1.0)
        s *= 2
    return g + c * d_end


def _pick_row(x, r):
    return jnp.sum(jnp.where(_row_iota(x.shape) == r, x, 0.0), axis=0, keepdims=True)


def _sigmoid(x):
    return 1.0 / (1.0 + jnp.exp(-x))


def _softplus(x):
    return jnp.maximum(x, 0.0) + jnp.log(1.0 + jnp.exp(-jnp.abs(x)))


_GELU_C = math.sqrt(2.0 / math.pi)


def _gelu_and_grad(x):
    x2 = x * x
    th = jnp.tanh(_GELU_C * (x + 0.044715 * x * x2))
    g = 0.5 * x * (1.0 + th)
    dg = 0.5 * (1.0 + th) + 0.5 * x * (1.0 - th * th) * _GELU_C * (1.0 + 3.0 * 0.044715 * x2)
    return g, dg


def _neg_expm1(y):
    small = -(y * (1.0 + y * (0.5 + y * (1.0 / 6.0 + y * (1.0 / 24.0)))))
    return jnp.where(y > -0.01, small, 1.0 - jnp.exp(y))


def _dot(a, b, dims):
    return lax.dot_general(a.astype(BF16), b.astype(BF16), (dims, ((), ())), preferred_element_type=F32)


def _dot_nn(a, b):
    return _dot(a, b, ((1,), (0,)))


def _dot_nt(a, b):
    return _dot(a, b, ((1,), (1,)))


def _dot_tn(a, b):
    return _dot(a, b, ((0,), (0,)))


_View = collections.namedtuple("_View", "arr kind layer dev0 ndev")


def _view_shape(v):
    r, c = v.arr.shape[2:]
    return (r, v.ndev * c) if v.kind == 'col' else (N_DEV * r, c)


def _view_spec(v, tr, tc, rc_of):
    r, c = v.arr.shape[2:]
    if v.kind == 'col':
        per = c // tc

        def imap(*g):
            ri, ci = rc_of(*g)
            return (v.dev0 + ci // per, v.layer, ri, ci % per)
    elif tr > r:
        def imap_blocks(*g):
            ri, ci = rc_of(*g)
            return (ri, v.layer, 0, ci)
        return pl.BlockSpec((tr // r, None, r, tc), imap_blocks)
    else:
        per = r // tr

        def imap(*g):
            ri, ci = rc_of(*g)
            return (ri // per, v.layer, ri % per, ci)
    return pl.BlockSpec((None, None, tr, tc), imap)


def _row_tile(v, pref):
    r = v.arr.shape[2]
    if r >= pref:
        return _tile(r, pref)
    q = max(q for q in (1, 2, 4, 8) if r * q <= pref)
    return r * q


def _rows2d(val):
    return val.reshape(-1, val.shape[-1]) if val.ndim == 3 else val


def _mm(a, b, mode, out_dtype=F32, res=None, name="mm", out=None):
    bshape = _view_shape(b) if isinstance(b, _View) else b.shape
    if mode == 'nn':
        (m, k), n = a.shape, bshape[1]
    elif mode == 'nt':
        (m, k), n = a.shape, bshape[0]
    else:
        (k, m), n = a.shape, bshape[1]
    tm, tn, tk = _tile(m, MM_TM), _tile(n, MM_TN), _tile(k, MM_TK)
    if isinstance(b, _View):
        if b.kind == 'col' and mode == 'nt':
            tk = _tile(b.arr.shape[3], MM_TK)
        elif b.kind == 'col':
            tn = _tile(b.arr.shape[3], MM_TN)
        elif mode == 'nt':
            tn = _row_tile(b, MM_TN)
        else:
            tk = _row_tile(b, MM_TK)
    if out is not None:
        if out.kind == 'col':
            tn = _tile(out.arr.shape[3], MM_TN)
        else:
            tm = _row_tile(out, MM_TM)
    nk = k // tk
    a_spec = pl.BlockSpec((tk, tm), lambda i, j, kk: (kk, i)) if mode == 'tn' else pl.BlockSpec((tm, tk), lambda i, j, kk: (i, kk))
    if isinstance(b, _View):
        b_spec = (_view_spec(b, tn, tk, lambda i, j, kk: (j, kk)) if mode == 'nt'
                  else _view_spec(b, tk, tn, lambda i, j, kk: (kk, j)))
    else:
        b_spec = pl.BlockSpec((tn, tk), lambda i, j, kk: (j, kk)) if mode == 'nt' else pl.BlockSpec((tk, tn), lambda i, j, kk: (kk, j))
    r_spec = pl.BlockSpec((tm, tn), lambda i, j, kk: (i, j))
    o_spec = r_spec if out is None else _view_spec(out, tm, tn, lambda i, j, kk: (i, j))
    dot = {'nn': _dot_nn, 'nt': _dot_nt, 'tn': _dot_tn}[mode]

    def body(*refs):
        refs = list(refs)
        if out is not None:
            del refs[2 + (res is not None)]
        a_ref, b_ref = refs[:2]
        r_ref = None if res is None else refs[2]
        o_ref = refs[2 if res is None else 3]

        def finish(total):
            if res is not None:
                total = total + r_ref[...].astype(F32)
            o_ref[...] = total.astype(out_dtype).reshape(o_ref.shape)

        part = dot(a_ref[...], _rows2d(b_ref[...]))
        if nk == 1:
            finish(part)
            return
        acc = refs[-1]
        kk = pl.program_id(2)

        @pl.when(kk == 0)
        def _():
            acc[...] = part

        @pl.when(kk > 0)
        def _():
            acc[...] += part

        @pl.when(kk == nk - 1)
        def _():
            finish(acc[...])

    args, specs = [a, b.arr if isinstance(b, _View) else b], [a_spec, b_spec]
    if res is not None:
        args.append(res)
        specs.append(r_spec)
    aliases, out_shape = {}, jax.ShapeDtypeStruct((m, n), out_dtype)
    if out is not None:
        aliases, out_shape = {len(args): 0}, jax.ShapeDtypeStruct(out.arr.shape, out.arr.dtype)
        args.append(out.arr)
        specs.append(pl.BlockSpec(memory_space=pl.ANY))
    return _pcall(body, name=name, out_shape=out_shape,
                  grid=(m // tm, n // tn, nk), in_specs=specs, out_specs=o_spec,
                  scratch_shapes=[] if nk == 1 else [pltpu.VMEM((tm, tn), F32)], input_output_aliases=aliases,
                  compiler_params=_params(("parallel", "parallel", "arbitrary")))(*args)


def _rms_fwd(x, w):
    s, d = x.shape
    tr = _tile(s, ROW_TILE, 16)

    def body(x_ref, w_ref, o_ref):
        xv = x_ref[...]
        r = lax.rsqrt(jnp.mean(xv * xv, axis=-1, keepdims=True) + RMS_EPS)
        o_ref[...] = (xv * r * w_ref[...]).astype(BF16)

    return _pcall(body, name="rms_fwd", out_shape=jax.ShapeDtypeStruct((s, d), BF16), grid=(s // tr,),
                  in_specs=[pl.BlockSpec((tr, d), lambda i: (i, 0)), pl.BlockSpec((1, d), lambda i: (0, 0))],
                  out_specs=pl.BlockSpec((tr, d), lambda i: (i, 0)), compiler_params=_params(("parallel",)))(x, w)


def _rms_bwd(x, w, dh, dres):
    s, d = x.shape
    tr = _tile(s, ROW_TILE, 16)

    def body(x_ref, w_ref, dh_ref, dr_ref, dx_ref, dxb_ref, dw_ref):
        i = pl.program_id(0)
        xv = x_ref[...]
        r = lax.rsqrt(jnp.mean(xv * xv, axis=-1, keepdims=True) + RMS_EPS)
        xh = xv * r
        dhv = dh_ref[...].astype(F32)
        dxh = dhv * w_ref[...]
        dxv = dr_ref[...] + r * (dxh - xh * jnp.mean(dxh * xh, axis=-1, keepdims=True))
        dx_ref[...] = dxv
        dxb_ref[...] = dxv.astype(BF16)
        part = jnp.sum(dhv * xh, axis=0, keepdims=True)

        @pl.when(i == 0)
        def _():
            dw_ref[...] = part

        @pl.when(i > 0)
        def _():
            dw_ref[...] += part

    row = pl.BlockSpec((tr, d), lambda i: (i, 0))
    vec = pl.BlockSpec((1, d), lambda i: (0, 0))
    return _pcall(body, name="rms_bwd",
                  out_shape=(jax.ShapeDtypeStruct((s, d), F32), jax.ShapeDtypeStruct((s, d), BF16),
                             jax.ShapeDtypeStruct((1, d), F32)),
                  grid=(s // tr,), in_specs=[row, vec, row, row], out_specs=(row, row, vec),
                  compiler_params=_params(("arbitrary",)))(x, w, dh, dres)


def _final_loss(x, w, target):
    s, d = x.shape
    tr = _tile(s, ROW_TILE, 16)

    def body(x_ref, w_ref, t_ref, l_ref, dx_ref, dxb_ref, dw_ref):
        i = pl.program_id(0)
        xv = x_ref[...]
        r = lax.rsqrt(jnp.mean(xv * xv, axis=-1, keepdims=True) + RMS_EPS)
        xh = xv * r
        err = xh * w_ref[...] - t_ref[...]
        lpart = 0.5 * jnp.sum(jnp.mean(err * err, axis=-1, keepdims=True), axis=0, keepdims=True)
        dy = err * (1.0 / d)
        dxh = dy * w_ref[...]
        dxv = r * (dxh - xh * jnp.mean(dxh * xh, axis=-1, keepdims=True))
        dx_ref[...] = dxv
        dxb_ref[...] = dxv.astype(BF16)
        part = jnp.sum(dy * xh, axis=0, keepdims=True)

        @pl.when(i == 0)
        def _():
            dw_ref[...] = part
            l_ref[...] = jnp.broadcast_to(lpart, l_ref.shape)

        @pl.when(i > 0)
        def _():
            dw_ref[...] += part
            l_ref[...] += jnp.broadcast_to(lpart, l_ref.shape)

    row = pl.BlockSpec((tr, d), lambda i: (i, 0))
    vec = pl.BlockSpec((1, d), lambda i: (0, 0))
    return _pcall(body, name="final_loss",
                  out_shape=(jax.ShapeDtypeStruct((8, 128), F32), jax.ShapeDtypeStruct((s, d), F32),
                             jax.ShapeDtypeStruct((s, d), BF16), jax.ShapeDtypeStruct((1, d), F32)),
                  grid=(s // tr,), in_specs=[row, vec, row],
                  out_specs=(pl.BlockSpec((8, 128), lambda i: (0, 0)), row, row, vec),
                  compiler_params=_params(("arbitrary",)))(x, w, target)


def _halo_prev(tt, cmap):
    return lambda *g: (jnp.maximum(g[-1] * (tt // HALO) - 1, 0), cmap(*g))


def _ffn_act_fwd(u, cw, cb):
    s, f2 = u.shape
    f = f2 // 2
    tt, tc = _tile(s, ROW_TILE, 16), _tile(f, 512)
    nc = f // tc

    def body(g_ref, gh_ref, v_ref, vh_ref, wg_ref, wv_ref, bg_ref, bv_ref, a_ref, gate_ref, val_ref):
        t = pl.program_id(1)

        def conv(x_ref, h_ref, w_ref, b_ref):
            x = x_ref[...].astype(F32)
            hal = jnp.where(t > 0, h_ref[...].astype(F32)[8:16], 0.0)
            return (w_ref[2:3, :] * x + w_ref[1:2, :] * _shift_down(x, hal, 1)
                    + w_ref[0:1, :] * _shift_down(x, hal, 2) + b_ref[...])

        gate = conv(g_ref, gh_ref, wg_ref, bg_ref)
        val = conv(v_ref, vh_ref, wv_ref, bv_ref)
        a_ref[...] = (_gelu_and_grad(gate)[0] * val).astype(BF16)
        gate_ref[...] = gate.astype(BF16)
        val_ref[...] = val.astype(BF16)

    def main(off):
        return pl.BlockSpec((tt, tc), lambda j, t: (t, j + off))

    def halo(off):
        return pl.BlockSpec((HALO, tc), _halo_prev(tt, lambda j, t: j + off))

    def vec(rows, off):
        return pl.BlockSpec((rows, tc), lambda j, t: (0, j + off))

    return _pcall(body, name="ffn_act_fwd", out_shape=(jax.ShapeDtypeStruct((s, f), BF16),) * 3, grid=(nc, s // tt),
                  in_specs=[main(0), halo(0), main(nc), halo(nc), vec(3, 0), vec(3, nc), vec(1, 0), vec(1, nc)],
                  out_specs=(main(0),) * 3,
                  compiler_params=_params(("parallel", "parallel")))(u, u, u, u, cw, cw, cb, cb)


def _ffn_act_bwd(u, gate, val, cw, da):
    s, f2 = u.shape
    f = f2 // 2
    tt, tc = _tile(s, ROW_TILE, 16), _tile(f, 512)
    nc, nt = f // tc, s // tt

    def body(ug_ref, uv_ref, gate_ref, val_ref, da_ref, wg_ref, wv_ref,
             dug_ref, duv_ref, dwg_ref, dwv_ref, dbg_ref, dbv_ref, carry):
        t = pl.program_id(1)

        @pl.when(t == 0)
        def _():
            carry[...] = jnp.zeros_like(carry)

        dav = da_ref[...].astype(F32)
        gl, dgl = _gelu_and_grad(gate_ref[...].astype(F32))
        dgate, dval = dav * val_ref[...].astype(F32) * dgl, dav * gl

        def back(k, d, x_ref, w_ref, du_ref, dw_ref, db_ref):
            nxt = carry[k]
            ds = [_shift_up(d, nxt, 2), _shift_up(d, nxt, 1), d]
            du_ref[...] = (w_ref[2:3, :] * ds[2] + w_ref[1:2, :] * ds[1] + w_ref[0:1, :] * ds[0]).astype(BF16)
            carry[k] = d[0:8]
            x = x_ref[...].astype(F32)
            parts = [jnp.sum(ds[r] * x, axis=0, keepdims=True) for r in range(3)]
            bpart = jnp.sum(d, axis=0, keepdims=True)

            @pl.when(t == 0)
            def _():
                for r in range(3):
                    dw_ref[r:r + 1, :] = parts[r]
                db_ref[...] = bpart

            @pl.when(t > 0)
            def _():
                for r in range(3):
                    dw_ref[r:r + 1, :] += parts[r]
                db_ref[...] += bpart

        back(0, dgate, ug_ref, wg_ref, dug_ref, dwg_ref, dbg_ref)
        back(1, dval, uv_ref, wv_ref, duv_ref, dwv_ref, dbv_ref)

    def main(off):
        return pl.BlockSpec((tt, tc), lambda j, t: (nt - 1 - t, j + off))

    def vec(rows, off):
        return pl.BlockSpec((rows, tc), lambda j, t: (0, j + off))

    dug, duv, dwg, dwv, dbg, dbv = _pcall(
        body, name="ffn_act_bwd",
        out_shape=(jax.ShapeDtypeStruct((s, f), BF16), jax.ShapeDtypeStruct((s, f), BF16),
                   jax.ShapeDtypeStruct((3, f), F32), jax.ShapeDtypeStruct((3, f), F32),
                   jax.ShapeDtypeStruct((1, f), F32), jax.ShapeDtypeStruct((1, f), F32)),
        grid=(nc, nt),
        in_specs=[main(0), main(nc), main(0), main(0), main(0), vec(3, 0), vec(3, nc)],
        out_specs=(main(0), main(0), vec(3, 0), vec(3, 0), vec(1, 0), vec(1, 0)),
        scratch_shapes=[pltpu.VMEM((2, 8, tc), F32)],
        compiler_params=_params(("parallel", "arbitrary")))(u, u, gate, val, da, cw, cw)
    return dug, duv, jnp.concatenate([dwg, dwv], axis=1), jnp.concatenate([dbg, dbv], axis=1)


def _chunk_cols(h, kd, vd, heads):
    dk, dv = kd // heads, vd // heads
    return (slice(h * dk, (h + 1) * dk), slice(kd + h * dk, kd + (h + 1) * dk),
            slice(2 * kd + h * dv, 2 * kd + (h + 1) * dv), slice(2 * kd + vd + h * dv, 2 * kd + vd + (h + 1) * dv))


def _rope(x, cos, sin):
    half = x.shape[1] // 2
    x1, x2 = x[:, :half], x[:, half:]
    return jnp.concatenate([x1 * cos - x2 * sin, x2 * cos + x1 * sin], axis=1)


def _unrope(d, cos, sin):
    half = d.shape[1] // 2
    d1, d2 = d[:, :half], d[:, half:]
    return jnp.concatenate([d1 * cos + d2 * sin, d2 * cos - d1 * sin], axis=1)


def _chunk_inputs(gla, h, heads, kd, vd, rows, proj_ref, aux):
    qc, kc, vc, gc = _chunk_cols(h, kd, vd, heads)
    dk = kd // heads
    q = proj_ref[rows, qc].astype(F32)
    k = proj_ref[rows, kc].astype(F32)
    v = proj_ref[rows, vc]
    g = proj_ref[rows, gc].astype(F32)
    c = {}
    if gla:
        z_ref, wgk_ref, bgk_ref = aux
        c['z'] = z_ref[rows, :]
        c['gk'] = _dot_nn(c['z'], wgk_ref[:, qc]) + bgk_ref[:, qc]
        la = (jnp.minimum(c['gk'], 0.0) - jnp.log(1.0 + jnp.exp(-jnp.abs(c['gk'])))) * (1.0 / GLA_GATE_TAU)
        b = _cumsum_rows(la)
        bl = jnp.sum(la, axis=0, keepdims=True)
        q = q * (dk ** -0.5)
    else:
        cos_ref, sin_ref = aux
        c['cos'], c['sin'] = cos_ref[rows, :], sin_ref[rows, :]
        q = _rope(q, c['cos'], c['sin'])
        k = _rope(k, c['cos'], c['sin']) * (dk ** -0.5)
        lg = math.log(1.0 - 2.0 ** (-5.0 - h))
        b = lg * (_row_iota((CHUNK, 1)).astype(F32) + 1.0)
        bl = jnp.full((1, 1), lg * CHUNK, F32)
    eb, enb = jnp.exp(b), jnp.exp(-b)
    c.update(q=q, k=k, v=v, g=g, b=b, bl=bl, eb=eb, enb=enb, ebl=jnp.exp(bl),
             qd=q * eb, kg=k * enb, qg=q * enb, kd=k * eb, ks=k * jnp.exp(bl - b))
    lower = _row_iota((CHUNK, CHUNK)) >= lax.broadcasted_iota(jnp.int32, (CHUNK, CHUNK), 1)
    c['lower'] = lower
    c['A'] = jnp.where(lower, _dot_nt(c['qd'], c['kg']), _dot_nt(c['qg'], c['kd']))
    return c


def _head_norm(gla, o):
    if not gla:
        o = o - jnp.mean(o, axis=-1, keepdims=True)
    r = lax.rsqrt(jnp.mean(o * o, axis=-1, keepdims=True) + RMS_EPS)
    return o * r, r


def _chunk_fwd(gla, proj, aux_arrays, nw, heads, kd, vd, cps):
    s, pw = proj.shape
    dk, dv = kd // heads, vd // heads
    rt = CHUNK * cps
    nb = s // rt
    n_aux = len(aux_arrays)

    def body(*refs):
        proj_ref, aux, nw_ref = refs[0], refs[1:1 + n_aux], refs[1 + n_aux]
        y_ref, o_ref, st_ref, state = refs[2 + n_aux:]

        @pl.when(pl.program_id(0) == 0)
        def _():
            state[...] = jnp.zeros_like(state)

        def chunk(ci, carry):
            rows = pl.ds(pl.multiple_of(ci * CHUNK, CHUNK), CHUNK)
            for h in range(heads):
                c = _chunk_inputs(gla, h, heads, kd, vd, rows, proj_ref, aux)
                vcols = slice(h * dv, (h + 1) * dv)
                st0 = state[h]
                st_ref[ci, h] = st0.astype(BF16)
                o = _dot_nn(c['A'], c['v']) + _dot_nt(c['qd'], st0)
                state[h] = st0 * c['ebl'] + _dot_tn(c['v'], c['ks'])
                oh, _ = _head_norm(gla, o)
                gv = c['g']
                y_ref[rows, vcols] = (oh * nw_ref[:, vcols] * (gv * _sigmoid(gv))).astype(BF16)
                o_ref[rows, vcols] = o.astype(BF16)
            return carry

        lax.fori_loop(0, cps, chunk, 0)

    row = lambda w: pl.BlockSpec((rt, w), lambda n: (n, 0))
    full = lambda a: pl.BlockSpec(a.shape, lambda n: (0,) * a.ndim)
    aux_specs = [row(a.shape[1]) if a.shape[0] == s else full(a) for a in aux_arrays]
    return _pcall(
        body, name="gla_fwd" if gla else "ret_fwd",
        out_shape=(jax.ShapeDtypeStruct((s, vd), BF16), jax.ShapeDtypeStruct((s, vd), BF16),
                   jax.ShapeDtypeStruct((s // CHUNK, heads, dv, dk), BF16)),
        grid=(nb,), in_specs=[row(pw)] + aux_specs + [full(nw)],
        out_specs=(row(vd), row(vd), pl.BlockSpec((cps, heads, dv, dk), lambda n: (n, 0, 0, 0))),
        scratch_shapes=[pltpu.VMEM((heads, dv, dk), F32)],
        compiler_params=_params(("arbitrary",)))(proj, *aux_arrays, nw)


def _chunk_bwd(gla, proj, aux_arrays, nw, o_st, st, dy, heads, kd, vd, cps):
    s, pw = proj.shape
    dk, dv = kd // heads, vd // heads
    rt = CHUNK * cps
    nb = s // rt
    n_aux = len(aux_arrays)

    def body(*refs):
        proj_ref, aux, nw_ref = refs[0], refs[1:1 + n_aux], refs[1 + n_aux]
        o_ref, st_ref, dy_ref = refs[2 + n_aux:5 + n_aux]
        outs = refs[5 + n_aux:]
        dp_ref, dnw_ref = outs[0], outs[1]
        if gla:
            dz_ref, dwgk_ref, dbgk_ref, dstate = outs[2:]
        else:
            dstate = outs[2]

        @pl.when(pl.program_id(0) == 0)
        def _():
            dstate[...] = jnp.zeros_like(dstate)
            dnw_ref[...] = jnp.zeros_like(dnw_ref)
            if gla:
                dwgk_ref[...] = jnp.zeros_like(dwgk_ref)
                dbgk_ref[...] = jnp.zeros_like(dbgk_ref)

        def chunk(i, carry):
            ci = cps - 1 - i
            rows = pl.ds(pl.multiple_of(ci * CHUNK, CHUNK), CHUNK)
            dz = jnp.zeros((CHUNK, GATE_PAD), F32)
            for h in range(heads):
                c = _chunk_inputs(gla, h, heads, kd, vd, rows, proj_ref, aux)
                qc, kc, vc, gc = _chunk_cols(h, kd, vd, heads)
                vcols = slice(h * dv, (h + 1) * dv)
                o = o_ref[rows, vcols].astype(F32)
                oh, r = _head_norm(gla, o)
                dyv = dy_ref[rows, vcols].astype(F32)
                gv = c['g']
                sg = _sigmoid(gv)
                nwv = nw_ref[:, vcols]
                dp_ref[rows, gc] = (dyv * oh * nwv * (sg * (1.0 + gv * (1.0 - sg)))).astype(BF16)
                dn = dyv * (gv * sg)
                dnw_ref[:, vcols] += jnp.sum(dn * oh, axis=0, keepdims=True)
                doh = dn * nwv
                do = doh - oh * jnp.mean(doh * oh, axis=-1, keepdims=True)
                if not gla:
                    do = do - jnp.mean(doh, axis=-1, keepdims=True)
                do = r * do
                st0 = st_ref[ci, h]
                dst1 = dstate[h]
                v = c['v']
                da = _dot_nt(do, v)
                dal = jnp.where(c['lower'], da, 0.0)
                dau = da - dal
                dp_ref[rows, vc] = (_dot_tn(c['A'], do) + _dot_nt(c['ks'], dst1)).astype(BF16)
                dqd = _dot_nn(dal, c['kg']) + _dot_nn(do, st0)
                dkg = _dot_tn(dal, c['qd'])
                dqg = _dot_nn(dau, c['kd'])
                dkd = _dot_tn(dau, c['qg'])
                dks = _dot_nn(v, dst1)
                dstate[h] = _dot_tn(do, c['qd']) + dst1 * c['ebl']
                dq = dqd * c['eb'] + dqg * c['enb']
                dkk = dkg * c['enb'] + dkd * c['eb'] + dks * jnp.exp(c['bl'] - c['b'])
                if gla:
                    db = dqd * c['qd'] - dkg * c['kg'] - dqg * c['qg'] + dkd * c['kd'] - dks * c['ks']
                    dbl = (jnp.sum(dks * c['ks'], axis=0, keepdims=True)
                           + c['ebl'] * jnp.sum(dst1 * st0.astype(F32), axis=0, keepdims=True))
                    db = db + jnp.where(_row_iota(db.shape) == CHUNK - 1, dbl, 0.0)
                    dgk = _rev_cumsum_rows(db) * (1.0 / GLA_GATE_TAU) / (1.0 + jnp.exp(c['gk']))
                    _, wgk_ref, _ = aux
                    dz = dz + _dot_nt(dgk, wgk_ref[:, qc])
                    dwgk_ref[:, qc] += _dot_tn(c['z'], dgk)
                    dbgk_ref[:, qc] += jnp.sum(dgk, axis=0, keepdims=True)
                    dp_ref[rows, qc] = (dq * (dk ** -0.5)).astype(BF16)
                    dp_ref[rows, kc] = dkk.astype(BF16)
                else:
                    dp_ref[rows, qc] = _unrope(dq, c['cos'], c['sin']).astype(BF16)
                    dp_ref[rows, kc] = (_unrope(dkk, c['cos'], c['sin']) * (dk ** -0.5)).astype(BF16)
            if gla:
                dz_ref[rows, :] = dz
            return carry

        lax.fori_loop(0, cps, chunk, 0)

    row = lambda w: pl.BlockSpec((rt, w), lambda n: (nb - 1 - n, 0))
    full = lambda a: pl.BlockSpec(a.shape, lambda n: (0,) * a.ndim)
    aux_specs = [row(a.shape[1]) if a.shape[0] == s else full(a) for a in aux_arrays]
    out_shape = [jax.ShapeDtypeStruct((s, pw), BF16), jax.ShapeDtypeStruct((1, vd), F32)]
    out_specs = [row(pw), full(nw)]
    if gla:
        wgk, bgk = aux_arrays[1], aux_arrays[2]
        out_shape += [jax.ShapeDtypeStruct((s, GATE_PAD), F32), jax.ShapeDtypeStruct(wgk.shape, F32),
                      jax.ShapeDtypeStruct(bgk.shape, F32)]
        out_specs += [row(GATE_PAD), full(wgk), full(bgk)]
    return _pcall(
        body, name="gla_bwd" if gla else "ret_bwd", out_shape=tuple(out_shape), grid=(nb,),
        in_specs=[row(pw)] + aux_specs + [full(nw), row(vd),
                                          pl.BlockSpec((cps, heads, dv, dk), lambda n: (nb - 1 - n, 0, 0, 0)), row(vd)],
        out_specs=tuple(out_specs), scratch_shapes=[pltpu.VMEM((heads, dv, dk), F32)],
        compiler_params=_params(("arbitrary",)))(proj, *aux_arrays, nw, o_st, st, dy)


def _lru_gates(xc, wga_ref, bga_ref, wgx_ref, bgx_ref, lam_ref):
    r = _sigmoid(_dot_nn(xc, wga_ref[0]) + bga_ref[...])
    i = _sigmoid(_dot_nn(xc, wgx_ref[0]) + bgx_ref[...])
    sp = _softplus(-lam_ref[...])
    la = -LRU_C * r * sp
    return r, i, sp, la, jnp.exp(la), jnp.sqrt(_neg_expm1(2.0 * la))


def _lru_specs(w, nbk, tt, tmap):
    main = lambda off: pl.BlockSpec((tt, LRU_BLOCK_W), lambda n, t: (tmap(t), n + off))
    halo = pl.BlockSpec((HALO, LRU_BLOCK_W), lambda n, t: (jnp.maximum(tmap(t) * (tt // HALO) - 1, 0), n))
    vec = lambda rows: pl.BlockSpec((rows, LRU_BLOCK_W), lambda n, t: (0, n))
    mat = pl.BlockSpec((1, LRU_BLOCK_W, LRU_BLOCK_W), lambda n, t: (n, 0, 0))
    return main, halo, vec, mat


def _lru_fwd(proj, cw, cb, wga, bga, wgx, bgx, lam):
    s, w2 = proj.shape
    w = w2 // 2
    nbk, tt = w // LRU_BLOCK_W, _tile(s, ROW_TILE, 16)
    main, halo, vec, mat = _lru_specs(w, nbk, tt, lambda t: t)

    def body(x_ref, xh_ref, y_ref, cw_ref, cb_ref, wga_ref, bga_ref, wgx_ref, bgx_ref, lam_ref,
             out_ref, hs_ref, hcar):
        t = pl.program_id(1)
        x = x_ref[...]
        hal = jnp.where(t > 0, xh_ref[8:16, :], 0.0)
        xc = (cw_ref[3:4, :] * x + cw_ref[2:3, :] * _shift_down(x, hal, 1) + cw_ref[1:2, :] * _shift_down(x, hal, 2)
              + cw_ref[0:1, :] * _shift_down(x, hal, 3) + cb_ref[...])
        r, i, sp, la, a, mlt = _lru_gates(xc, wga_ref, bga_ref, wgx_ref, bgx_ref, lam_ref)

        @pl.when(t == 0)
        def _():
            hcar[...] = jnp.zeros_like(hcar)

        h = _scan_fwd(a, xc * i * mlt, hcar[...])
        hcar[...] = _pick_row(h, tt - 1)
        hs_ref[...] = h
        out_ref[...] = (h * _gelu_and_grad(y_ref[...])[0]).astype(BF16)

    return _pcall(body, name="lru_fwd",
                  out_shape=(jax.ShapeDtypeStruct((s, w), BF16), jax.ShapeDtypeStruct((s, w), F32)),
                  grid=(nbk, s // tt),
                  in_specs=[main(0), halo, main(nbk), vec(4), vec(1), mat, vec(1), mat, vec(1), vec(1)],
                  out_specs=(main(0), main(0)), scratch_shapes=[pltpu.VMEM((1, LRU_BLOCK_W), F32)],
                  compiler_params=_params(("parallel", "arbitrary")))(proj, proj, proj, cw, cb, wga, bga, wgx, bgx, lam)


def _lru_bwd(proj, hs, dout, cw, cb, wga, bga, wgx, bgx, lam):
    s, w2 = proj.shape
    w = w2 // 2
    nbk, tt = w // LRU_BLOCK_W, _tile(s, ROW_TILE, 16)
    nt = s // tt
    main, halo, vec, mat = _lru_specs(w, nbk, tt, lambda t: nt - 1 - t)

    def body(x_ref, xh_ref, y_ref, hs_ref, hh_ref, do_ref, cw_ref, cb_ref, wga_ref, bga_ref, wgx_ref, bgx_ref,
             lam_ref, dx_ref, dy_ref, dcw_ref, dcb_ref, dbga_ref, dbgx_ref, dlam_ref, dwga_ref, dwgx_ref,
             dhcar, dxcar):
        t = pl.program_id(1)
        first_tile = t == nt - 1

        @pl.when(t == 0)
        def _():
            dhcar[...] = jnp.zeros_like(dhcar)
            dxcar[...] = jnp.zeros_like(dxcar)

        x = x_ref[...]
        hal = jnp.where(first_tile, 0.0, xh_ref[8:16, :])
        xs = [x, _shift_down(x, hal, 1), _shift_down(x, hal, 2), _shift_down(x, hal, 3)]
        xc = cw_ref[3:4, :] * xs[0] + cw_ref[2:3, :] * xs[1] + cw_ref[1:2, :] * xs[2] + cw_ref[0:1, :] * xs[3] + cb_ref[...]
        r, i, sp, la, a, mlt = _lru_gates(xc, wga_ref, bga_ref, wgx_ref, bgx_ref, lam_ref)
        h = hs_ref[...]
        hprev = _shift_down(h, jnp.where(first_tile, 0.0, hh_ref[8:16, :]), 1)
        gl, dgl = _gelu_and_grad(y_ref[...])
        dov = do_ref[...].astype(F32)
        dy_ref[...] = (dov * h * dgl).astype(BF16)
        row = _row_iota(a.shape)
        coef = jnp.where(row == tt - 1, 1.0, pltpu.roll(a, tt - 1, 0))
        dh = _scan_rev(coef, dov * gl, dhcar[...])
        dhcar[...] = _pick_row(a * dh, 0)
        dxc = dh * i * mlt
        di = dh * xc * mlt
        dm = dh * xc * i
        dla = dh * hprev * a - dm * jnp.exp(2.0 * la) / mlt
        dpa = dla * (-LRU_C * sp) * r * (1.0 - r)
        dpx = di * i * (1.0 - i)
        dxc = dxc + _dot_nt(dpa, wga_ref[0]) + _dot_nt(dpx, wgx_ref[0])
        nxt = dxcar[...]
        dx_ref[...] = (cw_ref[3:4, :] * dxc + cw_ref[2:3, :] * _shift_up(dxc, nxt, 1)
                       + cw_ref[1:2, :] * _shift_up(dxc, nxt, 2) + cw_ref[0:1, :] * _shift_up(dxc, nxt, 3)).astype(BF16)
        dxcar[...] = dxc[0:8]
        colsum = lambda v: jnp.sum(v, axis=0, keepdims=True)
        parts = [(dcb_ref, colsum(dxc)), (dbga_ref, colsum(dpa)), (dbgx_ref, colsum(dpx)),
                 (dlam_ref, colsum(dla * LRU_C * r) * _sigmoid(-lam_ref[...]))]
        wparts = [colsum(dxc * xs[3 - k]) for k in range(4)]
        dwa, dwx = _dot_tn(xc, dpa), _dot_tn(xc, dpx)

        @pl.when(t == 0)
        def _():
            for ref, val in parts:
                ref[...] = val
            for k in range(4):
                dcw_ref[k:k + 1, :] = wparts[k]
            dwga_ref[0] = dwa
            dwgx_ref[0] = dwx

        @pl.when(t > 0)
        def _():
            for ref, val in parts:
                ref[...] += val
            for k in range(4):
                dcw_ref[k:k + 1, :] += wparts[k]
            dwga_ref[0] += dwa
            dwgx_ref[0] += dwx

    sd = jax.ShapeDtypeStruct
    return _pcall(
        body, name="lru_bwd",
        out_shape=(sd((s, w), BF16), sd((s, w), BF16), sd((4, w), F32), sd((1, w), F32), sd((1, w), F32),
                   sd((1, w), F32), sd((1, w), F32), sd(wga.shape, F32), sd(wgx.shape, F32)),
        grid=(nbk, nt),
        in_specs=[main(0), halo, main(nbk), main(0), halo, main(0), vec(4), vec(1), mat, vec(1), mat, vec(1), vec(1)],
        out_specs=(main(0), main(0), vec(4), vec(1), vec(1), vec(1), vec(1), mat, mat),
        scratch_shapes=[pltpu.VMEM((1, LRU_BLOCK_W), F32), pltpu.VMEM((8, LRU_BLOCK_W), F32)],
        compiler_params=_params(("parallel", "arbitrary")))(proj, proj, proj, hs, hs, dout, cw, cb, wga, bga, wgx, bgx, lam)


def _adamw(parts, w, m, v, layer=None, prev=None):
    shape = w.shape
    cols = shape[-1]
    layers = 1 if layer is None else shape[0]
    w3, m3, v3 = (a.reshape(layers, -1, cols) for a in (w, m, v))
    rows = w3.shape[1]
    tr = _tile(rows, max(16, (1 << 17) // cols // 16 * 16), 16)
    blk = pl.BlockSpec((None, tr, cols), lambda i: (layer or 0, i, 0))
    p_args, p_specs = [], []
    for part in parts:
        if isinstance(part, tuple):
            stack, idx = part
            p_args.append(stack.reshape(stack.shape[0], rows, cols))
            p_specs.append(pl.BlockSpec((None, tr, cols), lambda i, idx=idx: (idx, i, 0)))
        else:
            p_args.append(part.reshape(rows, cols))
            p_specs.append(pl.BlockSpec((tr, cols), lambda i: (i, 0)))
    n_parts = len(parts)
    n_prev = 0 if prev is None else 4

    def body(*refs):
        w_ref, m_ref, v_ref = refs[n_parts:n_parts + 3]
        g_ref, d_ref, nm_ref, nv_ref = refs[n_parts + 3 + n_prev:]
        gv = refs[0][...].astype(F32)
        for p_ref in refs[1:n_parts]:
            gv = gv + p_ref[...].astype(F32)
        g_ref[...] = gv
        nm = ADAM_B1 * m_ref[...] + (1.0 - ADAM_B1) * gv
        nv = ADAM_B2 * v_ref[...] + (1.0 - ADAM_B2) * (gv * gv)
        m_hat = nm / (1.0 - ADAM_B1 ** ADAM_STEP)
        v_hat = nv / (1.0 - ADAM_B2 ** ADAM_STEP)
        d_ref[...] = -ADAM_LR * (m_hat / (jnp.sqrt(v_hat) + ADAM_EPS) + ADAM_WD * w_ref[...])
        nm_ref[...] = nm
        nv_ref[...] = nv

    prev_args = [] if prev is None else [a.reshape(layers, rows, cols) for a in prev]
    n_in = n_parts + 3
    out = _pcall(body, name="adamw", out_shape=tuple(jax.ShapeDtypeStruct((layers, rows, cols), F32) for _ in range(4)),
                 grid=(rows // tr,), in_specs=p_specs + [blk] * 3 + [pl.BlockSpec(memory_space=pl.ANY)] * n_prev,
                 out_specs=(blk,) * 4, input_output_aliases={n_in + q: q for q in range(n_prev)},
                 compiler_params=_params(("parallel",)))(*p_args, w3, m3, v3, *prev_args)
    return tuple(o.reshape(shape) for o in out)


def _sum_list(arrs, out_dtype):
    shape = arrs[0].shape
    cols = shape[-1]
    flat = [a.reshape(-1, cols) for a in arrs]
    rows = flat[0].shape[0]
    tr = _tile(rows, max(16, (1 << 18) // cols // 16 * 16), 16)

    def body(*refs):
        acc = refs[0][...].astype(F32)
        for p_ref in refs[1:-1]:
            acc = acc + p_ref[...].astype(F32)
        refs[-1][...] = acc.astype(out_dtype)

    blk = pl.BlockSpec((tr, cols), lambda i: (i, 0))
    out = _pcall(body, name="sum_list", out_shape=jax.ShapeDtypeStruct((rows, cols), out_dtype), grid=(rows // tr,),
                 in_specs=[blk] * len(flat), out_specs=blk, compiler_params=_params(("parallel",)))(*flat)
    return out.reshape(shape)


def _mesh_pos():
    return lax.axis_index("x"), lax.axis_index("y"), lax.axis_index("c")


def _all_gather(arrs):
    n = len(arrs)

    def body(*refs):
        x_refs, out_refs = refs[:n], refs[n:2 * n]
        send_sems, recv_sems, local_sems = refs[2 * n:]
        x, y, c = _mesh_pos()
        me, sibling = (x, y, c), (x, y, 1 - c)
        chips = [(1 - x, y), (x, 1 - y), (1 - x, 1 - y)]

        def slot(a, px, py, pc):
            return out_refs[a].at[4 * px + 2 * py + pc]

        def copy(a, k, block, to, src=None):
            return pltpu.make_async_remote_copy(
                src_ref=slot(a, *block) if src is None else src, dst_ref=slot(a, *block),
                send_sem=send_sems.at[7 * a + k], recv_sem=recv_sems.at[7 * a + k], device_id=to, device_id_type=MESH)

        mine = [pltpu.make_async_copy(x_refs[a], slot(a, *me), local_sems.at[a]) for a in range(n)]
        for cp in mine:
            cp.start()
        first = []
        for j, chip in enumerate(chips):
            first += [copy(a, 1 + j, me, (*chip, c), src=x_refs[a]) for a in range(n)]
        first += [copy(a, 0, me, sibling, src=x_refs[a]) for a in range(n)]
        for cp in first:
            cp.start()
        passed = []
        for j, chip in enumerate(chips):
            for a in range(n):
                copy(a, 1 + j, (*chip, c), me).wait_recv()
                passed.append(copy(a, 4 + j, (*chip, c), sibling))
                passed[-1].start()
        for a in range(n):
            copy(a, 0, sibling, me).wait_recv()
        for j, chip in enumerate(chips):
            for a in range(n):
                copy(a, 4 + j, (*chip, 1 - c), me).wait_recv()
        for cp in first + passed:
            cp.wait_send()
        for cp in mine:
            cp.wait()

    hbm = pl.BlockSpec(memory_space=pl.ANY)
    return _pcall(body, name="all_gather",
                  out_shape=tuple(jax.ShapeDtypeStruct((N_DEV,) + a.shape, a.dtype) for a in arrs),
                  in_specs=[hbm] * n, out_specs=(hbm,) * n,
                  scratch_shapes=[pltpu.SemaphoreType.DMA((7 * n,)), pltpu.SemaphoreType.DMA((7 * n,)),
                                  pltpu.SemaphoreType.DMA((n,))],
                  compiler_params=pltpu.CompilerParams(has_side_effects=True))(*arrs)


def _peers():
    x, y, c = _mesh_pos()
    out = []
    for k in (1, 2, 4, 3, 5, 6, 7):
        px, py, pc = (x + (k >> 2)) % 2, (y + ((k >> 1) & 1)) % 2, (c + (k & 1)) % 2
        out.append((k, (px, py, pc), 4 * px + 2 * py + pc))
    return out, 4 * x + 2 * y + c


def _peer_copies(n, scatter, src_refs, land_refs, send_sems, recv_sems):
    peers, me = _peers()
    copies = []
    for k, peer, peer_id in peers:
        for a in range(n):
            copies.append(pltpu.make_async_remote_copy(
                src_ref=src_refs[a].at[peer_id] if scatter else src_refs[a],
                dst_ref=land_refs[a].at[k - 1] if scatter else land_refs[a].at[me],
                send_sem=send_sems.at[7 * a + k - 1], recv_sem=recv_sems.at[7 * a + k - 1],
                device_id=peer, device_id_type=MESH))
    return copies


_HBM = pl.BlockSpec(memory_space=pltpu.HBM)
_SEM = pl.BlockSpec(memory_space=pltpu.SEMAPHORE)
_EFFECT = pltpu.SideEffectType.DATAFLOW_SIDE_EFFECTING


def _in_hbm(a):
    return pltpu.with_memory_space_constraint(a, pltpu.HBM)


def _exchange_start(srcs, lands, scatter, name):
    n = len(srcs)

    def body(*refs):
        send_sems, recv_sems, token = refs[2 * n], refs[2 * n + 1], refs[-1]
        for cp in _peer_copies(n, scatter, refs[:n], refs[n:2 * n], send_sems, recv_sems):
            cp.start()
        token[...] = jnp.zeros_like(token)

    thru = [pltpu.HBM(a.shape, a.dtype) for a in list(srcs) + list(lands)]
    out = _pcall(body, name=name,
                 out_shape=(pltpu.SemaphoreType.DMA((7 * n,)), pltpu.SemaphoreType.DMA((7 * n,)), *thru,
                            jax.ShapeDtypeStruct((8, 128), F32)),
                 in_specs=[_HBM] * (2 * n), out_specs=(_SEM, _SEM, *([_HBM] * (2 * n)), pl.BlockSpec(memory_space=pltpu.VMEM)),
                 input_output_aliases={i: 2 + i for i in range(2 * n)},
                 compiler_params=pltpu.CompilerParams(has_side_effects=_EFFECT))(*[_in_hbm(a) for a in list(srcs) + list(lands)])
    return out[0], out[1], out[2:2 + n], out[2 + n:2 + 2 * n], out[-1]


def _exchange_wait(send_sems, recv_sems, srcs, lands, after, scatter, name):
    n = len(srcs)

    def body(*refs):
        for cp in _peer_copies(n, scatter, refs[:n], refs[n:2 * n], refs[2 * n], refs[2 * n + 1]):
            cp.wait_send()
            cp.wait_recv()

    thru = tuple(pltpu.HBM(a.shape, a.dtype) for a in list(srcs) + list(lands))
    out = _pcall(body, name=name, out_shape=thru,
                 in_specs=[_HBM] * (2 * n) + [_SEM, _SEM, pl.BlockSpec(memory_space=pl.ANY)], out_specs=(_HBM,) * (2 * n),
                 input_output_aliases={i: i for i in range(2 * n)},
                 compiler_params=pltpu.CompilerParams(has_side_effects=_EFFECT))(*srcs, *lands, send_sems, recv_sems, after)
    return out[:n], out[n:]


def _pack(arrays, dtype):
    lead = arrays[0].shape[:-1]
    flat = jnp.concatenate([a.astype(dtype) for a in arrays], axis=-1)
    n = flat.shape[-1]
    unit = 16 * COMM_LANES
    pad = (-n) % unit
    flat = jnp.pad(flat, [(0, 0)] * len(lead) + [(0, pad)])
    return flat.reshape(lead + ((n + pad) // COMM_LANES, COMM_LANES))


def _unpack(packed, shapes):
    lead = packed.shape[:-2]
    flat = packed.reshape(lead + (-1,))
    out, off = [], 0
    for shp in shapes:
        n = math.prod(shp)
        out.append(flat[..., off:off + n].reshape(lead + tuple(shp)))
        off += n
    return out


def _unshard(g, ax):
    shp = list(g.shape[1:])
    shp[ax] *= N_DEV
    return jnp.moveaxis(g, 0, ax).reshape(shp)


def _split(full, ax):
    shp = list(full.shape)
    r = full.reshape(shp[:ax] + [N_DEV, shp[ax] // N_DEV] + shp[ax + 1:])
    return jnp.moveaxis(r, ax, 0)


def _rope_tables(s, dk):
    half = dk // 2
    inv = ROPE_BASE ** (-jnp.arange(half, dtype=F32) / half)
    ang = jnp.arange(s, dtype=F32)[:, None] * inv[None, :]
    return jnp.cos(ang), jnp.sin(ang)


def _stage_big(s):
    i, ffn = divmod(s, 2)
    if ffn:
        return [('ffn_w_up', i), ('ffn_w_down', i)]
    kind, j = i % 3, i // 3
    mixer = [['gla_w_in', 'gla_w_out'], ['lru_w_in', 'lru_w_ga', 'lru_w_gx', 'lru_w_out'], ['ret_w_in', 'ret_w_out']][kind]
    return [(n, j) for n in mixer]


def _local_step(x, target, p, layer_weights, layer_grads, token):
    s, d = x.shape
    depth = p['norm_mix_w'].shape[0]
    gla_kd, gla_vd = d // 2, d
    ret_kd, ret_vd = d, 2 * d
    cos, sin = _rope_tables(s, ret_kd // RET_HEADS)
    row = lambda v: v.reshape(1, -1)
    saved = []
    cur, gb = {}, {}

    def view(name, layer=0, dev0=0, ndev=N_DEV, arr=None):
        return _View(cur[name] if arr is None else arr, 'col' if SHARD_AX[name] == 2 else 'row', 0, dev0, ndev)

    def dw_into(name, layer, a, b, label, dev0=0, ndev=N_DEV):
        gb[name] = _mm(a, b, 'tn', BF16, name=label, out=view(name, 0, dev0, ndev, gb[name]))

    def gla_weights(j):
        w_in = cur['gla_w_in']
        w_main, w_z = w_in[:, :2 * gla_kd + 2 * gla_vd], w_in[:, 2 * gla_kd + 2 * gla_vd:]
        w_z = jnp.pad(w_z, ((0, 0), (0, GATE_PAD - GLA_GATE_RANK)))
        w_gk = jnp.pad(p['gla_w_gk'][j], ((0, GATE_PAD - GLA_GATE_RANK), (0, 0)))
        return w_main, w_z, w_gk, row(p['gla_b_gk'][j]), row(p['gla_norm_w'][j]), view('gla_w_out', j)

    def lru_weights(j):
        return (view('lru_w_in', j), p['lru_conv_w'][j], row(p['lru_conv_b'][j]), cur['lru_w_ga'], row(p['lru_b_ga'][j]),
                cur['lru_w_gx'], row(p['lru_b_gx'][j]), row(p['lru_lambda'][j]), view('lru_w_out', j))

    for i in range(depth):
        kind, j = i % 3, i // 3
        cur = layer_weights(2 * i, x)
        h = _rms_fwd(x, row(p['norm_mix_w'][i]) + (token if i == 0 else 0.0))
        if kind == 0:
            w_main, w_z, w_gk, b_gk, nw, w_out = gla_weights(j)
            proj = _mm(h, w_main, 'nn', BF16, name="gla_in")
            z = _mm(h, w_z, 'nn', F32, name="gla_z")
            y, o_st, st = _chunk_fwd(True, proj, (z, w_gk, b_gk), nw, GLA_HEADS, gla_kd, gla_vd, 4)
            mix = (proj, z, o_st, st, y)
        elif kind == 1:
            w_in, cw, cb, wga, bga, wgx, bgx, lam, w_out = lru_weights(j)
            proj = _mm(h, w_in, 'nn', F32, name="lru_in")
            y, hs = _lru_fwd(proj, cw, cb, wga, bga, wgx, bgx, lam)
            mix = (proj, hs, y)
        else:
            nw, w_out = row(p['ret_norm_w'][j]), view('ret_w_out', j)
            proj = _mm(h, view('ret_w_in', j), 'nn', BF16, name="ret_in")
            y, o_st, st = _chunk_fwd(False, proj, (cos, sin), nw, RET_HEADS, ret_kd, ret_vd, 2)
            mix = (proj, o_st, st, y)
        x_mid = _mm(y, w_out, 'nn', F32, res=x, name="mix_out")
        cur.update(layer_weights(2 * i + 1, x_mid))
        h2 = _rms_fwd(x_mid, row(p['norm_ffn_w'][i]))
        u = _mm(h2, view('ffn_w_up', i), 'nn', BF16, name="ffn_up")
        act, gate, val = _ffn_act_fwd(u, p['ffn_conv_w'][i], row(p['ffn_conv_b'][i]))
        x_out = _mm(act, view('ffn_w_down', i), 'nn', F32, res=x_mid, name="ffn_down")
        saved.append((x, h, mix, x_mid, h2, (u, gate, val), act, cur))
        x = x_out

    loss, dx, dxb, dw = _final_loss(x, row(p['norm_out_w']), target)
    g = {n: [None] * v.shape[0] for n, v in p.items() if n != 'norm_out_w'}
    g['norm_out_w'] = dw.reshape(-1)
    half = N_DEV // 2
    token = 0.0

    for i in reversed(range(depth)):
        kind, j = i % 3, i // 3
        x_in, h, mix, x_mid, h2, u, act, cur = saved[i]
        stage_names = [[n for n, _ in _stage_big(2 * i + part)] for part in (0, 1)]
        gb = {n: lax.empty(cur[n].shape, BF16) for n in stage_names[0] + stage_names[1] if n in GATHERED}
        dw_into('ffn_w_down', i, act, dxb, "ffn_down_dw")
        dact = _mm(dxb, view('ffn_w_down', i), 'nt', BF16, name="ffn_down_dx")
        dug, duv, dcw, dcb = _ffn_act_bwd(*u, p['ffn_conv_w'][i], dact)
        g['ffn_conv_w'][i], g['ffn_conv_b'][i] = dcw, dcb.reshape(-1)
        dw_into('ffn_w_up', i, h2, dug, "ffn_up_dw", 0, half)
        dw_into('ffn_w_up', i, h2, duv, "ffn_up_dw", half, half)
        dh2 = _mm(dug, view('ffn_w_up', i, 0, half), 'nt', F32, name="ffn_up_dx")
        dh2 = _mm(duv, view('ffn_w_up', i, half, half), 'nt', BF16, res=dh2, name="ffn_up_dx")
        dx, dxb, dnw = _rms_bwd(x_mid, row(p['norm_ffn_w'][i]) + token, dh2, dx)
        g['norm_ffn_w'][i] = dnw.reshape(-1)
        token = layer_grads(2 * i + 1, {n: gb[n] for n in stage_names[1]})
        if kind == 0:
            w_main, w_z, w_gk, b_gk, nw, w_out = gla_weights(j)
            proj, z, o_st, st, y = mix
            dw_into('gla_w_out', j, y, dxb, "mix_out_dw")
            dy = _mm(dxb, w_out, 'nt', BF16, name="mix_out_dx")
            dproj, dnw, dz, dwgk, dbgk = _chunk_bwd(True, proj, (z, w_gk, b_gk), nw, o_st, st, dy,
                                                    GLA_HEADS, gla_kd, gla_vd, 4)
            g['gla_norm_w'][j], g['gla_b_gk'][j] = dnw.reshape(-1), dbgk.reshape(-1)
            g['gla_w_gk'][j] = dwgk[:GLA_GATE_RANK]
            dw_main = _mm(h, dproj, 'tn', F32, name="gla_in_dw")
            dw_z = _mm(h, dz, 'tn', F32, name="gla_z_dw")
            dw_in = jnp.concatenate([dw_main, dw_z[:, :GLA_GATE_RANK]], axis=1)
            gb['gla_w_in'] = _split(dw_in, 1).astype(BF16)
            dh = _mm(dproj, w_main, 'nt', F32, name="gla_in_dx")
            dh = _mm(dz, w_z, 'nt', BF16, res=dh, name="gla_z_dx")
        elif kind == 1:
            w_in, cw, cb, wga, bga, wgx, bgx, lam, w_out = lru_weights(j)
            proj, hs, y = mix
            dw_into('lru_w_out', j, y, dxb, "mix_out_dw")
            dy = _mm(dxb, w_out, 'nt', BF16, name="mix_out_dx")
            dxb, dyb, dcw, dcb, dbga, dbgx, dlam, dwga, dwgx = _lru_bwd(proj, hs, dy, cw, cb, wga, bga, wgx, bgx, lam)
            g['lru_conv_w'][j], g['lru_conv_b'][j] = dcw, dcb.reshape(-1)
            g['lru_b_ga'][j], g['lru_b_gx'][j], g['lru_lambda'][j] = dbga.reshape(-1), dbgx.reshape(-1), dlam.reshape(-1)
            gb['lru_w_ga'], gb['lru_w_gx'] = _split(dwga, 1).astype(BF16), _split(dwgx, 1).astype(BF16)
            dproj = jnp.concatenate([dxb, dyb], axis=1)
            dw_into('lru_w_in', j, h, dproj, "lru_in_dw")
            dh = _mm(dproj, w_in, 'nt', BF16, name="lru_in_dx")
        else:
            nw, w_out = row(p['ret_norm_w'][j]), view('ret_w_out', j)
            proj, o_st, st, y = mix
            dw_into('ret_w_out', j, y, dxb, "mix_out_dw")
            dy = _mm(dxb, w_out, 'nt', BF16, name="mix_out_dx")
            dproj, dnw = _chunk_bwd(False, proj, (cos, sin), nw, o_st, st, dy, RET_HEADS, ret_kd, ret_vd, 2)
            g['ret_norm_w'][j] = dnw.reshape(-1)
            dw_into('ret_w_in', j, h, dproj, "ret_in_dw")
            dh = _mm(dproj, view('ret_w_in', j), 'nt', BF16, name="ret_in_dx")
        dx, dxb, dnw = _rms_bwd(x_in, row(p['norm_mix_w'][i]) + token, dh, dx)
        g['norm_mix_w'][i] = dnw.reshape(-1)
        token = layer_grads(2 * i, {n: gb[n] for n in stage_names[0]})

    grads = {n: (v if n == 'norm_out_w' else jnp.stack(v)) for n, v in g.items()}
    return loss[0, 0], dx, grads, token


def kernel(x, norm_mix_w, norm_ffn_w, norm_out_w, gla_w_in, gla_w_gk, gla_b_gk, gla_norm_w, gla_w_out, lru_w_in, lru_conv_w, lru_conv_b, lru_w_ga, lru_b_ga, lru_w_gx, lru_b_gx, lru_lambda, lru_w_out, ret_w_in, ret_norm_w, ret_w_out, ffn_w_up, ffn_conv_w, ffn_conv_b, ffn_w_down, loss_target, m_norm_mix_w, m_norm_ffn_w, m_norm_out_w, m_gla_w_in, m_gla_w_gk, m_gla_b_gk, m_gla_norm_w, m_gla_w_out, m_lru_w_in, m_lru_conv_w, m_lru_conv_b, m_lru_w_ga, m_lru_b_ga, m_lru_w_gx, m_lru_b_gx, m_lru_lambda, m_lru_w_out, m_ret_w_in, m_ret_norm_w, m_ret_w_out, m_ffn_w_up, m_ffn_conv_w, m_ffn_conv_b, m_ffn_w_down, v_norm_mix_w, v_norm_ffn_w, v_norm_out_w, v_gla_w_in, v_gla_w_gk, v_gla_b_gk, v_gla_norm_w, v_gla_w_out, v_lru_w_in, v_lru_conv_w, v_lru_conv_b, v_lru_w_ga, v_lru_b_ga, v_lru_w_gx, v_lru_b_gx, v_lru_lambda, v_lru_w_out, v_ret_w_in, v_ret_norm_w, v_ret_w_out, v_ffn_w_up, v_ffn_conv_w, v_ffn_conv_b, v_ffn_w_down):
    given = dict(locals())
    w = {n: given[n] for n in WEIGHTS}
    me_x, me_y, me_c = _mesh_pos()
    me = 4 * me_x + 2 * me_y + me_c

    n_stages = 2 * norm_mix_w.shape[0]

    def shards(s):
        return [w[n][j].astype(BF16) for n, j in _stage_big(s)]

    got0 = _all_gather(shards(0) + [_pack([w[n].reshape(-1) for n in SMALL_SHARDED], F32)])
    p = {n: w[n] for n in REPLICATED}
    for n, blk in zip(SMALL_SHARDED, _unpack(got0[-1], [w[n].shape for n in SMALL_SHARDED])):
        p[n] = _unshard(blk, SHARD_AX[n])
    gathers, token = {}, 0.0
    for s in range(1, n_stages):
        _, srcs = lax.optimization_barrier((got0[-1], shards(s)))
        lands =[lax.dynamic_update_index_in_dim(lax.empty((N_DEV,) + a.shape, BF16), a, me, 0) for a in srcs]
        *gathers[s], tok = _exchange_start(srcs, lands, False, "gather_start_%d" % s)
        token = token + tok[0, 0]

    def stage_weights(s, after):
        blocks = got0[:-1] if s == 0 else _exchange_wait(*gathers[s], after, False, "gather_wait_%d" % s)[1]
        out = {}
        for (n, _), blk in zip(_stage_big(s), blocks):
            out[n] = blk.reshape((N_DEV, 1) + blk.shape[1:]) if n in GATHERED else _unshard(blk, SHARD_AX[n] - 1)
        return out

    scatters = {}

    def stage_grads(s, gb):
        srcs = [gb[n].reshape((N_DEV,) + w[n].shape[1:]) for n, _ in _stage_big(s)]
        lands = [lax.empty((N_DEV - 1,) + a.shape[1:], BF16) for a in srcs]
        *scatters[s], tok = _exchange_start(srcs, lands, True, "scatter_start_%d" % s)
        return tok[0, 0]

    loss, grad_x, grads, token = _local_step(x[0], loss_target[0], p, stage_weights, stage_grads, token)
    loss = lax.psum(loss, ("x", "y", "c"))
    gw, delta, new_m, new_v = {}, {}, {}, {}

    after, big = grads['norm_out_w'] + token, {}
    for s in reversed(range(n_stages)):
        srcs, lands = _exchange_wait(*scatters[s], after, True, "scatter_wait_%d" % s)
        for (n, j), src, land in zip(_stage_big(s), srcs, lands):
            own = lax.dynamic_index_in_dim(src, me, 0, keepdims=False)
            big[n] = _adamw([own] + [(land, k) for k in range(N_DEV - 1)], w[n], given["m_" + n], given["v_" + n],
                            layer=j, prev=big.get(n))
            after = big[n][0]
    for n in BIG:
        gw[n], delta[n], new_m[n], new_v[n] = big[n]

    small = REPLICATED + SMALL_SHARDED
    _, packed = lax.optimization_barrier((after, _pack([grads[n].reshape(-1) for n in small], F32)))
    (parts,) = _all_gather([packed])
    summed = _unpack(_sum_list([parts[dev] for dev in range(N_DEV)], F32), [grads[n].shape for n in small])
    for n, gs in zip(small, summed):
        if SHARD_AX[n] is not None:
            gs = lax.dynamic_index_in_dim(_split(gs, SHARD_AX[n]), me, 0, keepdims=False)
        gw[n], delta[n], new_m[n], new_v[n] = _adamw([gs], w[n], given["m_" + n], given["v_" + n])

    return (loss, grad_x[None], *[gw[n] for n in WEIGHTS], *[delta[n] for n in WEIGHTS],
            *[new_m[n] for n in WEIGHTS], *[new_v[n] for n in WEIGHTS])
```

```python
import collections
import math

import jax
import jax.numpy as jnp
from jax import lax
from jax.experimental import pallas as pl
from jax.experimental.pallas import tpu as pltpu

F32 = jnp.float32
BF16 = jnp.bfloat16

N_DEV = 8
CHUNK = 64
RMS_EPS = 1e-6
GLA_HEADS = 4
GLA_GATE_RANK = 16
GLA_GATE_TAU = 16.0
GATE_PAD = 128
LRU_BLOCK_W = 256
LRU_C = 8.0
RET_HEADS = 8
ROPE_BASE = 10000.0
ADAM_LR, ADAM_B1, ADAM_B2, ADAM_EPS, ADAM_WD, ADAM_STEP = 0.001, 0.9, 0.999, 1e-08, 0.01, 10

HALO = 16
VMEM_LIMIT = 56 * 1024 * 1024
ROW_TILE = 256
COMM_LANES = 1024
MM_TM, MM_TN, MM_TK = 1024, 1024, 2048

MESH = pl.DeviceIdType.MESH

WEIGHTS = ['norm_mix_w', 'norm_ffn_w', 'norm_out_w', 'gla_w_in', 'gla_w_gk', 'gla_b_gk', 'gla_norm_w',
           'gla_w_out', 'lru_w_in', 'lru_conv_w', 'lru_conv_b', 'lru_w_ga', 'lru_b_ga', 'lru_w_gx',
           'lru_b_gx', 'lru_lambda', 'lru_w_out', 'ret_w_in', 'ret_norm_w', 'ret_w_out', 'ffn_w_up',
           'ffn_conv_w', 'ffn_conv_b', 'ffn_w_down']
SHARD_AX = {'norm_mix_w': None, 'norm_ffn_w': None, 'norm_out_w': None, 'gla_w_in': 2, 'gla_w_gk': 2,
            'gla_b_gk': 1, 'gla_norm_w': 1, 'gla_w_out': 1, 'lru_w_in': 2, 'lru_conv_w': 2,
            'lru_conv_b': None, 'lru_w_ga': 2, 'lru_b_ga': None, 'lru_w_gx': 2, 'lru_b_gx': None,
            'lru_lambda': None, 'lru_w_out': 1, 'ret_w_in': 2, 'ret_norm_w': 1, 'ret_w_out': 1,
            'ffn_w_up': 2, 'ffn_conv_w': 2, 'ffn_conv_b': None, 'ffn_w_down': 1}
BIG = ['gla_w_in', 'gla_w_out', 'lru_w_in', 'lru_w_ga', 'lru_w_gx', 'lru_w_out', 'ret_w_in', 'ret_w_out',
       'ffn_w_up', 'ffn_w_down']
GATHERED = ['gla_w_out', 'lru_w_in', 'lru_w_out', 'ret_w_in', 'ret_w_out', 'ffn_w_up', 'ffn_w_down']
SMALL_SHARDED = ['gla_w_gk', 'gla_b_gk', 'gla_norm_w', 'lru_conv_w', 'ret_norm_w', 'ffn_conv_w']
REPLICATED = [n for n in WEIGHTS if SHARD_AX[n] is None]


def _pcall(body, **kw):
    return pl.pallas_call(body, **kw)


def _params(sem=None, **kw):
    return pltpu.CompilerParams(dimension_semantics=sem, vmem_limit_bytes=VMEM_LIMIT, **kw)


def _tile(n, pref, align=128):
    if n <= pref:
        return n
    t = (pref // align) * align
    while t >= align:
        if n % t == 0:
            return t
        t -= align
    return n


def _row_iota(shape):
    return lax.broadcasted_iota(jnp.int32, shape, 0)


def _shift_down(x, halo, s):
    t, c = x.shape
    r = pltpu.roll(x.reshape(t // 8, 8, c), s, 1)
    prev = jnp.concatenate([pltpu.roll(halo, s, 0)[None], r[:-1]], axis=0)
    sub = lax.broadcasted_iota(jnp.int32, r.shape, 1)
    return jnp.where(sub < s, prev, r).reshape(t, c)


def _shift_up(x, nxt, s):
    t, c = x.shape
    r = pltpu.roll(x.reshape(t // 8, 8, c), 8 - s, 1)
    follow = jnp.concatenate([r[1:], pltpu.roll(nxt, 8 - s, 0)[None]], axis=0)
    sub = lax.broadcasted_iota(jnp.int32, r.shape, 1)
    return jnp.where(sub >= 8 - s, follow, r).reshape(t, c)


def _cumsum_rows(x):
    t, row, s = x.shape[0], _row_iota(x.shape), 1
    while s < t:
        x = x + jnp.where(row >= s, pltpu.roll(x, s, 0), 0.0)
        s *= 2
    return x


def _rev_cumsum_rows(x):
    t, row, s = x.shape[0], _row_iota(x.shape), 1
    while s < t:
        x = x + jnp.where(row < t - s, pltpu.roll(x, t - s, 0), 0.0)
        s *= 2
    return x


def _scan_fwd(a, u, h0):
    t, row, s = a.shape[0], _row_iota(a.shape), 1
    while s < t:
        keep = row >= s
        u = u + a * jnp.where(keep, pltpu.roll(u, s, 0), 0.0)
        a = a * jnp.where(keep, pltpu.roll(a, s, 0), 1.0)
        s *= 2
    return u + a * h0


def _scan_rev(c, g, d_end):
    t, row, s = c.shape[0], _row_iota(c.shape), 1
    while s < t:
        keep = row < t - s
        g = g + c * jnp.where(keep, pltpu.roll(g, t - s, 0), 0.0)
        c = c * jnp.where(keep, pltpu.roll(c, t - s, 0), 1.0)
        s *= 2
    return g + c * d_end


def _pick_row(x, r):
    return jnp.sum(jnp.where(_row_iota(x.shape) == r, x, 0.0), axis=0, keepdims=True)


def _sigmoid(x):
    return 1.0 / (1.0 + jnp.exp(-x))


def _softplus(x):
    return jnp.maximum(x, 0.0) + jnp.log(1.0 + jnp.exp(-jnp.abs(x)))


_GELU_C = math.sqrt(2.0 / math.pi)


def _gelu_and_grad(x):
    x2 = x * x
    th = jnp.tanh(_GELU_C * (x + 0.044715 * x * x2))
    g = 0.5 * x * (1.0 + th)
    dg = 0.5 * (1.0 + th) + 0.5 * x * (1.0 - th * th) * _GELU_C * (1.0 + 3.0 * 0.044715 * x2)
    return g, dg


def _neg_expm1(y):
    small = -(y * (1.0 + y * (0.5 + y * (1.0 / 6.0 + y * (1.0 / 24.0)))))
    return jnp.where(y > -0.01, small, 1.0 - jnp.exp(y))


def _dot(a, b, dims):
    return lax.dot_general(a.astype(BF16), b.astype(BF16), (dims, ((), ())), preferred_element_type=F32)


def _dot_nn(a, b):
    return _dot(a, b, ((1,), (0,)))


def _dot_nt(a, b):
    return _dot(a, b, ((1,), (1,)))


def _dot_tn(a, b):
    return _dot(a, b, ((0,), (0,)))


_View = collections.namedtuple("_View", "arr kind layer dev0 ndev")


def _view_shape(v):
    r, c = v.arr.shape[2:]
    return (r, v.ndev * c) if v.kind == 'col' else (N_DEV * r, c)


def _view_spec(v, tr, tc, rc_of):
    r, c = v.arr.shape[2:]
    if v.kind == 'col':
        per = c // tc

        def imap(*g):
            ri, ci = rc_of(*g)
            return (v.dev0 + ci // per, v.layer, ri, ci % per)
    elif tr > r:
        def imap_blocks(*g):
            ri, ci = rc_of(*g)
            return (ri, v.layer, 0, ci)
        return pl.BlockSpec((tr // r, None, r, tc), imap_blocks)
    else:
        per = r // tr

        def imap(*g):
            ri, ci = rc_of(*g)
            return (ri // per, v.layer, ri % per, ci)
    return pl.BlockSpec((None, None, tr, tc), imap)


def _row_tile(v, pref):
    r = v.arr.shape[2]
    if r >= pref:
        return _tile(r, pref)
    q = max(q for q in (1, 2, 4, 8) if r * q <= pref)
    return r * q


def _rows2d(val):
    return val.reshape(-1, val.shape[-1]) if val.ndim == 3 else val


def _mm(a, b, mode, out_dtype=F32, res=None, name="mm", out=None):
    bshape = _view_shape(b) if isinstance(b, _View) else b.shape
    if mode == 'nn':
        (m, k), n = a.shape, bshape[1]
    elif mode == 'nt':
        (m, k), n = a.shape, bshape[0]
    else:
        (k, m), n = a.shape, bshape[1]
    tm, tn, tk = _tile(m, MM_TM), _tile(n, MM_TN), _tile(k, MM_TK)
    if isinstance(b, _View):
        if b.kind == 'col' and mode == 'nt':
            tk = _tile(b.arr.shape[3], MM_TK)
        elif b.kind == 'col':
            tn = _tile(b.arr.shape[3], MM_TN)
        elif mode == 'nt':
            tn = _row_tile(b, MM_TN)
        else:
            tk = _row_tile(b, MM_TK)
    if out is not None:
        if out.kind == 'col':
            tn = _tile(out.arr.shape[3], MM_TN)
        else:
            tm = _row_tile(out, MM_TM)
    nk = k // tk
    a_spec = pl.BlockSpec((tk, tm), lambda i, j, kk: (kk, i)) if mode == 'tn' else pl.BlockSpec((tm, tk), lambda i, j, kk: (i, kk))
    if isinstance(b, _View):
        b_spec = (_view_spec(b, tn, tk, lambda i, j, kk: (j, kk)) if mode == 'nt'
                  else _view_spec(b, tk, tn, lambda i, j, kk: (kk, j)))
    else:
        b_spec = pl.BlockSpec((tn, tk), lambda i, j, kk: (j, kk)) if mode == 'nt' else pl.BlockSpec((tk, tn), lambda i, j, kk: (kk, j))
    r_spec = pl.BlockSpec((tm, tn), lambda i, j, kk: (i, j))
    o_spec = r_spec if out is None else _view_spec(out, tm, tn, lambda i, j, kk: (i, j))
    dot = {'nn': _dot_nn, 'nt': _dot_nt, 'tn': _dot_tn}[mode]

    def body(*refs):
        refs = list(refs)
        if out is not None:
            del refs[2 + (res is not None)]
        a_ref, b_ref = refs[:2]
        r_ref = None if res is None else refs[2]
        o_ref = refs[2 if res is None else 3]

        def finish(total):
            if res is not None:
                total = total + r_ref[...].astype(F32)
            o_ref[...] = total.astype(out_dtype).reshape(o_ref.shape)

        part = dot(a_ref[...], _rows2d(b_ref[...]))
        if nk == 1:
            finish(part)
            return
        acc = refs[-1]
        kk = pl.program_id(2)

        @pl.when(kk == 0)
        def _():
            acc[...] = part

        @pl.when(kk > 0)
        def _():
            acc[...] += part

        @pl.when(kk == nk - 1)
        def _():
            finish(acc[...])

    args, specs = [a, b.arr if isinstance(b, _View) else b], [a_spec, b_spec]
    if res is not None:
        args.append(res)
        specs.append(r_spec)
    aliases, out_shape = {}, jax.ShapeDtypeStruct((m, n), out_dtype)
    if out is not None:
        aliases, out_shape = {len(args): 0}, jax.ShapeDtypeStruct(out.arr.shape, out.arr.dtype)
        args.append(out.arr)
        specs.append(pl.BlockSpec(memory_space=pl.ANY))
    return _pcall(body, name=name, out_shape=out_shape,
                  grid=(m // tm, n // tn, nk), in_specs=specs, out_specs=o_spec,
                  scratch_shapes=[] if nk == 1 else [pltpu.VMEM((tm, tn), F32)], input_output_aliases=aliases,
                  compiler_params=_params(("parallel", "parallel", "arbitrary")))(*args)


def _rms_fwd(x, w):
    s, d = x.shape
    tr = _tile(s, ROW_TILE, 16)

    def body(x_ref, w_ref, o_ref):
        xv = x_ref[...]
        r = lax.rsqrt(jnp.mean(xv * xv, axis=-1, keepdims=True) + RMS_EPS)
        o_ref[...] = (xv * r * w_ref[...]).astype(BF16)

    return _pcall(body, name="rms_fwd", out_shape=jax.ShapeDtypeStruct((s, d), BF16), grid=(s // tr,),
                  in_specs=[pl.BlockSpec((tr, d), lambda i: (i, 0)), pl.BlockSpec((1, d), lambda i: (0, 0))],
                  out_specs=pl.BlockSpec((tr, d), lambda i: (i, 0)), compiler_params=_params(("parallel",)))(x, w)


def _rms_bwd(x, w, dh, dres):
    s, d = x.shape
    tr = _tile(s, ROW_TILE, 16)

    def body(x_ref, w_ref, dh_ref, dr_ref, dx_ref, dxb_ref, dw_ref):
        i = pl.program_id(0)
        xv = x_ref[...]
        r = lax.rsqrt(jnp.mean(xv * xv, axis=-1, keepdims=True) + RMS_EPS)
        xh = xv * r
        dhv = dh_ref[...].astype(F32)
        dxh = dhv * w_ref[...]
        dxv = dr_ref[...] + r * (dxh - xh * jnp.mean(dxh * xh, axis=-1, keepdims=True))
        dx_ref[...] = dxv
        dxb_ref[...] = dxv.astype(BF16)
        part = jnp.sum(dhv * xh, axis=0, keepdims=True)

        @pl.when(i == 0)
        def _():
            dw_ref[...] = part

        @pl.when(i > 0)
        def _():
            dw_ref[...] += part

    row = pl.BlockSpec((tr, d), lambda i: (i, 0))
    vec = pl.BlockSpec((1, d), lambda i: (0, 0))
    return _pcall(body, name="rms_bwd",
                  out_shape=(jax.ShapeDtypeStruct((s, d), F32), jax.ShapeDtypeStruct((s, d), BF16),
                             jax.ShapeDtypeStruct((1, d), F32)),
                  grid=(s // tr,), in_specs=[row, vec, row, row], out_specs=(row, row, vec),
                  compiler_params=_params(("arbitrary",)))(x, w, dh, dres)


def _final_loss(x, w, target):
    s, d = x.shape
    tr = _tile(s, ROW_TILE, 16)

    def body(x_ref, w_ref, t_ref, l_ref, dx_ref, dxb_ref, dw_ref):
        i = pl.program_id(0)
        xv = x_ref[...]
        r = lax.rsqrt(jnp.mean(xv * xv, axis=-1, keepdims=True) + RMS_EPS)
        xh = xv * r
        err = xh * w_ref[...] - t_ref[...]
        lpart = 0.5 * jnp.sum(jnp.mean(err * err, axis=-1, keepdims=True), axis=0, keepdims=True)
        dy = err * (1.0 / d)
        dxh = dy * w_ref[...]
        dxv = r * (dxh - xh * jnp.mean(dxh * xh, axis=-1, keepdims=True))
        dx_ref[...] = dxv
        dxb_ref[...] = dxv.astype(BF16)
        part = jnp.sum(dy * xh, axis=0, keepdims=True)

        @pl.when(i == 0)
        def _():
            dw_ref[...] = part
            l_ref[...] = jnp.broadcast_to(lpart, l_ref.shape)

        @pl.when(i > 0)
        def _():
            dw_ref[...] += part
            l_ref[...] += jnp.broadcast_to(lpart, l_ref.shape)

    row = pl.BlockSpec((tr, d), lambda i: (i, 0))
    vec = pl.BlockSpec((1, d), lambda i: (0, 0))
    return _pcall(body, name="final_loss",
                  out_shape=(jax.ShapeDtypeStruct((8, 128), F32), jax.ShapeDtypeStruct((s, d), F32),
                             jax.ShapeDtypeStruct((s, d), BF16), jax.ShapeDtypeStruct((1, d), F32)),
                  grid=(s // tr,), in_specs=[row, vec, row],
                  out_specs=(pl.BlockSpec((8, 128), lambda i: (0, 0)), row, row, vec),
                  compiler_params=_params(("arbitrary",)))(x, w, target)


def _halo_prev(tt, cmap):
    return lambda *g: (jnp.maximum(g[-1] * (tt // HALO) - 1, 0), cmap(*g))


def _ffn_up_act(h, w, cw, cb):
    s, k = h.shape
    c = w.arr.shape[3]
    half = w.ndev // 2
    f = half * c
    tm, tn = _tile(s, 512, 16), _tile(c, MM_TN)
    per, nj = c // tn, f // tn

    def body(h_ref, hh_ref, wg_ref, wv_ref, cwg_ref, cwv_ref, cbg_ref, cbv_ref,
             a_ref, gate_ref, val_ref, ug_ref, uv_ref):
        i = pl.program_id(1)
        hm, hh = h_ref[...], hh_ref[...]

        def conv_half(w_ref, cw_ref, cb_ref, u_ref):
            wt = w_ref[...]
            u = _dot_nn(hm, wt)
            hal = jnp.where(i > 0, _dot_nn(hh, wt)[8:16], 0.0)
            u_ref[...] = u.astype(BF16)
            return (cw_ref[2:3, :] * u + cw_ref[1:2, :] * _shift_down(u, hal, 1)
                    + cw_ref[0:1, :] * _shift_down(u, hal, 2) + cb_ref[...])

        gate = conv_half(wg_ref, cwg_ref, cbg_ref, ug_ref)
        val = conv_half(wv_ref, cwv_ref, cbv_ref, uv_ref)
        a_ref[...] = (_gelu_and_grad(gate)[0] * val).astype(BF16)
        gate_ref[...] = gate.astype(BF16)
        val_ref[...] = val.astype(BF16)

    def wspec(dev0):
        return pl.BlockSpec((None, None, k, tn), lambda j, i: (w.dev0 + dev0 + j // per, w.layer, 0, j % per))

    def vec(rows, off):
        return pl.BlockSpec((rows, tn), lambda j, i: (0, j + off))

    out = pl.BlockSpec((tm, tn), lambda j, i: (i, j))
    return _pcall(body, name="ffn_up_act", out_shape=(jax.ShapeDtypeStruct((s, f), BF16),) * 5, grid=(nj, s // tm),
                  in_specs=[pl.BlockSpec((tm, k), lambda j, i: (i, 0)),
                            pl.BlockSpec((HALO, k), lambda j, i: (jnp.maximum(i * (tm // HALO) - 1, 0), 0)),
                            wspec(0), wspec(half), vec(3, 0), vec(3, nj), vec(1, 0), vec(1, nj)],
                  out_specs=(out,) * 5,
                  compiler_params=_params(("parallel", "parallel")))(h, h, w.arr, w.arr, cw, cw, cb, cb)


def _ffn_act_bwd(ug, uv, gate, val, cw, da):
    s, f = ug.shape
    tt, tc = _tile(s, ROW_TILE, 16), _tile(f, 512)
    nc, nt = f // tc, s // tt

    def body(ug_ref, uv_ref, gate_ref, val_ref, da_ref, wg_ref, wv_ref,
             dug_ref, duv_ref, dwg_ref, dwv_ref, dbg_ref, dbv_ref, carry):
        t = pl.program_id(1)

        @pl.when(t == 0)
        def _():
            carry[...] = jnp.zeros_like(carry)

        dav = da_ref[...].astype(F32)
        gl, dgl = _gelu_and_grad(gate_ref[...].astype(F32))
        dgate, dval = dav * val_ref[...].astype(F32) * dgl, dav * gl

        def back(k, d, x_ref, w_ref, du_ref, dw_ref, db_ref):
            nxt = carry[k]
            ds = [_shift_up(d, nxt, 2), _shift_up(d, nxt, 1), d]
            du_ref[...] = (w_ref[2:3, :] * ds[2] + w_ref[1:2, :] * ds[1] + w_ref[0:1, :] * ds[0]).astype(BF16)
            carry[k] = d[0:8]
            x = x_ref[...].astype(F32)
            parts = [jnp.sum(ds[r] * x, axis=0, keepdims=True) for r in range(3)]
            bpart = jnp.sum(d, axis=0, keepdims=True)

            @pl.when(t == 0)
            def _():
                for r in range(3):
                    dw_ref[r:r + 1, :] = parts[r]
                db_ref[...] = bpart

            @pl.when(t > 0)
            def _():
                for r in range(3):
                    dw_ref[r:r + 1, :] += parts[r]
                db_ref[...] += bpart

        back(0, dgate, ug_ref, wg_ref, dug_ref, dwg_ref, dbg_ref)
        back(1, dval, uv_ref, wv_ref, duv_ref, dwv_ref, dbv_ref)

    def main(off):
        return pl.BlockSpec((tt, tc), lambda j, t: (nt - 1 - t, j + off))

    def vec(rows, off):
        return pl.BlockSpec((rows, tc), lambda j, t: (0, j + off))

    dug, duv, dwg, dwv, dbg, dbv = _pcall(
        body, name="ffn_act_bwd",
        out_shape=(jax.ShapeDtypeStruct((s, f), BF16), jax.ShapeDtypeStruct((s, f), BF16),
                   jax.ShapeDtypeStruct((3, f), F32), jax.ShapeDtypeStruct((3, f), F32),
                   jax.ShapeDtypeStruct((1, f), F32), jax.ShapeDtypeStruct((1, f), F32)),
        grid=(nc, nt),
        in_specs=[main(0), main(0), main(0), main(0), main(0), vec(3, 0), vec(3, nc)],
        out_specs=(main(0), main(0), vec(3, 0), vec(3, 0), vec(1, 0), vec(1, 0)),
        scratch_shapes=[pltpu.VMEM((2, 8, tc), F32)],
        compiler_params=_params(("parallel", "arbitrary")))(ug, uv, gate, val, da, cw, cw)
    return dug, duv, jnp.concatenate([dwg, dwv], axis=1), jnp.concatenate([dbg, dbv], axis=1)


def _chunk_cols(h, kd, vd, heads):
    dk, dv = kd // heads, vd // heads
    return (slice(h * dk, (h + 1) * dk), slice(kd + h * dk, kd + (h + 1) * dk),
            slice(2 * kd + h * dv, 2 * kd + (h + 1) * dv), slice(2 * kd + vd + h * dv, 2 * kd + vd + (h + 1) * dv))


def _rope(x, cos, sin):
    half = x.shape[1] // 2
    x1, x2 = x[:, :half], x[:, half:]
    return jnp.concatenate([x1 * cos - x2 * sin, x2 * cos + x1 * sin], axis=1)


def _unrope(d, cos, sin):
    half = d.shape[1] // 2
    d1, d2 = d[:, :half], d[:, half:]
    return jnp.concatenate([d1 * cos + d2 * sin, d2 * cos - d1 * sin], axis=1)


def _chunk_inputs(gla, h, heads, kd, vd, rows, proj_ref, aux):
    qc, kc, vc, gc = _chunk_cols(h, kd, vd, heads)
    dk = kd // heads
    q = proj_ref[rows, qc].astype(F32)
    k = proj_ref[rows, kc].astype(F32)
    v = proj_ref[rows, vc]
    g = proj_ref[rows, gc].astype(F32)
    c = {}
    if gla:
        z_ref, wgk_ref, bgk_ref = aux
        c['z'] = z_ref[rows, :]
        c['gk'] = _dot_nn(c['z'], wgk_ref[:, qc]) + bgk_ref[:, qc]
        la = (jnp.minimum(c['gk'], 0.0) - jnp.log(1.0 + jnp.exp(-jnp.abs(c['gk'])))) * (1.0 / GLA_GATE_TAU)
        b = _cumsum_rows(la)
        bl = jnp.sum(la, axis=0, keepdims=True)
        q = q * (dk ** -0.5)
    else:
        cos_ref, sin_ref = aux
        c['cos'], c['sin'] = cos_ref[rows, :], sin_ref[rows, :]
        q = _rope(q, c['cos'], c['sin'])
        k = _rope(k, c['cos'], c['sin']) * (dk ** -0.5)
        lg = math.log(1.0 - 2.0 ** (-5.0 - h))
        b = lg * (_row_iota((CHUNK, 1)).astype(F32) + 1.0)
        bl = jnp.full((1, 1), lg * CHUNK, F32)
    eb, enb = jnp.exp(b), jnp.exp(-b)
    c.update(q=q, k=k, v=v, g=g, b=b, bl=bl, eb=eb, enb=enb, ebl=jnp.exp(bl),
             qd=q * eb, kg=k * enb, qg=q * enb, kd=k * eb, ks=k * jnp.exp(bl - b))
    lower = _row_iota((CHUNK, CHUNK)) >= lax.broadcasted_iota(jnp.int32, (CHUNK, CHUNK), 1)
    c['lower'] = lower
    c['A'] = jnp.where(lower, _dot_nt(c['qd'], c['kg']), _dot_nt(c['qg'], c['kd']))
    return c


def _head_norm(gla, o):
    if not gla:
        o = o - jnp.mean(o, axis=-1, keepdims=True)
    r = lax.rsqrt(jnp.mean(o * o, axis=-1, keepdims=True) + RMS_EPS)
    return o * r, r


def _chunk_fwd(gla, proj, aux_arrays, nw, heads, kd, vd, cps):
    s, pw = proj.shape
    dk, dv = kd // heads, vd // heads
    rt = CHUNK * cps
    nb = s // rt
    n_aux = len(aux_arrays)

    def body(*refs):
        proj_ref, aux, nw_ref = refs[0], refs[1:1 + n_aux], refs[1 + n_aux]
        y_ref, o_ref, st_ref, state = refs[2 + n_aux:]

        @pl.when(pl.program_id(0) == 0)
        def _():
            state[...] = jnp.zeros_like(state)

        def chunk(ci, carry):
            rows = pl.ds(pl.multiple_of(ci * CHUNK, CHUNK), CHUNK)
            for h in range(heads):
                c = _chunk_inputs(gla, h, heads, kd, vd, rows, proj_ref, aux)
                vcols = slice(h * dv, (h + 1) * dv)
                st0 = state[h]
                st_ref[ci, h] = st0.astype(BF16)
                o = _dot_nn(c['A'], c['v']) + _dot_nt(c['qd'], st0)
                state[h] = st0 * c['ebl'] + _dot_tn(c['v'], c['ks'])
                oh, _ = _head_norm(gla, o)
                gv = c['g']
                y_ref[rows, vcols] = (oh * nw_ref[:, vcols] * (gv * _sigmoid(gv))).astype(BF16)
                o_ref[rows, vcols] = o.astype(BF16)
            return carry

        lax.fori_loop(0, cps, chunk, 0)

    row = lambda w: pl.BlockSpec((rt, w), lambda n: (n, 0))
    full = lambda a: pl.BlockSpec(a.shape, lambda n: (0,) * a.ndim)
    aux_specs = [row(a.shape[1]) if a.shape[0] == s else full(a) for a in aux_arrays]
    return _pcall(
        body, name="gla_fwd" if gla else "ret_fwd",
        out_shape=(jax.ShapeDtypeStruct((s, vd), BF16), jax.ShapeDtypeStruct((s, vd), BF16),
                   jax.ShapeDtypeStruct((s // CHUNK, heads, dv, dk), BF16)),
        grid=(nb,), in_specs=[row(pw)] + aux_specs + [full(nw)],
        out_specs=(row(vd), row(vd), pl.BlockSpec((cps, heads, dv, dk), lambda n: (n, 0, 0, 0))),
        scratch_shapes=[pltpu.VMEM((heads, dv, dk), F32)],
        compiler_params=_params(("arbitrary",)))(proj, *aux_arrays, nw)


def _chunk_bwd(gla, proj, aux_arrays, nw, o_st, st, dy, heads, kd, vd, cps):
    s, pw = proj.shape
    dk, dv = kd // heads, vd // heads
    rt = CHUNK * cps
    nb = s // rt
    n_aux = len(aux_arrays)

    def body(*refs):
        proj_ref, aux, nw_ref = refs[0], refs[1:1 + n_aux], refs[1 + n_aux]
        o_ref, st_ref, dy_ref = refs[2 + n_aux:5 + n_aux]
        outs = refs[5 + n_aux:]
        dp_ref, dnw_ref = outs[0], outs[1]
        if gla:
            dz_ref, dwgk_ref, dbgk_ref, dstate = outs[2:]
        else:
            dstate = outs[2]

        @pl.when(pl.program_id(0) == 0)
        def _():
            dstate[...] = jnp.zeros_like(dstate)
            dnw_ref[...] = jnp.zeros_like(dnw_ref)
            if gla:
                dwgk_ref[...] = jnp.zeros_like(dwgk_ref)
                dbgk_ref[...] = jnp.zeros_like(dbgk_ref)

        def chunk(i, carry):
            ci = cps - 1 - i
            rows = pl.ds(pl.multiple_of(ci * CHUNK, CHUNK), CHUNK)
            dz = jnp.zeros((CHUNK, GATE_PAD), F32)
            for h in range(heads):
                c = _chunk_inputs(gla, h, heads, kd, vd, rows, proj_ref, aux)
                qc, kc, vc, gc = _chunk_cols(h, kd, vd, heads)
                vcols = slice(h * dv, (h + 1) * dv)
                o = o_ref[rows, vcols].astype(F32)
                oh, r = _head_norm(gla, o)
                dyv = dy_ref[rows, vcols].astype(F32)
                gv = c['g']
                sg = _sigmoid(gv)
                nwv = nw_ref[:, vcols]
                dp_ref[rows, gc] = (dyv * oh * nwv * (sg * (1.0 + gv * (1.0 - sg)))).astype(BF16)
                dn = dyv * (gv * sg)
                dnw_ref[:, vcols] += jnp.sum(dn * oh, axis=0, keepdims=True)
                doh = dn * nwv
                do = doh - oh * jnp.mean(doh * oh, axis=-1, keepdims=True)
                if not gla:
                    do = do - jnp.mean(doh, axis=-1, keepdims=True)
                do = r * do
                st0 = st_ref[ci, h]
                dst1 = dstate[h]
                v = c['v']
                da = _dot_nt(do, v)
                dal = jnp.where(c['lower'], da, 0.0)
                dau = da - dal
                dp_ref[rows, vc] = (_dot_tn(c['A'], do) + _dot_nt(c['ks'], dst1)).astype(BF16)
                dqd = _dot_nn(dal, c['kg']) + _dot_nn(do, st0)
                dkg = _dot_tn(dal, c['qd'])
                dqg = _dot_nn(dau, c['kd'])
                dkd = _dot_tn(dau, c['qg'])
                dks = _dot_nn(v, dst1)
                dstate[h] = _dot_tn(do, c['qd']) + dst1 * c['ebl']
                dq = dqd * c['eb'] + dqg * c['enb']
                dkk = dkg * c['enb'] + dkd * c['eb'] + dks * jnp.exp(c['bl'] - c['b'])
                if gla:
                    db = dqd * c['qd'] - dkg * c['kg'] - dqg * c['qg'] + dkd * c['kd'] - dks * c['ks']
                    dbl = (jnp.sum(dks * c['ks'], axis=0, keepdims=True)
                           + c['ebl'] * jnp.sum(dst1 * st0.astype(F32), axis=0, keepdims=True))
                    db = db + jnp.where(_row_iota(db.shape) == CHUNK - 1, dbl, 0.0)
                    dgk = _rev_cumsum_rows(db) * (1.0 / GLA_GATE_TAU) / (1.0 + jnp.exp(c['gk']))
                    _, wgk_ref, _ = aux
                    dz = dz + _dot_nt(dgk, wgk_ref[:, qc])
                    dwgk_ref[:, qc] += _dot_tn(c['z'], dgk)
                    dbgk_ref[:, qc] += jnp.sum(dgk, axis=0, keepdims=True)
                    dp_ref[rows, qc] = (dq * (dk ** -0.5)).astype(BF16)
                    dp_ref[rows, kc] = dkk.astype(BF16)
                else:
                    dp_ref[rows, qc] = _unrope(dq, c['cos'], c['sin']).astype(BF16)
                    dp_ref[rows, kc] = (_unrope(dkk, c['cos'], c['sin']) * (dk ** -0.5)).astype(BF16)
            if gla:
                dz_ref[rows, :] = dz
            return carry

        lax.fori_loop(0, cps, chunk, 0)

    row = lambda w: pl.BlockSpec((rt, w), lambda n: (nb - 1 - n, 0))
    full = lambda a: pl.BlockSpec(a.shape, lambda n: (0,) * a.ndim)
    aux_specs = [row(a.shape[1]) if a.shape[0] == s else full(a) for a in aux_arrays]
    out_shape = [jax.ShapeDtypeStruct((s, pw), BF16), jax.ShapeDtypeStruct((1, vd), F32)]
    out_specs = [row(pw), full(nw)]
    if gla:
        wgk, bgk = aux_arrays[1], aux_arrays[2]
        out_shape += [jax.ShapeDtypeStruct((s, GATE_PAD), F32), jax.ShapeDtypeStruct(wgk.shape, F32),
                      jax.ShapeDtypeStruct(bgk.shape, F32)]
        out_specs += [row(GATE_PAD), full(wgk), full(bgk)]
    return _pcall(
        body, name="gla_bwd" if gla else "ret_bwd", out_shape=tuple(out_shape), grid=(nb,),
        in_specs=[row(pw)] + aux_specs + [full(nw), row(vd),
                                          pl.BlockSpec((cps, heads, dv, dk), lambda n: (nb - 1 - n, 0, 0, 0)), row(vd)],
        out_specs=tuple(out_specs), scratch_shapes=[pltpu.VMEM((heads, dv, dk), F32)],
        compiler_params=_params(("arbitrary",)))(proj, *aux_arrays, nw, o_st, st, dy)


def _lru_gates(xc, wga_ref, bga_ref, wgx_ref, bgx_ref, lam_ref):
    r = _sigmoid(_dot_nn(xc, wga_ref[0]) + bga_ref[...])
    i = _sigmoid(_dot_nn(xc, wgx_ref[0]) + bgx_ref[...])
    sp = _softplus(-lam_ref[...])
    la = -LRU_C * r * sp
    return r, i, sp, la, jnp.exp(la), jnp.sqrt(_neg_expm1(2.0 * la))


def _lru_specs(w, nbk, tt, tmap):
    main = lambda off: pl.BlockSpec((tt, LRU_BLOCK_W), lambda n, t: (tmap(t), n + off))
    halo = pl.BlockSpec((HALO, LRU_BLOCK_W), lambda n, t: (jnp.maximum(tmap(t) * (tt // HALO) - 1, 0), n))
    vec = lambda rows: pl.BlockSpec((rows, LRU_BLOCK_W), lambda n, t: (0, n))
    mat = pl.BlockSpec((1, LRU_BLOCK_W, LRU_BLOCK_W), lambda n, t: (n, 0, 0))
    return main, halo, vec, mat


def _lru_fwd(proj, cw, cb, wga, bga, wgx, bgx, lam):
    s, w2 = proj.shape
    w = w2 // 2
    nbk, tt = w // LRU_BLOCK_W, _tile(s, ROW_TILE, 16)
    main, halo, vec, mat = _lru_specs(w, nbk, tt, lambda t: t)

    def body(x_ref, xh_ref, y_ref, cw_ref, cb_ref, wga_ref, bga_ref, wgx_ref, bgx_ref, lam_ref,
             out_ref, hs_ref, hcar):
        t = pl.program_id(1)
        x = x_ref[...]
        hal = jnp.where(t > 0, xh_ref[8:16, :], 0.0)
        xc = (cw_ref[3:4, :] * x + cw_ref[2:3, :] * _shift_down(x, hal, 1) + cw_ref[1:2, :] * _shift_down(x, hal, 2)
              + cw_ref[0:1, :] * _shift_down(x, hal, 3) + cb_ref[...])
        r, i, sp, la, a, mlt = _lru_gates(xc, wga_ref, bga_ref, wgx_ref, bgx_ref, lam_ref)

        @pl.when(t == 0)
        def _():
            hcar[...] = jnp.zeros_like(hcar)

        h = _scan_fwd(a, xc * i * mlt, hcar[...])
        hcar[...] = _pick_row(h, tt - 1)
        hs_ref[...] = h
        out_ref[...] = (h * _gelu_and_grad(y_ref[...])[0]).astype(BF16)

    return _pcall(body, name="lru_fwd",
                  out_shape=(jax.ShapeDtypeStruct((s, w), BF16), jax.ShapeDtypeStruct((s, w), F32)),
                  grid=(nbk, s // tt),
                  in_specs=[main(0), halo, main(nbk), vec(4), vec(1), mat, vec(1), mat, vec(1), vec(1)],
                  out_specs=(main(0), main(0)), scratch_shapes=[pltpu.VMEM((1, LRU_BLOCK_W), F32)],
                  compiler_params=_params(("parallel", "arbitrary")))(proj, proj, proj, cw, cb, wga, bga, wgx, bgx, lam)


def _lru_bwd(proj, hs, dout, cw, cb, wga, bga, wgx, bgx, lam):
    s, w2 = proj.shape
    w = w2 // 2
    nbk, tt = w // LRU_BLOCK_W, _tile(s, ROW_TILE, 16)
    nt = s // tt
    main, halo, vec, mat = _lru_specs(w, nbk, tt, lambda t: nt - 1 - t)

    def body(x_ref, xh_ref, y_ref, hs_ref, hh_ref, do_ref, cw_ref, cb_ref, wga_ref, bga_ref, wgx_ref, bgx_ref,
             lam_ref, dx_ref, dy_ref, dcw_ref, dcb_ref, dbga_ref, dbgx_ref, dlam_ref, dwga_ref, dwgx_ref,
             dhcar, dxcar):
        t = pl.program_id(1)
        first_tile = t == nt - 1

        @pl.when(t == 0)
        def _():
            dhcar[...] = jnp.zeros_like(dhcar)
            dxcar[...] = jnp.zeros_like(dxcar)

        x = x_ref[...]
        hal = jnp.where(first_tile, 0.0, xh_ref[8:16, :])
        xs = [x, _shift_down(x, hal, 1), _shift_down(x, hal, 2), _shift_down(x, hal, 3)]
        xc = cw_ref[3:4, :] * xs[0] + cw_ref[2:3, :] * xs[1] + cw_ref[1:2, :] * xs[2] + cw_ref[0:1, :] * xs[3] + cb_ref[...]
        r, i, sp, la, a, mlt = _lru_gates(xc, wga_ref, bga_ref, wgx_ref, bgx_ref, lam_ref)
        h = hs_ref[...]
        hprev = _shift_down(h, jnp.where(first_tile, 0.0, hh_ref[8:16, :]), 1)
        gl, dgl = _gelu_and_grad(y_ref[...])
        dov = do_ref[...].astype(F32)
        dy_ref[...] = (dov * h * dgl).astype(BF16)
        row = _row_iota(a.shape)
        coef = jnp.where(row == tt - 1, 1.0, pltpu.roll(a, tt - 1, 0))
        dh = _scan_rev(coef, dov * gl, dhcar[...])
        dhcar[...] = _pick_row(a * dh, 0)
        dxc = dh * i * mlt
        di = dh * xc * mlt
        dm = dh * xc * i
        dla = dh * hprev * a - dm * jnp.exp(2.0 * la) / mlt
        dpa = dla * (-LRU_C * sp) * r * (1.0 - r)
        dpx = di * i * (1.0 - i)
        dxc = dxc + _dot_nt(dpa, wga_ref[0]) + _dot_nt(dpx, wgx_ref[0])
        nxt = dxcar[...]
        dx_ref[...] = (cw_ref[3:4, :] * dxc + cw_ref[2:3, :] * _shift_up(dxc, nxt, 1)
                       + cw_ref[1:2, :] * _shift_up(dxc, nxt, 2) + cw_ref[0:1, :] * _shift_up(dxc, nxt, 3)).astype(BF16)
        dxcar[...] = dxc[0:8]
        colsum = lambda v: jnp.sum(v, axis=0, keepdims=True)
        parts = [(dcb_ref, colsum(dxc)), (dbga_ref, colsum(dpa)), (dbgx_ref, colsum(dpx)),
                 (dlam_ref, colsum(dla * LRU_C * r) * _sigmoid(-lam_ref[...]))]
        wparts = [colsum(dxc * xs[3 - k]) for k in range(4)]
        dwa, dwx = _dot_tn(xc, dpa), _dot_tn(xc, dpx)

        @pl.when(t == 0)
        def _():
            for ref, val in parts:
                ref[...] = val
            for k in range(4):
                dcw_ref[k:k + 1, :] = wparts[k]
            dwga_ref[0] = dwa
            dwgx_ref[0] = dwx

        @pl.when(t > 0)
        def _():
            for ref, val in parts:
                ref[...] += val
            for k in range(4):
                dcw_ref[k:k + 1, :] += wparts[k]
            dwga_ref[0] += dwa
            dwgx_ref[0] += dwx

    sd = jax.ShapeDtypeStruct
    return _pcall(
        body, name="lru_bwd",
        out_shape=(sd((s, w), BF16), sd((s, w), BF16), sd((4, w), F32), sd((1, w), F32), sd((1, w), F32),
                   sd((1, w), F32), sd((1, w), F32), sd(wga.shape, F32), sd(wgx.shape, F32)),
        grid=(nbk, nt),
        in_specs=[main(0), halo, main(nbk), main(0), halo, main(0), vec(4), vec(1), mat, vec(1), mat, vec(1), vec(1)],
        out_specs=(main(0), main(0), vec(4), vec(1), vec(1), vec(1), vec(1), mat, mat),
        scratch_shapes=[pltpu.VMEM((1, LRU_BLOCK_W), F32), pltpu.VMEM((8, LRU_BLOCK_W), F32)],
        compiler_params=_params(("parallel", "arbitrary")))(proj, proj, proj, hs, hs, dout, cw, cb, wga, bga, wgx, bgx, lam)


def _adamw(parts, w, m, v, layer=None, prev=None):
    shape = w.shape
    cols = shape[-1]
    layers = 1 if layer is None else shape[0]
    w3, m3, v3 = (a.reshape(layers, -1, cols) for a in (w, m, v))
    rows = w3.shape[1]
    tr = _tile(rows, max(16, (1 << 17) // cols // 16 * 16), 16)
    blk = pl.BlockSpec((None, tr, cols), lambda i: (layer or 0, i, 0))
    p_args, p_specs = [], []
    for part in parts:
        if isinstance(part, tuple):
            stack, idx = part
            p_args.append(stack.reshape(stack.shape[0], rows, cols))
            p_specs.append(pl.BlockSpec((None, tr, cols), lambda i, idx=idx: (idx, i, 0)))
        else:
            p_args.append(part.reshape(rows, cols))
            p_specs.append(pl.BlockSpec((tr, cols), lambda i: (i, 0)))
    n_parts = len(parts)
    n_prev = 0 if prev is None else 4

    def body(*refs):
        w_ref, m_ref, v_ref = refs[n_parts:n_parts + 3]
        g_ref, d_ref, nm_ref, nv_ref = refs[n_parts + 3 + n_prev:]
        gv = refs[0][...].astype(F32)
        for p_ref in refs[1:n_parts]:
            gv = gv + p_ref[...].astype(F32)
        g_ref[...] = gv
        nm = ADAM_B1 * m_ref[...] + (1.0 - ADAM_B1) * gv
        nv = ADAM_B2 * v_ref[...] + (1.0 - ADAM_B2) * (gv * gv)
        m_hat = nm / (1.0 - ADAM_B1 ** ADAM_STEP)
        v_hat = nv / (1.0 - ADAM_B2 ** ADAM_STEP)
        d_ref[...] = -ADAM_LR * (m_hat / (jnp.sqrt(v_hat) + ADAM_EPS) + ADAM_WD * w_ref[...])
        nm_ref[...] = nm
        nv_ref[...] = nv

    prev_args = [] if prev is None else [a.reshape(layers, rows, cols) for a in prev]
    n_in = n_parts + 3
    out = _pcall(body, name="adamw", out_shape=tuple(jax.ShapeDtypeStruct((layers, rows, cols), F32) for _ in range(4)),
                 grid=(rows // tr,), in_specs=p_specs + [blk] * 3 + [pl.BlockSpec(memory_space=pl.ANY)] * n_prev,
                 out_specs=(blk,) * 4, input_output_aliases={n_in + q: q for q in range(n_prev)},
                 compiler_params=_params(("parallel",)))(*p_args, w3, m3, v3, *prev_args)
    return tuple(o.reshape(shape) for o in out)


def _sum_list(arrs, out_dtype):
    shape = arrs[0].shape
    cols = shape[-1]
    flat = [a.reshape(-1, cols) for a in arrs]
    rows = flat[0].shape[0]
    tr = _tile(rows, max(16, (1 << 18) // cols // 16 * 16), 16)

    def body(*refs):
        acc = refs[0][...].astype(F32)
        for p_ref in refs[1:-1]:
            acc = acc + p_ref[...].astype(F32)
        refs[-1][...] = acc.astype(out_dtype)

    blk = pl.BlockSpec((tr, cols), lambda i: (i, 0))
    out = _pcall(body, name="sum_list", out_shape=jax.ShapeDtypeStruct((rows, cols), out_dtype), grid=(rows // tr,),
                 in_specs=[blk] * len(flat), out_specs=blk, compiler_params=_params(("parallel",)))(*flat)
    return out.reshape(shape)


def _mesh_pos():
    return lax.axis_index("x"), lax.axis_index("y"), lax.axis_index("c")


def _all_gather(arrs):
    n = len(arrs)

    def body(*refs):
        x_refs, out_refs = refs[:n], refs[n:2 * n]
        send_sems, recv_sems, local_sems = refs[2 * n:]
        x, y, c = _mesh_pos()
        me, sibling = (x, y, c), (x, y, 1 - c)
        chips = [(1 - x, y), (x, 1 - y), (1 - x, 1 - y)]

        def slot(a, px, py, pc):
            return out_refs[a].at[4 * px + 2 * py + pc]

        def copy(a, k, block, to, src=None):
            return pltpu.make_async_remote_copy(
                src_ref=slot(a, *block) if src is None else src, dst_ref=slot(a, *block),
                send_sem=send_sems.at[7 * a + k], recv_sem=recv_sems.at[7 * a + k], device_id=to, device_id_type=MESH)

        mine = [pltpu.make_async_copy(x_refs[a], slot(a, *me), local_sems.at[a]) for a in range(n)]
        for cp in mine:
            cp.start()
        first = []
        for j, chip in enumerate(chips):
            first += [copy(a, 1 + j, me, (*chip, c), src=x_refs[a]) for a in range(n)]
        first += [copy(a, 0, me, sibling, src=x_refs[a]) for a in range(n)]
        for cp in first:
            cp.start()
        passed = []
        for j, chip in enumerate(chips):
            for a in range(n):
                copy(a, 1 + j, (*chip, c), me).wait_recv()
                passed.append(copy(a, 4 + j, (*chip, c), sibling))
                passed[-1].start()
        for a in range(n):
            copy(a, 0, sibling, me).wait_recv()
        for j, chip in enumerate(chips):
            for a in range(n):
                copy(a, 4 + j, (*chip, 1 - c), me).wait_recv()
        for cp in first + passed:
            cp.wait_send()
        for cp in mine:
            cp.wait()

    hbm = pl.BlockSpec(memory_space=pl.ANY)
    return _pcall(body, name="all_gather",
                  out_shape=tuple(jax.ShapeDtypeStruct((N_DEV,) + a.shape, a.dtype) for a in arrs),
                  in_specs=[hbm] * n, out_specs=(hbm,) * n,
                  scratch_shapes=[pltpu.SemaphoreType.DMA((7 * n,)), pltpu.SemaphoreType.DMA((7 * n,)),
                                  pltpu.SemaphoreType.DMA((n,))],
                  compiler_params=pltpu.CompilerParams(has_side_effects=True))(*arrs)


def _peers():
    x, y, c = _mesh_pos()
    out = []
    for k in (1, 2, 4, 3, 5, 6, 7):
        px, py, pc = (x + (k >> 2)) % 2, (y + ((k >> 1) & 1)) % 2, (c + (k & 1)) % 2
        out.append((k, (px, py, pc), 4 * px + 2 * py + pc))
    return out, 4 * x + 2 * y + c


def _peer_copies(n, scatter, src_refs, land_refs, send_sems, recv_sems):
    peers, me = _peers()
    copies = []
    for k, peer, peer_id in peers:
        for a in range(n):
            copies.append(pltpu.make_async_remote_copy(
                src_ref=src_refs[a].at[peer_id] if scatter else src_refs[a],
                dst_ref=land_refs[a].at[k - 1] if scatter else land_refs[a].at[me],
                send_sem=send_sems.at[7 * a + k - 1], recv_sem=recv_sems.at[7 * a + k - 1],
                device_id=peer, device_id_type=MESH))
    return copies


_HBM = pl.BlockSpec(memory_space=pltpu.HBM)
_SEM = pl.BlockSpec(memory_space=pltpu.SEMAPHORE)
_EFFECT = pltpu.SideEffectType.DATAFLOW_SIDE_EFFECTING


def _in_hbm(a):
    return pltpu.with_memory_space_constraint(a, pltpu.HBM)


def _exchange_start(srcs, lands, scatter, name):
    n = len(srcs)

    def body(*refs):
        send_sems, recv_sems, token = refs[2 * n], refs[2 * n + 1], refs[-1]
        for cp in _peer_copies(n, scatter, refs[:n], refs[n:2 * n], send_sems, recv_sems):
            cp.start()
        token[...] = jnp.zeros_like(token)

    thru = [pltpu.HBM(a.shape, a.dtype) for a in list(srcs) + list(lands)]
    out = _pcall(body, name=name,
                 out_shape=(pltpu.SemaphoreType.DMA((7 * n,)), pltpu.SemaphoreType.DMA((7 * n,)), *thru,
                            jax.ShapeDtypeStruct((8, 128), F32)),
                 in_specs=[_HBM] * (2 * n), out_specs=(_SEM, _SEM, *([_HBM] * (2 * n)), pl.BlockSpec(memory_space=pltpu.VMEM)),
                 input_output_aliases={i: 2 + i for i in range(2 * n)},
                 compiler_params=pltpu.CompilerParams(has_side_effects=_EFFECT))(*[_in_hbm(a) for a in list(srcs) + list(lands)])
    return out[0], out[1], out[2:2 + n], out[2 + n:2 + 2 * n], out[-1]


def _exchange_wait(send_sems, recv_sems, srcs, lands, after, scatter, name):
    n = len(srcs)

    def body(*refs):
        for cp in _peer_copies(n, scatter, refs[:n], refs[n:2 * n], refs[2 * n], refs[2 * n + 1]):
            cp.wait_send()
            cp.wait_recv()

    thru = tuple(pltpu.HBM(a.shape, a.dtype) for a in list(srcs) + list(lands))
    out = _pcall(body, name=name, out_shape=thru,
                 in_specs=[_HBM] * (2 * n) + [_SEM, _SEM, pl.BlockSpec(memory_space=pl.ANY)], out_specs=(_HBM,) * (2 * n),
                 input_output_aliases={i: i for i in range(2 * n)},
                 compiler_params=pltpu.CompilerParams(has_side_effects=_EFFECT))(*srcs, *lands, send_sems, recv_sems, after)
    return out[:n], out[n:]


def _pack(arrays, dtype):
    lead = arrays[0].shape[:-1]
    flat = jnp.concatenate([a.astype(dtype) for a in arrays], axis=-1)
    n = flat.shape[-1]
    unit = 16 * COMM_LANES
    pad = (-n) % unit
    flat = jnp.pad(flat, [(0, 0)] * len(lead) + [(0, pad)])
    return flat.reshape(lead + ((n + pad) // COMM_LANES, COMM_LANES))


def _unpack(packed, shapes):
    lead = packed.shape[:-2]
    flat = packed.reshape(lead + (-1,))
    out, off = [], 0
    for shp in shapes:
        n = math.prod(shp)
        out.append(flat[..., off:off + n].reshape(lead + tuple(shp)))
        off += n
    return out


def _unshard(g, ax):
    shp = list(g.shape[1:])
    shp[ax] *= N_DEV
    return jnp.moveaxis(g, 0, ax).reshape(shp)


def _split(full, ax):
    shp = list(full.shape)
    r = full.reshape(shp[:ax] + [N_DEV, shp[ax] // N_DEV] + shp[ax + 1:])
    return jnp.moveaxis(r, ax, 0)


def _rope_tables(s, dk):
    half = dk // 2
    inv = ROPE_BASE ** (-jnp.arange(half, dtype=F32) / half)
    ang = jnp.arange(s, dtype=F32)[:, None] * inv[None, :]
    return jnp.cos(ang), jnp.sin(ang)


def _stage_big(s):
    i, ffn = divmod(s, 2)
    if ffn:
        return [('ffn_w_up', i), ('ffn_w_down', i)]
    kind, j = i % 3, i // 3
    mixer = [['gla_w_in', 'gla_w_out'], ['lru_w_in', 'lru_w_ga', 'lru_w_gx', 'lru_w_out'], ['ret_w_in', 'ret_w_out']][kind]
    return [(n, j) for n in mixer]


def _local_step(x, target, p, layer_weights, layer_grads, token):
    s, d = x.shape
    depth = p['norm_mix_w'].shape[0]
    gla_kd, gla_vd = d // 2, d
    ret_kd, ret_vd = d, 2 * d
    cos, sin = _rope_tables(s, ret_kd // RET_HEADS)
    row = lambda v: v.reshape(1, -1)
    saved = []
    cur, gb = {}, {}

    def view(name, layer=0, dev0=0, ndev=N_DEV, arr=None):
        return _View(cur[name] if arr is None else arr, 'col' if SHARD_AX[name] == 2 else 'row', 0, dev0, ndev)

    def dw_into(name, layer, a, b, label, dev0=0, ndev=N_DEV):
        gb[name] = _mm(a, b, 'tn', BF16, name=label, out=view(name, 0, dev0, ndev, gb[name]))

    def gla_weights(j):
        w_in = cur['gla_w_in']
        w_main, w_z = w_in[:, :2 * gla_kd + 2 * gla_vd], w_in[:, 2 * gla_kd + 2 * gla_vd:]
        w_z = jnp.pad(w_z, ((0, 0), (0, GATE_PAD - GLA_GATE_RANK)))
        w_gk = jnp.pad(p['gla_w_gk'][j], ((0, GATE_PAD - GLA_GATE_RANK), (0, 0)))
        return w_main, w_z, w_gk, row(p['gla_b_gk'][j]), row(p['gla_norm_w'][j]), view('gla_w_out', j)

    def lru_weights(j):
        return (view('lru_w_in', j), p['lru_conv_w'][j], row(p['lru_conv_b'][j]), cur['lru_w_ga'], row(p['lru_b_ga'][j]),
                cur['lru_w_gx'], row(p['lru_b_gx'][j]), row(p['lru_lambda'][j]), view('lru_w_out', j))

    for i in range(depth):
        kind, j = i % 3, i // 3
        cur = layer_weights(2 * i, x)
        h = _rms_fwd(x, row(p['norm_mix_w'][i]) + (token if i == 0 else 0.0))
        if kind == 0:
            w_main, w_z, w_gk, b_gk, nw, w_out = gla_weights(j)
            proj = _mm(h, w_main, 'nn', BF16, name="gla_in")
            z = _mm(h, w_z, 'nn', F32, name="gla_z")
            y, o_st, st = _chunk_fwd(True, proj, (z, w_gk, b_gk), nw, GLA_HEADS, gla_kd, gla_vd, 4)
            mix = (proj, z, o_st, st, y)
        elif kind == 1:
            w_in, cw, cb, wga, bga, wgx, bgx, lam, w_out = lru_weights(j)
            proj = _mm(h, w_in, 'nn', F32, name="lru_in")
            y, hs = _lru_fwd(proj, cw, cb, wga, bga, wgx, bgx, lam)
            mix = (proj, hs, y)
        else:
            nw, w_out = row(p['ret_norm_w'][j]), view('ret_w_out', j)
            proj = _mm(h, view('ret_w_in', j), 'nn', BF16, name="ret_in")
            y, o_st, st = _chunk_fwd(False, proj, (cos, sin), nw, RET_HEADS, ret_kd, ret_vd, 2)
            mix = (proj, o_st, st, y)
        x_mid = _mm(y, w_out, 'nn', F32, res=x, name="mix_out")
        cur.update(layer_weights(2 * i + 1, x_mid))
        h2 = _rms_fwd(x_mid, row(p['norm_ffn_w'][i]))
        act, gate, val, ug, uv = _ffn_up_act(h2, view('ffn_w_up', i), p['ffn_conv_w'][i], row(p['ffn_conv_b'][i]))
        x_out = _mm(act, view('ffn_w_down', i), 'nn', F32, res=x_mid, name="ffn_down")
        saved.append((x, h, mix, x_mid, h2, (ug, uv, gate, val), act, cur))
        x = x_out

    loss, dx, dxb, dw = _final_loss(x, row(p['norm_out_w']), target)
    g = {n: [None] * v.shape[0] for n, v in p.items() if n != 'norm_out_w'}
    g['norm_out_w'] = dw.reshape(-1)
    half = N_DEV // 2
    token = 0.0

    for i in reversed(range(depth)):
        kind, j = i % 3, i // 3
        x_in, h, mix, x_mid, h2, u, act, cur = saved[i]
        stage_names = [[n for n, _ in _stage_big(2 * i + part)] for part in (0, 1)]
        gb = {n: lax.empty(cur[n].shape, BF16) for n in stage_names[0] + stage_names[1] if n in GATHERED}
        dw_into('ffn_w_down', i, act, dxb, "ffn_down_dw")
        dact = _mm(dxb, view('ffn_w_down', i), 'nt', BF16, name="ffn_down_dx")
        dug, duv, dcw, dcb = _ffn_act_bwd(*u, p['ffn_conv_w'][i], dact)
        g['ffn_conv_w'][i], g['ffn_conv_b'][i] = dcw, dcb.reshape(-1)
        dw_into('ffn_w_up', i, h2, dug, "ffn_up_dw", 0, half)
        dw_into('ffn_w_up', i, h2, duv, "ffn_up_dw", half, half)
        dh2 = _mm(dug, view('ffn_w_up', i, 0, half), 'nt', F32, name="ffn_up_dx")
        dh2 = _mm(duv, view('ffn_w_up', i, half, half), 'nt', BF16, res=dh2, name="ffn_up_dx")
        dx, dxb, dnw = _rms_bwd(x_mid, row(p['norm_ffn_w'][i]) + token, dh2, dx)
        g['norm_ffn_w'][i] = dnw.reshape(-1)
        token = layer_grads(2 * i + 1, {n: gb[n] for n in stage_names[1]})
        if kind == 0:
            w_main, w_z, w_gk, b_gk, nw, w_out = gla_weights(j)
            proj, z, o_st, st, y = mix
            dw_into('gla_w_out', j, y, dxb, "mix_out_dw")
            dy = _mm(dxb, w_out, 'nt', BF16, name="mix_out_dx")
            dproj, dnw, dz, dwgk, dbgk = _chunk_bwd(True, proj, (z, w_gk, b_gk), nw, o_st, st, dy,
                                                    GLA_HEADS, gla_kd, gla_vd, 4)
            g['gla_norm_w'][j], g['gla_b_gk'][j] = dnw.reshape(-1), dbgk.reshape(-1)
            g['gla_w_gk'][j] = dwgk[:GLA_GATE_RANK]
            dw_main = _mm(h, dproj, 'tn', F32, name="gla_in_dw")
            dw_z = _mm(h, dz, 'tn', F32, name="gla_z_dw")
            dw_in = jnp.concatenate([dw_main, dw_z[:, :GLA_GATE_RANK]], axis=1)
            gb['gla_w_in'] = _split(dw_in, 1).astype(BF16)
            dh = _mm(dproj, w_main, 'nt', F32, name="gla_in_dx")
            dh = _mm(dz, w_z, 'nt', BF16, res=dh, name="gla_z_dx")
        elif kind == 1:
            w_in, cw, cb, wga, bga, wgx, bgx, lam, w_out = lru_weights(j)
            proj, hs, y = mix
            dw_into('lru_w_out', j, y, dxb, "mix_out_dw")
            dy = _mm(dxb, w_out, 'nt', BF16, name="mix_out_dx")
            dxb, dyb, dcw, dcb, dbga, dbgx, dlam, dwga, dwgx = _lru_bwd(proj, hs, dy, cw, cb, wga, bga, wgx, bgx, lam)
            g['lru_conv_w'][j], g['lru_conv_b'][j] = dcw, dcb.reshape(-1)
            g['lru_b_ga'][j], g['lru_b_gx'][j], g['lru_lambda'][j] = dbga.reshape(-1), dbgx.reshape(-1), dlam.reshape(-1)
            gb['lru_w_ga'], gb['lru_w_gx'] = _split(dwga, 1).astype(BF16), _split(dwgx, 1).astype(BF16)
            dproj = jnp.concatenate([dxb, dyb], axis=1)
            dw_into('lru_w_in', j, h, dproj, "lru_in_dw")
            dh = _mm(dproj, w_in, 'nt', BF16, name="lru_in_dx")
        else:
            nw, w_out = row(p['ret_norm_w'][j]), view('ret_w_out', j)
            proj, o_st, st, y = mix
            dw_into('ret_w_out', j, y, dxb, "mix_out_dw")
            dy = _mm(dxb, w_out, 'nt', BF16, name="mix_out_dx")
            dproj, dnw = _chunk_bwd(False, proj, (cos, sin), nw, o_st, st, dy, RET_HEADS, ret_kd, ret_vd, 2)
            g['ret_norm_w'][j] = dnw.reshape(-1)
            dw_into('ret_w_in', j, h, dproj, "ret_in_dw")
            dh = _mm(dproj, view('ret_w_in', j), 'nt', BF16, name="ret_in_dx")
        dx, dxb, dnw = _rms_bwd(x_in, row(p['norm_mix_w'][i]) + token, dh, dx)
        g['norm_mix_w'][i] = dnw.reshape(-1)
        token = layer_grads(2 * i, {n: gb[n] for n in stage_names[0]})

    grads = {n: (v if n == 'norm_out_w' else jnp.stack(v)) for n, v in g.items()}
    return loss[0, 0], dx, grads, token


def kernel(x, norm_mix_w, norm_ffn_w, norm_out_w, gla_w_in, gla_w_gk, gla_b_gk, gla_norm_w, gla_w_out, lru_w_in, lru_conv_w, lru_conv_b, lru_w_ga, lru_b_ga, lru_w_gx, lru_b_gx, lru_lambda, lru_w_out, ret_w_in, ret_norm_w, ret_w_out, ffn_w_up, ffn_conv_w, ffn_conv_b, ffn_w_down, loss_target, m_norm_mix_w, m_norm_ffn_w, m_norm_out_w, m_gla_w_in, m_gla_w_gk, m_gla_b_gk, m_gla_norm_w, m_gla_w_out, m_lru_w_in, m_lru_conv_w, m_lru_conv_b, m_lru_w_ga, m_lru_b_ga, m_lru_w_gx, m_lru_b_gx, m_lru_lambda, m_lru_w_out, m_ret_w_in, m_ret_norm_w, m_ret_w_out, m_ffn_w_up, m_ffn_conv_w, m_ffn_conv_b, m_ffn_w_down, v_norm_mix_w, v_norm_ffn_w, v_norm_out_w, v_gla_w_in, v_gla_w_gk, v_gla_b_gk, v_gla_norm_w, v_gla_w_out, v_lru_w_in, v_lru_conv_w, v_lru_conv_b, v_lru_w_ga, v_lru_b_ga, v_lru_w_gx, v_lru_b_gx, v_lru_lambda, v_lru_w_out, v_ret_w_in, v_ret_norm_w, v_ret_w_out, v_ffn_w_up, v_ffn_conv_w, v_ffn_conv_b, v_ffn_w_down):
    given = dict(locals())
    w = {n: given[n] for n in WEIGHTS}
    me_x, me_y, me_c = _mesh_pos()
    me = 4 * me_x + 2 * me_y + me_c

    n_stages = 2 * norm_mix_w.shape[0]

    def shards(s):
        return [w[n][j].astype(BF16) for n, j in _stage_big(s)]

    got0 = _all_gather(shards(0) + [_pack([w[n].reshape(-1) for n in SMALL_SHARDED], F32)])
    p = {n: w[n] for n in REPLICATED}
    for n, blk in zip(SMALL_SHARDED, _unpack(got0[-1], [w[n].shape for n in SMALL_SHARDED])):
        p[n] = _unshard(blk, SHARD_AX[n])
    gathers, token = {}, 0.0
    for s in range(1, n_stages):
        _, srcs = lax.optimization_barrier((got0[-1], shards(s)))
        lands =[lax.dynamic_update_index_in_dim(lax.empty((N_DEV,) + a.shape, BF16), a, me, 0) for a in srcs]
        *gathers[s], tok = _exchange_start(srcs, lands, False, "gather_start_%d" % s)
        token = token + tok[0, 0]

    def stage_weights(s, after):
        blocks = got0[:-1] if s == 0 else _exchange_wait(*gathers[s], after, False, "gather_wait_%d" % s)[1]
        out = {}
        for (n, _), blk in zip(_stage_big(s), blocks):
            out[n] = blk.reshape((N_DEV, 1) + blk.shape[1:]) if n in GATHERED else _unshard(blk, SHARD_AX[n] - 1)
        return out

    scatters = {}

    def stage_grads(s, gb):
        srcs = [gb[n].reshape((N_DEV,) + w[n].shape[1:]) for n, _ in _stage_big(s)]
        lands = [lax.empty((N_DEV - 1,) + a.shape[1:], BF16) for a in srcs]
        *scatters[s], tok = _exchange_start(srcs, lands, True, "scatter_start_%d" % s)
        return tok[0, 0]

    loss, grad_x, grads, token = _local_step(x[0], loss_target[0], p, stage_weights, stage_grads, token)
    loss = lax.psum(loss, ("x", "y", "c"))
    gw, delta, new_m, new_v = {}, {}, {}, {}

    after, big = grads['norm_out_w'] + token, {}
    for s in reversed(range(n_stages)):
        srcs, lands = _exchange_wait(*scatters[s], after, True, "scatter_wait_%d" % s)
        for (n, j), src, land in zip(_stage_big(s), srcs, lands):
            own = lax.dynamic_index_in_dim(src, me, 0, keepdims=False)
            big[n] = _adamw([own] + [(land, k) for k in range(N_DEV - 1)], w[n], given["m_" + n], given["v_" + n],
                            layer=j, prev=big.get(n))
            after = big[n][0]
    for n in BIG:
        gw[n], delta[n], new_m[n], new_v[n] = big[n]

    small = REPLICATED + SMALL_SHARDED
    _, packed = lax.optimization_barrier((after, _pack([grads[n].reshape(-1) for n in small], F32)))
    (parts,) = _all_gather([packed])
    summed = _unpack(_sum_list([parts[dev] for dev in range(N_DEV)], F32), [grads[n].shape for n in small])
    for n, gs in zip(small, summed):
        if SHARD_AX[n] is not None:
            gs = lax.dynamic_index_in_dim(_split(gs, SHARD_AX[n]), me, 0, keepdims=False)
        gw[n], delta[n], new_m[n], new_v[n] = _adamw([gs], w[n], given["m_" + n], given["v_" + n])

    return (loss, grad_x[None], *[gw[n] for n in WEIGHTS], *[delta[n] for n in WEIGHTS],
            *[new_m[n] for n in WEIGHTS], *[new_v[n] for n in WEIGHTS])
```

```python
import collections
import math

import jax
import jax.numpy as jnp
from jax import lax
from jax.experimental import pallas as pl
from jax.experimental.pallas import tpu as pltpu

F32 = jnp.float32
BF16 = jnp.bfloat16

N_DEV = 8
CHUNK = 64
RMS_EPS = 1e-6
GLA_HEADS = 4
GLA_GATE_RANK = 16
GLA_GATE_TAU = 16.0
GATE_PAD = 128
LRU_BLOCK_W = 256
LRU_C = 8.0
RET_HEADS = 8
ROPE_BASE = 10000.0
ADAM_LR, ADAM_B1, ADAM_B2, ADAM_EPS, ADAM_WD, ADAM_STEP = 0.001, 0.9, 0.999, 1e-08, 0.01, 10

HALO = 16
VMEM_LIMIT = 56 * 1024 * 1024
ROW_TILE = 256
COMM_LANES = 1024
MM_TM, MM_TN, MM_TK = 1024, 1024, 2048

MESH = pl.DeviceIdType.MESH

WEIGHTS = ['norm_mix_w', 'norm_ffn_w', 'norm_out_w', 'gla_w_in', 'gla_w_gk', 'gla_b_gk', 'gla_norm_w',
           'gla_w_out', 'lru_w_in', 'lru_conv_w', 'lru_conv_b', 'lru_w_ga', 'lru_b_ga', 'lru_w_gx',
           'lru_b_gx', 'lru_lambda', 'lru_w_out', 'ret_w_in', 'ret_norm_w', 'ret_w_out', 'ffn_w_up',
           'ffn_conv_w', 'ffn_conv_b', 'ffn_w_down']
SHARD_AX = {'norm_mix_w': None, 'norm_ffn_w': None, 'norm_out_w': None, 'gla_w_in': 2, 'gla_w_gk': 2,
            'gla_b_gk': 1, 'gla_norm_w': 1, 'gla_w_out': 1, 'lru_w_in': 2, 'lru_conv_w': 2,
            'lru_conv_b': None, 'lru_w_ga': 2, 'lru_b_ga': None, 'lru_w_gx': 2, 'lru_b_gx': None,
            'lru_lambda': None, 'lru_w_out': 1, 'ret_w_in': 2, 'ret_norm_w': 1, 'ret_w_out': 1,
            'ffn_w_up': 2, 'ffn_conv_w': 2, 'ffn_conv_b': None, 'ffn_w_down': 1}
BIG = ['gla_w_in', 'gla_w_out', 'lru_w_in', 'lru_w_ga', 'lru_w_gx', 'lru_w_out', 'ret_w_in', 'ret_w_out',
       'ffn_w_up', 'ffn_w_down']
GATHERED = ['gla_w_out', 'lru_w_in', 'lru_w_out', 'ret_w_in', 'ret_w_out', 'ffn_w_up', 'ffn_w_down']
SMALL_SHARDED = ['gla_w_gk', 'gla_b_gk', 'gla_norm_w', 'lru_conv_w', 'ret_norm_w', 'ffn_conv_w']
REPLICATED = [n for n in WEIGHTS if SHARD_AX[n] is None]


def _pcall(body, **kw):
    return pl.pallas_call(body, **kw)


def _params(sem=None, **kw):
    return pltpu.CompilerParams(dimension_semantics=sem, vmem_limit_bytes=VMEM_LIMIT, **kw)


def _tile(n, pref, align=128):
    if n <= pref:
        return n
    t = (pref // align) * align
    while t >= align:
        if n % t == 0:
            return t
        t -= align
    return n


def _row_iota(shape):
    return lax.broadcasted_iota(jnp.int32, shape, 0)


def _shift_down(x, halo, s):
    t, c = x.shape
    r = pltpu.roll(x.reshape(t // 8, 8, c), s, 1)
    prev = jnp.concatenate([pltpu.roll(halo, s, 0)[None], r[:-1]], axis=0)
    sub = lax.broadcasted_iota(jnp.int32, r.shape, 1)
    return jnp.where(sub < s, prev, r).reshape(t, c)


def _shift_up(x, nxt, s):
    t, c = x.shape
    r = pltpu.roll(x.reshape(t // 8, 8, c), 8 - s, 1)
    follow = jnp.concatenate([r[1:], pltpu.roll(nxt, 8 - s, 0)[None]], axis=0)
    sub = lax.broadcasted_iota(jnp.int32, r.shape, 1)
    return jnp.where(sub >= 8 - s, follow, r).reshape(t, c)


def _cumsum_rows(x):
    t, row, s = x.shape[0], _row_iota(x.shape), 1
    while s < t:
        x = x + jnp.where(row >= s, pltpu.roll(x, s, 0), 0.0)
        s *= 2
    return x


def _rev_cumsum_rows(x):
    t, row, s = x.shape[0], _row_iota(x.shape), 1
    while s < t:
        x = x + jnp.where(row < t - s, pltpu.roll(x, t - s, 0), 0.0)
        s *= 2
    return x


def _scan_fwd(a, u, h0):
    t, row, s = a.shape[0], _row_iota(a.shape), 1
    while s < t:
        keep = row >= s
        u = u + a * jnp.where(keep, pltpu.roll(u, s, 0), 0.0)
        a = a * jnp.where(keep, pltpu.roll(a, s, 0), 1.0)
        s *= 2
    return u + a * h0


def _scan_rev(c, g, d_end):
    t, row, s = c.shape[0], _row_iota(c.shape), 1
    while s < t:
        keep = row < t - s
        g = g + c * jnp.where(keep, pltpu.roll(g, t - s, 0), 0.0)
        c = c * jnp.where(keep, pltpu.roll(c, t - s, 0), 1.0)
        s *= 2
    return g + c * d_end


def _pick_row(x, r):
    return jnp.sum(jnp.where(_row_iota(x.shape) == r, x, 0.0), axis=0, keepdims=True)


def _sigmoid(x):
    return 1.0 / (1.0 + jnp.exp(-x))


def _softplus(x):
    return jnp.maximum(x, 0.0) + jnp.log(1.0 + jnp.exp(-jnp.abs(x)))


_GELU_C = math.sqrt(2.0 / math.pi)


def _gelu_and_grad(x):
    x2 = x * x
    th = jnp.tanh(_GELU_C * (x + 0.044715 * x * x2))
    g = 0.5 * x * (1.0 + th)
    dg = 0.5 * (1.0 + th) + 0.5 * x * (1.0 - th * th) * _GELU_C * (1.0 + 3.0 * 0.044715 * x2)
    return g, dg


def _neg_expm1(y):
    small = -(y * (1.0 + y * (0.5 + y * (1.0 / 6.0 + y * (1.0 / 24.0)))))
    return jnp.where(y > -0.01, small, 1.0 - jnp.exp(y))


def _dot(a, b, dims):
    return lax.dot_general(a.astype(BF16), b.astype(BF16), (dims, ((), ())), preferred_element_type=F32)


def _dot_nn(a, b):
    return _dot(a, b, ((1,), (0,)))


def _dot_nt(a, b):
    return _dot(a, b, ((1,), (1,)))


def _dot_tn(a, b):
    return _dot(a, b, ((0,), (0,)))


_View = collections.namedtuple("_View", "arr kind layer dev0 ndev")


def _view_shape(v):
    r, c = v.arr.shape[2:]
    return (r, v.ndev * c) if v.kind == 'col' else (N_DEV * r, c)


def _view_spec(v, tr, tc, rc_of):
    r, c = v.arr.shape[2:]
    if v.kind == 'col':
        per = c // tc

        def imap(*g):
            ri, ci = rc_of(*g)
            return (v.dev0 + ci // per, v.layer, ri, ci % per)
    elif tr > r:
        def imap_blocks(*g):
            ri, ci = rc_of(*g)
            return (ri, v.layer, 0, ci)
        return pl.BlockSpec((tr // r, None, r, tc), imap_blocks)
    else:
        per = r // tr

        def imap(*g):
            ri, ci = rc_of(*g)
            return (ri // per, v.layer, ri % per, ci)
    return pl.BlockSpec((None, None, tr, tc), imap)


def _row_tile(v, pref):
    r = v.arr.shape[2]
    if r >= pref:
        return _tile(r, pref)
    q = max(q for q in (1, 2, 4, 8) if r * q <= pref)
    return r * q


def _rows2d(val):
    return val.reshape(-1, val.shape[-1]) if val.ndim == 3 else val


def _mm(a, b, mode, out_dtype=F32, res=None, name="mm", out=None):
    bshape = _view_shape(b) if isinstance(b, _View) else b.shape
    if mode == 'nn':
        (m, k), n = a.shape, bshape[1]
    elif mode == 'nt':
        (m, k), n = a.shape, bshape[0]
    else:
        (k, m), n = a.shape, bshape[1]
    tm, tn, tk = _tile(m, MM_TM), _tile(n, MM_TN), _tile(k, MM_TK)
    if isinstance(b, _View):
        if b.kind == 'col' and mode == 'nt':
            tk = _tile(b.arr.shape[3], MM_TK)
        elif b.kind == 'col':
            tn = _tile(b.arr.shape[3], MM_TN)
        elif mode == 'nt':
            tn = _row_tile(b, MM_TN)
        else:
            tk = _row_tile(b, MM_TK)
    if out is not None:
        if out.kind == 'col':
            tn = _tile(out.arr.shape[3], MM_TN)
        else:
            tm = _row_tile(out, MM_TM)
    nk = k // tk
    a_spec = pl.BlockSpec((tk, tm), lambda i, j, kk: (kk, i)) if mode == 'tn' else pl.BlockSpec((tm, tk), lambda i, j, kk: (i, kk))
    if isinstance(b, _View):
        b_spec = (_view_spec(b, tn, tk, lambda i, j, kk: (j, kk)) if mode == 'nt'
                  else _view_spec(b, tk, tn, lambda i, j, kk: (kk, j)))
    else:
        b_spec = pl.BlockSpec((tn, tk), lambda i, j, kk: (j, kk)) if mode == 'nt' else pl.BlockSpec((tk, tn), lambda i, j, kk: (kk, j))
    r_spec = pl.BlockSpec((tm, tn), lambda i, j, kk: (i, j))
    o_spec = r_spec if out is None else _view_spec(out, tm, tn, lambda i, j, kk: (i, j))
    dot = {'nn': _dot_nn, 'nt': _dot_nt, 'tn': _dot_tn}[mode]

    def body(*refs):
        refs = list(refs)
        if out is not None:
            del refs[2 + (res is not None)]
        a_ref, b_ref = refs[:2]
        r_ref = None if res is None else refs[2]
        o_ref = refs[2 if res is None else 3]

        def finish(total):
            if res is not None:
                total = total + r_ref[...].astype(F32)
            o_ref[...] = total.astype(out_dtype).reshape(o_ref.shape)

        part = dot(a_ref[...], _rows2d(b_ref[...]))
        if nk == 1:
            finish(part)
            return
        acc = refs[-1]
        kk = pl.program_id(2)

        @pl.when(kk == 0)
        def _():
            acc[...] = part

        @pl.when(kk > 0)
        def _():
            acc[...] += part

        @pl.when(kk == nk - 1)
        def _():
            finish(acc[...])

    args, specs = [a, b.arr if isinstance(b, _View) else b], [a_spec, b_spec]
    if res is not None:
        args.append(res)
        specs.append(r_spec)
    aliases, out_shape = {}, jax.ShapeDtypeStruct((m, n), out_dtype)
    if out is not None:
        aliases, out_shape = {len(args): 0}, jax.ShapeDtypeStruct(out.arr.shape, out.arr.dtype)
        args.append(out.arr)
        specs.append(pl.BlockSpec(memory_space=pl.ANY))
    return _pcall(body, name=name, out_shape=out_shape,
                  grid=(m // tm, n // tn, nk), in_specs=specs, out_specs=o_spec,
                  scratch_shapes=[] if nk == 1 else [pltpu.VMEM((tm, tn), F32)], input_output_aliases=aliases,
                  compiler_params=_params(("parallel", "parallel", "arbitrary")))(*args)


def _rms_fwd(x, w):
    s, d = x.shape
    tr = _tile(s, ROW_TILE, 16)

    def body(x_ref, w_ref, o_ref):
        xv = x_ref[...]
        r = lax.rsqrt(jnp.mean(xv * xv, axis=-1, keepdims=True) + RMS_EPS)
        o_ref[...] = (xv * r * w_ref[...]).astype(BF16)

    return _pcall(body, name="rms_fwd", out_shape=jax.ShapeDtypeStruct((s, d), BF16), grid=(s // tr,),
                  in_specs=[pl.BlockSpec((tr, d), lambda i: (i, 0)), pl.BlockSpec((1, d), lambda i: (0, 0))],
                  out_specs=pl.BlockSpec((tr, d), lambda i: (i, 0)), compiler_params=_params(("parallel",)))(x, w)


def _rms_bwd(x, w, dh, dres):
    s, d = x.shape
    tr = _tile(s, ROW_TILE, 16)

    def body(x_ref, w_ref, dh_ref, dr_ref, dx_ref, dxb_ref, dw_ref):
        i = pl.program_id(0)
        xv = x_ref[...]
        r = lax.rsqrt(jnp.mean(xv * xv, axis=-1, keepdims=True) + RMS_EPS)
        xh = xv * r
        dhv = dh_ref[...].astype(F32)
        dxh = dhv * w_ref[...]
        dxv = dr_ref[...] + r * (dxh - xh * jnp.mean(dxh * xh, axis=-1, keepdims=True))
        dx_ref[...] = dxv
        dxb_ref[...] = dxv.astype(BF16)
        part = jnp.sum(dhv * xh, axis=0, keepdims=True)

        @pl.when(i == 0)
        def _():
            dw_ref[...] = part

        @pl.when(i > 0)
        def _():
            dw_ref[...] += part

    row = pl.BlockSpec((tr, d), lambda i: (i, 0))
    vec = pl.BlockSpec((1, d), lambda i: (0, 0))
    return _pcall(body, name="rms_bwd",
                  out_shape=(jax.ShapeDtypeStruct((s, d), F32), jax.ShapeDtypeStruct((s, d), BF16),
                             jax.ShapeDtypeStruct((1, d), F32)),
                  grid=(s // tr,), in_specs=[row, vec, row, row], out_specs=(row, row, vec),
                  compiler_params=_params(("arbitrary",)))(x, w, dh, dres)


def _final_loss(x, w, target):
    s, d = x.shape
    tr = _tile(s, ROW_TILE, 16)

    def body(x_ref, w_ref, t_ref, l_ref, dx_ref, dxb_ref, dw_ref):
        i = pl.program_id(0)
        xv = x_ref[...]
        r = lax.rsqrt(jnp.mean(xv * xv, axis=-1, keepdims=True) + RMS_EPS)
        xh = xv * r
        err = xh * w_ref[...] - t_ref[...]
        lpart = 0.5 * jnp.sum(jnp.mean(err * err, axis=-1, keepdims=True), axis=0, keepdims=True)
        dy = err * (1.0 / d)
        dxh = dy * w_ref[...]
        dxv = r * (dxh - xh * jnp.mean(dxh * xh, axis=-1, keepdims=True))
        dx_ref[...] = dxv
        dxb_ref[...] = dxv.astype(BF16)
        part = jnp.sum(dy * xh, axis=0, keepdims=True)

        @pl.when(i == 0)
        def _():
            dw_ref[...] = part
            l_ref[...] = jnp.broadcast_to(lpart, l_ref.shape)

        @pl.when(i > 0)
        def _():
            dw_ref[...] += part
            l_ref[...] += jnp.broadcast_to(lpart, l_ref.shape)

    row = pl.BlockSpec((tr, d), lambda i: (i, 0))
    vec = pl.BlockSpec((1, d), lambda i: (0, 0))
    return _pcall(body, name="final_loss",
                  out_shape=(jax.ShapeDtypeStruct((8, 128), F32), jax.ShapeDtypeStruct((s, d), F32),
                             jax.ShapeDtypeStruct((s, d), BF16), jax.ShapeDtypeStruct((1, d), F32)),
                  grid=(s // tr,), in_specs=[row, vec, row],
                  out_specs=(pl.BlockSpec((8, 128), lambda i: (0, 0)), row, row, vec),
                  compiler_params=_params(("arbitrary",)))(x, w, target)


def _ffn_up_act(h, w, cw, cb):
    s, k = h.shape
    c = w.arr.shape[3]
    half = w.ndev // 2
    f = half * c
    tm, tn = _tile(s, 512, 16), _tile(c, MM_TN)
    per, nj = c // tn, f // tn

    def body(h_ref, hh_ref, wg_ref, wv_ref, cwg_ref, cwv_ref, cbg_ref, cbv_ref,
             a_ref, gate_ref, val_ref, ug_ref, uv_ref):
        i = pl.program_id(1)
        hm, hh = h_ref[...], hh_ref[...]

        def conv_half(w_ref, cw_ref, cb_ref, u_ref):
            wt = w_ref[...]
            u = _dot_nn(hm, wt)
            hal = jnp.where(i > 0, _dot_nn(hh, wt)[8:16], 0.0)
            u_ref[...] = u.astype(BF16)
            return (cw_ref[2:3, :] * u + cw_ref[1:2, :] * _shift_down(u, hal, 1)
                    + cw_ref[0:1, :] * _shift_down(u, hal, 2) + cb_ref[...])

        gate = conv_half(wg_ref, cwg_ref, cbg_ref, ug_ref)
        val = conv_half(wv_ref, cwv_ref, cbv_ref, uv_ref)
        a_ref[...] = (_gelu_and_grad(gate)[0] * val).astype(BF16)
        gate_ref[...] = gate.astype(BF16)
        val_ref[...] = val.astype(BF16)

    def wspec(dev0):
        return pl.BlockSpec((None, None, k, tn), lambda j, i: (w.dev0 + dev0 + j // per, w.layer, 0, j % per))

    def vec(rows, off):
        return pl.BlockSpec((rows, tn), lambda j, i: (0, j + off))

    out = pl.BlockSpec((tm, tn), lambda j, i: (i, j))
    return _pcall(body, name="ffn_up_act", out_shape=(jax.ShapeDtypeStruct((s, f), BF16),) * 5, grid=(nj, s // tm),
                  in_specs=[pl.BlockSpec((tm, k), lambda j, i: (i, 0)),
                            pl.BlockSpec((HALO, k), lambda j, i: (jnp.maximum(i * (tm // HALO) - 1, 0), 0)),
                            wspec(0), wspec(half), vec(3, 0), vec(3, nj), vec(1, 0), vec(1, nj)],
                  out_specs=(out,) * 5,
                  compiler_params=_params(("parallel", "parallel")))(h, h, w.arr, w.arr, cw, cw, cb, cb)


def _ffn_down_dx_act(dy, w, ug, uv, gate, val, cw):
    s, k = dy.shape
    f = ug.shape[1]
    r = w.arr.shape[2]
    tm, tn = _tile(s, 512, 16), _tile(r, MM_TN)
    per, nj, ni = r // tn, f // tn, s // tm

    def body(dy_ref, dyn_ref, w_ref, ug_ref, uv_ref, gate_ref, val_ref, gaten_ref, valn_ref, wg_ref, wv_ref,
             dug_ref, duv_ref, dwg_ref, dwv_ref, dbg_ref, dbv_ref):
        i = pl.program_id(1)
        wt = w_ref[...]

        def d_gate_val(da, g_ref, v_ref):
            gl, dgl = _gelu_and_grad(g_ref[...].astype(F32))
            return da * v_ref[...].astype(F32) * dgl, da * gl

        dgate, dval = d_gate_val(_dot_nt(dy_ref[...], wt), gate_ref, val_ref)
        dgate_n, dval_n = d_gate_val(_dot_nt(dyn_ref[...], wt), gaten_ref, valn_ref)

        def back(d, d_next, x_ref, cw_ref, du_ref, dw_ref, db_ref):
            nxt = jnp.where(i < ni - 1, d_next[0:8], 0.0)
            ds = [_shift_up(d, nxt, 2), _shift_up(d, nxt, 1), d]
            du_ref[...] = (cw_ref[2:3, :] * ds[2] + cw_ref[1:2, :] * ds[1] + cw_ref[0:1, :] * ds[0]).astype(BF16)
            x = x_ref[...].astype(F32)
            parts = [jnp.sum(ds[t] * x, axis=0, keepdims=True) for t in range(3)]
            bpart = jnp.sum(d, axis=0, keepdims=True)

            @pl.when(i == 0)
            def _():
                for t in range(3):
                    dw_ref[t:t + 1, :] = parts[t]
                db_ref[...] = bpart

            @pl.when(i > 0)
            def _():
                for t in range(3):
                    dw_ref[t:t + 1, :] += parts[t]
                db_ref[...] += bpart

        back(dgate, dgate_n, ug_ref, wg_ref, dug_ref, dwg_ref, dbg_ref)
        back(dval, dval_n, uv_ref, wv_ref, duv_ref, dwv_ref, dbv_ref)

    def next_rows(cols, cmap):
        return pl.BlockSpec((HALO, cols), lambda j, i: (jnp.minimum((i + 1) * (tm // HALO), s // HALO - 1), cmap(j)))

    main = pl.BlockSpec((tm, tn), lambda j, i: (i, j))

    def vec(rows, off):
        return pl.BlockSpec((rows, tn), lambda j, i: (0, j + off))

    dug, duv, dwg, dwv, dbg, dbv = _pcall(
        body, name="ffn_down_dx_act",
        out_shape=(jax.ShapeDtypeStruct((s, f), BF16), jax.ShapeDtypeStruct((s, f), BF16),
                   jax.ShapeDtypeStruct((3, f), F32), jax.ShapeDtypeStruct((3, f), F32),
                   jax.ShapeDtypeStruct((1, f), F32), jax.ShapeDtypeStruct((1, f), F32)),
        grid=(nj, ni),
        in_specs=[pl.BlockSpec((tm, k), lambda j, i: (i, 0)), next_rows(k, lambda j: 0),
                  pl.BlockSpec((None, None, tn, k), lambda j, i: (j // per, w.layer, j % per, 0)),
                  main, main, main, main, next_rows(tn, lambda j: j), next_rows(tn, lambda j: j),
                  vec(3, 0), vec(3, nj)],
        out_specs=(main, main, vec(3, 0), vec(3, 0), vec(1, 0), vec(1, 0)),
        compiler_params=_params(("parallel", "arbitrary")))(dy, dy, w.arr, ug, uv, gate, val, gate, val, cw, cw)
    return dug, duv, jnp.concatenate([dwg, dwv], axis=1), jnp.concatenate([dbg, dbv], axis=1)


def _chunk_cols(h, kd, vd, heads):
    dk, dv = kd // heads, vd // heads
    return (slice(h * dk, (h + 1) * dk), slice(kd + h * dk, kd + (h + 1) * dk),
            slice(2 * kd + h * dv, 2 * kd + (h + 1) * dv), slice(2 * kd + vd + h * dv, 2 * kd + vd + (h + 1) * dv))


def _rope(x, cos, sin):
    half = x.shape[1] // 2
    x1, x2 = x[:, :half], x[:, half:]
    return jnp.concatenate([x1 * cos - x2 * sin, x2 * cos + x1 * sin], axis=1)


def _unrope(d, cos, sin):
    half = d.shape[1] // 2
    d1, d2 = d[:, :half], d[:, half:]
    return jnp.concatenate([d1 * cos + d2 * sin, d2 * cos - d1 * sin], axis=1)


def _chunk_inputs(gla, h, heads, kd, vd, rows, proj_ref, aux):
    qc, kc, vc, gc = _chunk_cols(h, kd, vd, heads)
    dk = kd // heads
    q = proj_ref[rows, qc].astype(F32)
    k = proj_ref[rows, kc].astype(F32)
    v = proj_ref[rows, vc]
    g = proj_ref[rows, gc].astype(F32)
    c = {}
    if gla:
        z_ref, wgk_ref, bgk_ref = aux
        c['z'] = z_ref[rows, :]
        c['gk'] = _dot_nn(c['z'], wgk_ref[:, qc]) + bgk_ref[:, qc]
        la = (jnp.minimum(c['gk'], 0.0) - jnp.log(1.0 + jnp.exp(-jnp.abs(c['gk'])))) * (1.0 / GLA_GATE_TAU)
        b = _cumsum_rows(la)
        bl = jnp.sum(la, axis=0, keepdims=True)
        q = q * (dk ** -0.5)
    else:
        cos_ref, sin_ref = aux
        c['cos'], c['sin'] = cos_ref[rows, :], sin_ref[rows, :]
        q = _rope(q, c['cos'], c['sin'])
        k = _rope(k, c['cos'], c['sin']) * (dk ** -0.5)
        lg = math.log(1.0 - 2.0 ** (-5.0 - h))
        b = lg * (_row_iota((CHUNK, 1)).astype(F32) + 1.0)
        bl = jnp.full((1, 1), lg * CHUNK, F32)
    eb, enb = jnp.exp(b), jnp.exp(-b)
    c.update(q=q, k=k, v=v, g=g, b=b, bl=bl, eb=eb, enb=enb, ebl=jnp.exp(bl),
             qd=q * eb, kg=k * enb, qg=q * enb, kd=k * eb, ks=k * jnp.exp(bl - b))
    lower = _row_iota((CHUNK, CHUNK)) >= lax.broadcasted_iota(jnp.int32, (CHUNK, CHUNK), 1)
    c['lower'] = lower
    c['A'] = jnp.where(lower, _dot_nt(c['qd'], c['kg']), _dot_nt(c['qg'], c['kd']))
    return c


def _head_norm(gla, o):
    if not gla:
        o = o - jnp.mean(o, axis=-1, keepdims=True)
    r = lax.rsqrt(jnp.mean(o * o, axis=-1, keepdims=True) + RMS_EPS)
    return o * r, r


def _chunk_fwd(gla, proj, aux_arrays, nw, heads, kd, vd, cps):
    s, pw = proj.shape
    dk, dv = kd // heads, vd // heads
    rt = CHUNK * cps
    nb = s // rt
    n_aux = len(aux_arrays)

    def body(*refs):
        proj_ref, aux, nw_ref = refs[0], refs[1:1 + n_aux], refs[1 + n_aux]
        y_ref, o_ref, st_ref, state = refs[2 + n_aux:]

        @pl.when(pl.program_id(0) == 0)
        def _():
            state[...] = jnp.zeros_like(state)

        def chunk(ci, carry):
            rows = pl.ds(pl.multiple_of(ci * CHUNK, CHUNK), CHUNK)
            for h in range(heads):
                c = _chunk_inputs(gla, h, heads, kd, vd, rows, proj_ref, aux)
                vcols = slice(h * dv, (h + 1) * dv)
                st0 = state[h]
                st_ref[ci, h] = st0.astype(BF16)
                o = _dot_nn(c['A'], c['v']) + _dot_nt(c['qd'], st0)
                state[h] = st0 * c['ebl'] + _dot_tn(c['v'], c['ks'])
                oh, _ = _head_norm(gla, o)
                gv = c['g']
                y_ref[rows, vcols] = (oh * nw_ref[:, vcols] * (gv * _sigmoid(gv))).astype(BF16)
                o_ref[rows, vcols] = o.astype(BF16)
            return carry

        lax.fori_loop(0, cps, chunk, 0)

    row = lambda w: pl.BlockSpec((rt, w), lambda n: (n, 0))
    full = lambda a: pl.BlockSpec(a.shape, lambda n: (0,) * a.ndim)
    aux_specs = [row(a.shape[1]) if a.shape[0] == s else full(a) for a in aux_arrays]
    return _pcall(
        body, name="gla_fwd" if gla else "ret_fwd",
        out_shape=(jax.ShapeDtypeStruct((s, vd), BF16), jax.ShapeDtypeStruct((s, vd), BF16),
                   jax.ShapeDtypeStruct((s // CHUNK, heads, dv, dk), BF16)),
        grid=(nb,), in_specs=[row(pw)] + aux_specs + [full(nw)],
        out_specs=(row(vd), row(vd), pl.BlockSpec((cps, heads, dv, dk), lambda n: (n, 0, 0, 0))),
        scratch_shapes=[pltpu.VMEM((heads, dv, dk), F32)],
        compiler_params=_params(("arbitrary",)))(proj, *aux_arrays, nw)


def _chunk_bwd(gla, proj, aux_arrays, nw, o_st, st, dy, heads, kd, vd, cps):
    s, pw = proj.shape
    dk, dv = kd // heads, vd // heads
    rt = CHUNK * cps
    nb = s // rt
    n_aux = len(aux_arrays)

    def body(*refs):
        proj_ref, aux, nw_ref = refs[0], refs[1:1 + n_aux], refs[1 + n_aux]
        o_ref, st_ref, dy_ref = refs[2 + n_aux:5 + n_aux]
        outs = refs[5 + n_aux:]
        dp_ref, dnw_ref = outs[0], outs[1]
        if gla:
            dz_ref, dwgk_ref, dbgk_ref, dstate = outs[2:]
        else:
            dstate = outs[2]

        @pl.when(pl.program_id(0) == 0)
        def _():
            dstate[...] = jnp.zeros_like(dstate)
            dnw_ref[...] = jnp.zeros_like(dnw_ref)
            if gla:
                dwgk_ref[...] = jnp.zeros_like(dwgk_ref)
                dbgk_ref[...] = jnp.zeros_like(dbgk_ref)

        def chunk(i, carry):
            ci = cps - 1 - i
            rows = pl.ds(pl.multiple_of(ci * CHUNK, CHUNK), CHUNK)
            dz = jnp.zeros((CHUNK, GATE_PAD), F32)
            for h in range(heads):
                c = _chunk_inputs(gla, h, heads, kd, vd, rows, proj_ref, aux)
                qc, kc, vc, gc = _chunk_cols(h, kd, vd, heads)
                vcols = slice(h * dv, (h + 1) * dv)
                o = o_ref[rows, vcols].astype(F32)
                oh, r = _head_norm(gla, o)
                dyv = dy_ref[rows, vcols].astype(F32)
                gv = c['g']
                sg = _sigmoid(gv)
                nwv = nw_ref[:, vcols]
                dp_ref[rows, gc] = (dyv * oh * nwv * (sg * (1.0 + gv * (1.0 - sg)))).astype(BF16)
                dn = dyv * (gv * sg)
                dnw_ref[:, vcols] += jnp.sum(dn * oh, axis=0, keepdims=True)
                doh = dn * nwv
                do = doh - oh * jnp.mean(doh * oh, axis=-1, keepdims=True)
                if not gla:
                    do = do - jnp.mean(doh, axis=-1, keepdims=True)
                do = r * do
                st0 = st_ref[ci, h]
                dst1 = dstate[h]
                v = c['v']
                da = _dot_nt(do, v)
                dal = jnp.where(c['lower'], da, 0.0)
                dau = da - dal
                dp_ref[rows, vc] = (_dot_tn(c['A'], do) + _dot_nt(c['ks'], dst1)).astype(BF16)
                dqd = _dot_nn(dal, c['kg']) + _dot_nn(do, st0)
                dkg = _dot_tn(dal, c['qd'])
                dqg = _dot_nn(dau, c['kd'])
                dkd = _dot_tn(dau, c['qg'])
                dks = _dot_nn(v, dst1)
                dstate[h] = _dot_tn(do, c['qd']) + dst1 * c['ebl']
                dq = dqd * c['eb'] + dqg * c['enb']
                dkk = dkg * c['enb'] + dkd * c['eb'] + dks * jnp.exp(c['bl'] - c['b'])
                if gla:
                    db = dqd * c['qd'] - dkg * c['kg'] - dqg * c['qg'] + dkd * c['kd'] - dks * c['ks']
                    dbl = (jnp.sum(dks * c['ks'], axis=0, keepdims=True)
                           + c['ebl'] * jnp.sum(dst1 * st0.astype(F32), axis=0, keepdims=True))
                    db = db + jnp.where(_row_iota(db.shape) == CHUNK - 1, dbl, 0.0)
                    dgk = _rev_cumsum_rows(db) * (1.0 / GLA_GATE_TAU) / (1.0 + jnp.exp(c['gk']))
                    _, wgk_ref, _ = aux
                    dz = dz + _dot_nt(dgk, wgk_ref[:, qc])
                    dwgk_ref[:, qc] += _dot_tn(c['z'], dgk)
                    dbgk_ref[:, qc] += jnp.sum(dgk, axis=0, keepdims=True)
                    dp_ref[rows, qc] = (dq * (dk ** -0.5)).astype(BF16)
                    dp_ref[rows, kc] = dkk.astype(BF16)
                else:
                    dp_ref[rows, qc] = _unrope(dq, c['cos'], c['sin']).astype(BF16)
                    dp_ref[rows, kc] = (_unrope(dkk, c['cos'], c['sin']) * (dk ** -0.5)).astype(BF16)
            if gla:
                dz_ref[rows, :] = dz
            return carry

        lax.fori_loop(0, cps, chunk, 0)

    row = lambda w: pl.BlockSpec((rt, w), lambda n: (nb - 1 - n, 0))
    full = lambda a: pl.BlockSpec(a.shape, lambda n: (0,) * a.ndim)
    aux_specs = [row(a.shape[1]) if a.shape[0] == s else full(a) for a in aux_arrays]
    out_shape = [jax.ShapeDtypeStruct((s, pw), BF16), jax.ShapeDtypeStruct((1, vd), F32)]
    out_specs = [row(pw), full(nw)]
    if gla:
        wgk, bgk = aux_arrays[1], aux_arrays[2]
        out_shape += [jax.ShapeDtypeStruct((s, GATE_PAD), F32), jax.ShapeDtypeStruct(wgk.shape, F32),
                      jax.ShapeDtypeStruct(bgk.shape, F32)]
        out_specs += [row(GATE_PAD), full(wgk), full(bgk)]
    return _pcall(
        body, name="gla_bwd" if gla else "ret_bwd", out_shape=tuple(out_shape), grid=(nb,),
        in_specs=[row(pw)] + aux_specs + [full(nw), row(vd),
                                          pl.BlockSpec((cps, heads, dv, dk), lambda n: (nb - 1 - n, 0, 0, 0)), row(vd)],
        out_specs=tuple(out_specs), scratch_shapes=[pltpu.VMEM((heads, dv, dk), F32)],
        compiler_params=_params(("arbitrary",)))(proj, *aux_arrays, nw, o_st, st, dy)


def _lru_gates(xc, wga_ref, bga_ref, wgx_ref, bgx_ref, lam_ref):
    r = _sigmoid(_dot_nn(xc, wga_ref[0]) + bga_ref[...])
    i = _sigmoid(_dot_nn(xc, wgx_ref[0]) + bgx_ref[...])
    sp = _softplus(-lam_ref[...])
    la = -LRU_C * r * sp
    return r, i, sp, la, jnp.exp(la), jnp.sqrt(_neg_expm1(2.0 * la))


def _lru_specs(w, nbk, tt, tmap):
    main = lambda off: pl.BlockSpec((tt, LRU_BLOCK_W), lambda n, t: (tmap(t), n + off))
    halo = pl.BlockSpec((HALO, LRU_BLOCK_W), lambda n, t: (jnp.maximum(tmap(t) * (tt // HALO) - 1, 0), n))
    vec = lambda rows: pl.BlockSpec((rows, LRU_BLOCK_W), lambda n, t: (0, n))
    mat = pl.BlockSpec((1, LRU_BLOCK_W, LRU_BLOCK_W), lambda n, t: (n, 0, 0))
    return main, halo, vec, mat


def _lru_fwd(proj, cw, cb, wga, bga, wgx, bgx, lam):
    s, w2 = proj.shape
    w = w2 // 2
    nbk, tt = w // LRU_BLOCK_W, _tile(s, ROW_TILE, 16)
    main, halo, vec, mat = _lru_specs(w, nbk, tt, lambda t: t)

    def body(x_ref, xh_ref, y_ref, cw_ref, cb_ref, wga_ref, bga_ref, wgx_ref, bgx_ref, lam_ref,
             out_ref, hs_ref, hcar):
        t = pl.program_id(1)
        x = x_ref[...]
        hal = jnp.where(t > 0, xh_ref[8:16, :], 0.0)
        xc = (cw_ref[3:4, :] * x + cw_ref[2:3, :] * _shift_down(x, hal, 1) + cw_ref[1:2, :] * _shift_down(x, hal, 2)
              + cw_ref[0:1, :] * _shift_down(x, hal, 3) + cb_ref[...])
        r, i, sp, la, a, mlt = _lru_gates(xc, wga_ref, bga_ref, wgx_ref, bgx_ref, lam_ref)

        @pl.when(t == 0)
        def _():
            hcar[...] = jnp.zeros_like(hcar)

        h = _scan_fwd(a, xc * i * mlt, hcar[...])
        hcar[...] = _pick_row(h, tt - 1)
        hs_ref[...] = h
        out_ref[...] = (h * _gelu_and_grad(y_ref[...])[0]).astype(BF16)

    return _pcall(body, name="lru_fwd",
                  out_shape=(jax.ShapeDtypeStruct((s, w), BF16), jax.ShapeDtypeStruct((s, w), F32)),
                  grid=(nbk, s // tt),
                  in_specs=[main(0), halo, main(nbk), vec(4), vec(1), mat, vec(1), mat, vec(1), vec(1)],
                  out_specs=(main(0), main(0)), scratch_shapes=[pltpu.VMEM((1, LRU_BLOCK_W), F32)],
                  compiler_params=_params(("parallel", "arbitrary")))(proj, proj, proj, cw, cb, wga, bga, wgx, bgx, lam)


def _lru_bwd(proj, hs, dout, cw, cb, wga, bga, wgx, bgx, lam):
    s, w2 = proj.shape
    w = w2 // 2
    nbk, tt = w // LRU_BLOCK_W, _tile(s, ROW_TILE, 16)
    nt = s // tt
    main, halo, vec, mat = _lru_specs(w, nbk, tt, lambda t: nt - 1 - t)

    def body(x_ref, xh_ref, y_ref, hs_ref, hh_ref, do_ref, cw_ref, cb_ref, wga_ref, bga_ref, wgx_ref, bgx_ref,
             lam_ref, dx_ref, dy_ref, dcw_ref, dcb_ref, dbga_ref, dbgx_ref, dlam_ref, dwga_ref, dwgx_ref,
             dhcar, dxcar):
        t = pl.program_id(1)
        first_tile = t == nt - 1

        @pl.when(t == 0)
        def _():
            dhcar[...] = jnp.zeros_like(dhcar)
            dxcar[...] = jnp.zeros_like(dxcar)

        x = x_ref[...]
        hal = jnp.where(first_tile, 0.0, xh_ref[8:16, :])
        xs = [x, _shift_down(x, hal, 1), _shift_down(x, hal, 2), _shift_down(x, hal, 3)]
        xc = cw_ref[3:4, :] * xs[0] + cw_ref[2:3, :] * xs[1] + cw_ref[1:2, :] * xs[2] + cw_ref[0:1, :] * xs[3] + cb_ref[...]
        r, i, sp, la, a, mlt = _lru_gates(xc, wga_ref, bga_ref, wgx_ref, bgx_ref, lam_ref)
        h = hs_ref[...]
        hprev = _shift_down(h, jnp.where(first_tile, 0.0, hh_ref[8:16, :]), 1)
        gl, dgl = _gelu_and_grad(y_ref[...])
        dov = do_ref[...].astype(F32)
        dy_ref[...] = (dov * h * dgl).astype(BF16)
        row = _row_iota(a.shape)
        coef = jnp.where(row == tt - 1, 1.0, pltpu.roll(a, tt - 1, 0))
        dh = _scan_rev(coef, dov * gl, dhcar[...])
        dhcar[...] = _pick_row(a * dh, 0)
        dxc = dh * i * mlt
        di = dh * xc * mlt
        dm = dh * xc * i
        dla = dh * hprev * a - dm * jnp.exp(2.0 * la) / mlt
        dpa = dla * (-LRU_C * sp) * r * (1.0 - r)
        dpx = di * i * (1.0 - i)
        dxc = dxc + _dot_nt(dpa, wga_ref[0]) + _dot_nt(dpx, wgx_ref[0])
        nxt = dxcar[...]
        dx_ref[...] = (cw_ref[3:4, :] * dxc + cw_ref[2:3, :] * _shift_up(dxc, nxt, 1)
                       + cw_ref[1:2, :] * _shift_up(dxc, nxt, 2) + cw_ref[0:1, :] * _shift_up(dxc, nxt, 3)).astype(BF16)
        dxcar[...] = dxc[0:8]
        colsum = lambda v: jnp.sum(v, axis=0, keepdims=True)
        parts = [(dcb_ref, colsum(dxc)), (dbga_ref, colsum(dpa)), (dbgx_ref, colsum(dpx)),
                 (dlam_ref, colsum(dla * LRU_C * r) * _sigmoid(-lam_ref[...]))]
        wparts = [colsum(dxc * xs[3 - k]) for k in range(4)]
        dwa, dwx = _dot_tn(xc, dpa), _dot_tn(xc, dpx)

        @pl.when(t == 0)
        def _():
            for ref, val in parts:
                ref[...] = val
            for k in range(4):
                dcw_ref[k:k + 1, :] = wparts[k]
            dwga_ref[0] = dwa
            dwgx_ref[0] = dwx

        @pl.when(t > 0)
        def _():
            for ref, val in parts:
                ref[...] += val
            for k in range(4):
                dcw_ref[k:k + 1, :] += wparts[k]
            dwga_ref[0] += dwa
            dwgx_ref[0] += dwx

    sd = jax.ShapeDtypeStruct
    return _pcall(
        body, name="lru_bwd",
        out_shape=(sd((s, w), BF16), sd((s, w), BF16), sd((4, w), F32), sd((1, w), F32), sd((1, w), F32),
                   sd((1, w), F32), sd((1, w), F32), sd(wga.shape, F32), sd(wgx.shape, F32)),
        grid=(nbk, nt),
        in_specs=[main(0), halo, main(nbk), main(0), halo, main(0), vec(4), vec(1), mat, vec(1), mat, vec(1), vec(1)],
        out_specs=(main(0), main(0), vec(4), vec(1), vec(1), vec(1), vec(1), mat, mat),
        scratch_shapes=[pltpu.VMEM((1, LRU_BLOCK_W), F32), pltpu.VMEM((8, LRU_BLOCK_W), F32)],
        compiler_params=_params(("parallel", "arbitrary")))(proj, proj, proj, hs, hs, dout, cw, cb, wga, bga, wgx, bgx, lam)


def _adamw(parts, w, m, v, layer=None, prev=None):
    shape = w.shape
    cols = shape[-1]
    layers = 1 if layer is None else shape[0]
    w3, m3, v3 = (a.reshape(layers, -1, cols) for a in (w, m, v))
    rows = w3.shape[1]
    tr = _tile(rows, max(16, (1 << 17) // cols // 16 * 16), 16)
    blk = pl.BlockSpec((None, tr, cols), lambda i: (layer or 0, i, 0))
    p_args, p_specs = [], []
    for part in parts:
        if isinstance(part, tuple):
            stack, idx = part
            p_args.append(stack.reshape(stack.shape[0], rows, cols))
            p_specs.append(pl.BlockSpec((None, tr, cols), lambda i, idx=idx: (idx, i, 0)))
        else:
            p_args.append(part.reshape(rows, cols))
            p_specs.append(pl.BlockSpec((tr, cols), lambda i: (i, 0)))
    n_parts = len(parts)
    n_prev = 0 if prev is None else 4

    def body(*refs):
        w_ref, m_ref, v_ref = refs[n_parts:n_parts + 3]
        g_ref, d_ref, nm_ref, nv_ref = refs[n_parts + 3 + n_prev:]
        gv = refs[0][...].astype(F32)
        for p_ref in refs[1:n_parts]:
            gv = gv + p_ref[...].astype(F32)
        g_ref[...] = gv
        nm = ADAM_B1 * m_ref[...] + (1.0 - ADAM_B1) * gv
        nv = ADAM_B2 * v_ref[...] + (1.0 - ADAM_B2) * (gv * gv)
        m_hat = nm / (1.0 - ADAM_B1 ** ADAM_STEP)
        v_hat = nv / (1.0 - ADAM_B2 ** ADAM_STEP)
        d_ref[...] = -ADAM_LR * (m_hat / (jnp.sqrt(v_hat) + ADAM_EPS) + ADAM_WD * w_ref[...])
        nm_ref[...] = nm
        nv_ref[...] = nv

    prev_args = [] if prev is None else [a.reshape(layers, rows, cols) for a in prev]
    n_in = n_parts + 3
    out = _pcall(body, name="adamw", out_shape=tuple(jax.ShapeDtypeStruct((layers, rows, cols), F32) for _ in range(4)),
                 grid=(rows // tr,), in_specs=p_specs + [blk] * 3 + [pl.BlockSpec(memory_space=pl.ANY)] * n_prev,
                 out_specs=(blk,) * 4, input_output_aliases={n_in + q: q for q in range(n_prev)},
                 compiler_params=_params(("parallel",)))(*p_args, w3, m3, v3, *prev_args)
    return tuple(o.reshape(shape) for o in out)


def _sum_list(arrs, out_dtype):
    shape = arrs[0].shape
    cols = shape[-1]
    flat = [a.reshape(-1, cols) for a in arrs]
    rows = flat[0].shape[0]
    tr = _tile(rows, max(16, (1 << 18) // cols // 16 * 16), 16)

    def body(*refs):
        acc = refs[0][...].astype(F32)
        for p_ref in refs[1:-1]:
            acc = acc + p_ref[...].astype(F32)
        refs[-1][...] = acc.astype(out_dtype)

    blk = pl.BlockSpec((tr, cols), lambda i: (i, 0))
    out = _pcall(body, name="sum_list", out_shape=jax.ShapeDtypeStruct((rows, cols), out_dtype), grid=(rows // tr,),
                 in_specs=[blk] * len(flat), out_specs=blk, compiler_params=_params(("parallel",)))(*flat)
    return out.reshape(shape)


def _mesh_pos():
    return lax.axis_index("x"), lax.axis_index("y"), lax.axis_index("c")


def _all_gather(arrs):
    n = len(arrs)

    def body(*refs):
        x_refs, out_refs = refs[:n], refs[n:2 * n]
        send_sems, recv_sems, local_sems = refs[2 * n:]
        x, y, c = _mesh_pos()
        me, sibling = (x, y, c), (x, y, 1 - c)
        chips = [(1 - x, y), (x, 1 - y), (1 - x, 1 - y)]

        def slot(a, px, py, pc):
            return out_refs[a].at[4 * px + 2 * py + pc]

        def copy(a, k, block, to, src=None):
            return pltpu.make_async_remote_copy(
                src_ref=slot(a, *block) if src is None else src, dst_ref=slot(a, *block),
                send_sem=send_sems.at[7 * a + k], recv_sem=recv_sems.at[7 * a + k], device_id=to, device_id_type=MESH)

        mine = [pltpu.make_async_copy(x_refs[a], slot(a, *me), local_sems.at[a]) for a in range(n)]
        for cp in mine:
            cp.start()
        first = []
        for j, chip in enumerate(chips):
            first += [copy(a, 1 + j, me, (*chip, c), src=x_refs[a]) for a in range(n)]
        first += [copy(a, 0, me, sibling, src=x_refs[a]) for a in range(n)]
        for cp in first:
            cp.start()
        passed = []
        for j, chip in enumerate(chips):
            for a in range(n):
                copy(a, 1 + j, (*chip, c), me).wait_recv()
                passed.append(copy(a, 4 + j, (*chip, c), sibling))
                passed[-1].start()
        for a in range(n):
            copy(a, 0, sibling, me).wait_recv()
        for j, chip in enumerate(chips):
            for a in range(n):
                copy(a, 4 + j, (*chip, 1 - c), me).wait_recv()
        for cp in first + passed:
            cp.wait_send()
        for cp in mine:
            cp.wait()

    hbm = pl.BlockSpec(memory_space=pl.ANY)
    return _pcall(body, name="all_gather",
                  out_shape=tuple(jax.ShapeDtypeStruct((N_DEV,) + a.shape, a.dtype) for a in arrs),
                  in_specs=[hbm] * n, out_specs=(hbm,) * n,
                  scratch_shapes=[pltpu.SemaphoreType.DMA((7 * n,)), pltpu.SemaphoreType.DMA((7 * n,)),
                                  pltpu.SemaphoreType.DMA((n,))],
                  compiler_params=pltpu.CompilerParams(has_side_effects=True))(*arrs)


def _peers():
    x, y, c = _mesh_pos()
    out = []
    for k in (1, 2, 4, 3, 5, 6, 7):
        px, py, pc = (x + (k >> 2)) % 2, (y + ((k >> 1) & 1)) % 2, (c + (k & 1)) % 2
        out.append((k, (px, py, pc), 4 * px + 2 * py + pc))
    return out, 4 * x + 2 * y + c


def _peer_copies(n, scatter, src_refs, land_refs, send_sems, recv_sems):
    peers, me = _peers()
    copies = []
    for k, peer, peer_id in peers:
        for a in range(n):
            copies.append(pltpu.make_async_remote_copy(
                src_ref=src_refs[a].at[peer_id] if scatter else src_refs[a],
                dst_ref=land_refs[a].at[k - 1] if scatter else land_refs[a].at[me],
                send_sem=send_sems.at[7 * a + k - 1], recv_sem=recv_sems.at[7 * a + k - 1],
                device_id=peer, device_id_type=MESH))
    return copies


_HBM = pl.BlockSpec(memory_space=pltpu.HBM)
_SEM = pl.BlockSpec(memory_space=pltpu.SEMAPHORE)
_EFFECT = pltpu.SideEffectType.DATAFLOW_SIDE_EFFECTING


def _in_hbm(a):
    return pltpu.with_memory_space_constraint(a, pltpu.HBM)


def _exchange_start(srcs, lands, scatter, name):
    n = len(srcs)

    def body(*refs):
        send_sems, recv_sems, token = refs[2 * n], refs[2 * n + 1], refs[-1]
        for cp in _peer_copies(n, scatter, refs[:n], refs[n:2 * n], send_sems, recv_sems):
            cp.start()
        token[...] = jnp.zeros_like(token)

    thru = [pltpu.HBM(a.shape, a.dtype) for a in list(srcs) + list(lands)]
    out = _pcall(body, name=name,
                 out_shape=(pltpu.SemaphoreType.DMA((7 * n,)), pltpu.SemaphoreType.DMA((7 * n,)), *thru,
                            jax.ShapeDtypeStruct((8, 128), F32)),
                 in_specs=[_HBM] * (2 * n), out_specs=(_SEM, _SEM, *([_HBM] * (2 * n)), pl.BlockSpec(memory_space=pltpu.VMEM)),
                 input_output_aliases={i: 2 + i for i in range(2 * n)},
                 compiler_params=pltpu.CompilerParams(has_side_effects=_EFFECT))(*[_in_hbm(a) for a in list(srcs) + list(lands)])
    return out[0], out[1], out[2:2 + n], out[2 + n:2 + 2 * n], out[-1]


def _exchange_wait(send_sems, recv_sems, srcs, lands, after, scatter, name):
    n = len(srcs)

    def body(*refs):
        for cp in _peer_copies(n, scatter, refs[:n], refs[n:2 * n], refs[2 * n], refs[2 * n + 1]):
            cp.wait_send()
            cp.wait_recv()

    thru = tuple(pltpu.HBM(a.shape, a.dtype) for a in list(srcs) + list(lands))
    out = _pcall(body, name=name, out_shape=thru,
                 in_specs=[_HBM] * (2 * n) + [_SEM, _SEM, pl.BlockSpec(memory_space=pl.ANY)], out_specs=(_HBM,) * (2 * n),
                 input_output_aliases={i: i for i in range(2 * n)},
                 compiler_params=pltpu.CompilerParams(has_side_effects=_EFFECT))(*srcs, *lands, send_sems, recv_sems, after)
    return out[:n], out[n:]


def _pack(arrays, dtype):
    lead = arrays[0].shape[:-1]
    flat = jnp.concatenate([a.astype(dtype) for a in arrays], axis=-1)
    n = flat.shape[-1]
    unit = 16 * COMM_LANES
    pad = (-n) % unit
    flat = jnp.pad(flat, [(0, 0)] * len(lead) + [(0, pad)])
    return flat.reshape(lead + ((n + pad) // COMM_LANES, COMM_LANES))


def _unpack(packed, shapes):
    lead = packed.shape[:-2]
    flat = packed.reshape(lead + (-1,))
    out, off = [], 0
    for shp in shapes:
        n = math.prod(shp)
        out.append(flat[..., off:off + n].reshape(lead + tuple(shp)))
        off += n
    return out


def _unshard(g, ax):
    shp = list(g.shape[1:])
    shp[ax] *= N_DEV
    return jnp.moveaxis(g, 0, ax).reshape(shp)


def _split(full, ax):
    shp = list(full.shape)
    r = full.reshape(shp[:ax] + [N_DEV, shp[ax] // N_DEV] + shp[ax + 1:])
    return jnp.moveaxis(r, ax, 0)


def _rope_tables(s, dk):
    half = dk // 2
    inv = ROPE_BASE ** (-jnp.arange(half, dtype=F32) / half)
    ang = jnp.arange(s, dtype=F32)[:, None] * inv[None, :]
    return jnp.cos(ang), jnp.sin(ang)


def _stage_big(s):
    i, ffn = divmod(s, 2)
    if ffn:
        return [('ffn_w_up', i), ('ffn_w_down', i)]
    kind, j = i % 3, i // 3
    mixer = [['gla_w_in', 'gla_w_out'], ['lru_w_in', 'lru_w_ga', 'lru_w_gx', 'lru_w_out'], ['ret_w_in', 'ret_w_out']][kind]
    return [(n, j) for n in mixer]


def _local_step(x, target, p, layer_weights, layer_grads, token):
    s, d = x.shape
    depth = p['norm_mix_w'].shape[0]
    gla_kd, gla_vd = d // 2, d
    ret_kd, ret_vd = d, 2 * d
    cos, sin = _rope_tables(s, ret_kd // RET_HEADS)
    row = lambda v: v.reshape(1, -1)
    saved = []
    cur, gb = {}, {}

    def view(name, layer=0, dev0=0, ndev=N_DEV, arr=None):
        return _View(cur[name] if arr is None else arr, 'col' if SHARD_AX[name] == 2 else 'row', 0, dev0, ndev)

    def dw_into(name, layer, a, b, label, dev0=0, ndev=N_DEV):
        gb[name] = _mm(a, b, 'tn', BF16, name=label, out=view(name, 0, dev0, ndev, gb[name]))

    def gla_weights(j):
        w_in = cur['gla_w_in']
        w_main, w_z = w_in[:, :2 * gla_kd + 2 * gla_vd], w_in[:, 2 * gla_kd + 2 * gla_vd:]
        w_z = jnp.pad(w_z, ((0, 0), (0, GATE_PAD - GLA_GATE_RANK)))
        w_gk = jnp.pad(p['gla_w_gk'][j], ((0, GATE_PAD - GLA_GATE_RANK), (0, 0)))
        return w_main, w_z, w_gk, row(p['gla_b_gk'][j]), row(p['gla_norm_w'][j]), view('gla_w_out', j)

    def lru_weights(j):
        return (view('lru_w_in', j), p['lru_conv_w'][j], row(p['lru_conv_b'][j]), cur['lru_w_ga'], row(p['lru_b_ga'][j]),
                cur['lru_w_gx'], row(p['lru_b_gx'][j]), row(p['lru_lambda'][j]), view('lru_w_out', j))

    for i in range(depth):
        kind, j = i % 3, i // 3
        cur = layer_weights(2 * i, x)
        h = _rms_fwd(x, row(p['norm_mix_w'][i]) + (token if i == 0 else 0.0))
        if kind == 0:
            w_main, w_z, w_gk, b_gk, nw, w_out = gla_weights(j)
            proj = _mm(h, w_main, 'nn', BF16, name="gla_in")
            z = _mm(h, w_z, 'nn', F32, name="gla_z")
            y, o_st, st = _chunk_fwd(True, proj, (z, w_gk, b_gk), nw, GLA_HEADS, gla_kd, gla_vd, 4)
            mix = (proj, z, o_st, st, y)
        elif kind == 1:
            w_in, cw, cb, wga, bga, wgx, bgx, lam, w_out = lru_weights(j)
            proj = _mm(h, w_in, 'nn', F32, name="lru_in")
            y, hs = _lru_fwd(proj, cw, cb, wga, bga, wgx, bgx, lam)
            mix = (proj, hs, y)
        else:
            nw, w_out = row(p['ret_norm_w'][j]), view('ret_w_out', j)
            proj = _mm(h, view('ret_w_in', j), 'nn', BF16, name="ret_in")
            y, o_st, st = _chunk_fwd(False, proj, (cos, sin), nw, RET_HEADS, ret_kd, ret_vd, 2)
            mix = (proj, o_st, st, y)
        x_mid = _mm(y, w_out, 'nn', F32, res=x, name="mix_out")
        cur.update(layer_weights(2 * i + 1, x_mid))
        h2 = _rms_fwd(x_mid, row(p['norm_ffn_w'][i]))
        act, gate, val, ug, uv = _ffn_up_act(h2, view('ffn_w_up', i), p['ffn_conv_w'][i], row(p['ffn_conv_b'][i]))
        x_out = _mm(act, view('ffn_w_down', i), 'nn', F32, res=x_mid, name="ffn_down")
        saved.append((x, h, mix, x_mid, h2, (ug, uv, gate, val), act, cur))
        x = x_out

    loss, dx, dxb, dw = _final_loss(x, row(p['norm_out_w']), target)
    g = {n: [None] * v.shape[0] for n, v in p.items() if n != 'norm_out_w'}
    g['norm_out_w'] = dw.reshape(-1)
    half = N_DEV // 2
    token = 0.0

    for i in reversed(range(depth)):
        kind, j = i % 3, i // 3
        x_in, h, mix, x_mid, h2, u, act, cur = saved[i]
        stage_names = [[n for n, _ in _stage_big(2 * i + part)] for part in (0, 1)]
        gb = {n: lax.empty(cur[n].shape, BF16) for n in stage_names[0] + stage_names[1] if n in GATHERED}
        dw_into('ffn_w_down', i, act, dxb, "ffn_down_dw")
        dug, duv, dcw, dcb = _ffn_down_dx_act(dxb, view('ffn_w_down', i), *u, p['ffn_conv_w'][i])
        g['ffn_conv_w'][i], g['ffn_conv_b'][i] = dcw, dcb.reshape(-1)
        dw_into('ffn_w_up', i, h2, dug, "ffn_up_dw", 0, half)
        dw_into('ffn_w_up', i, h2, duv, "ffn_up_dw", half, half)
        dh2 = _mm(dug, view('ffn_w_up', i, 0, half), 'nt', F32, name="ffn_up_dx")
        dh2 = _mm(duv, view('ffn_w_up', i, half, half), 'nt', BF16, res=dh2, name="ffn_up_dx")
        dx, dxb, dnw = _rms_bwd(x_mid, row(p['norm_ffn_w'][i]) + token, dh2, dx)
        g['norm_ffn_w'][i] = dnw.reshape(-1)
        token = layer_grads(2 * i + 1, {n: gb[n] for n in stage_names[1]})
        if kind == 0:
            w_main, w_z, w_gk, b_gk, nw, w_out = gla_weights(j)
            proj, z, o_st, st, y = mix
            dw_into('gla_w_out', j, y, dxb, "mix_out_dw")
            dy = _mm(dxb, w_out, 'nt', BF16, name="mix_out_dx")
            dproj, dnw, dz, dwgk, dbgk = _chunk_bwd(True, proj, (z, w_gk, b_gk), nw, o_st, st, dy,
                                                    GLA_HEADS, gla_kd, gla_vd, 4)
            g['gla_norm_w'][j], g['gla_b_gk'][j] = dnw.reshape(-1), dbgk.reshape(-1)
            g['gla_w_gk'][j] = dwgk[:GLA_GATE_RANK]
            dw_main = _mm(h, dproj, 'tn', F32, name="gla_in_dw")
            dw_z = _mm(h, dz, 'tn', F32, name="gla_z_dw")
            dw_in = jnp.concatenate([dw_main, dw_z[:, :GLA_GATE_RANK]], axis=1)
            gb['gla_w_in'] = _split(dw_in, 1).astype(BF16)
            dh = _mm(dproj, w_main, 'nt', F32, name="gla_in_dx")
            dh = _mm(dz, w_z, 'nt', BF16, res=dh, name="gla_z_dx")
        elif kind == 1:
            w_in, cw, cb, wga, bga, wgx, bgx, lam, w_out = lru_weights(j)
            proj, hs, y = mix
            dw_into('lru_w_out', j, y, dxb, "mix_out_dw")
            dy = _mm(dxb, w_out, 'nt', BF16, name="mix_out_dx")
            dxb, dyb, dcw, dcb, dbga, dbgx, dlam, dwga, dwgx = _lru_bwd(proj, hs, dy, cw, cb, wga, bga, wgx, bgx, lam)
            g['lru_conv_w'][j], g['lru_conv_b'][j] = dcw, dcb.reshape(-1)
            g['lru_b_ga'][j], g['lru_b_gx'][j], g['lru_lambda'][j] = dbga.reshape(-1), dbgx.reshape(-1), dlam.reshape(-1)
            gb['lru_w_ga'], gb['lru_w_gx'] = _split(dwga, 1).astype(BF16), _split(dwgx, 1).astype(BF16)
            dproj = jnp.concatenate([dxb, dyb], axis=1)
            dw_into('lru_w_in', j, h, dproj, "lru_in_dw")
            dh = _mm(dproj, w_in, 'nt', BF16, name="lru_in_dx")
        else:
            nw, w_out = row(p['ret_norm_w'][j]), view('ret_w_out', j)
            proj, o_st, st, y = mix
            dw_into('ret_w_out', j, y, dxb, "mix_out_dw")
            dy = _mm(dxb, w_out, 'nt', BF16, name="mix_out_dx")
            dproj, dnw = _chunk_bwd(False, proj, (cos, sin), nw, o_st, st, dy, RET_HEADS, ret_kd, ret_vd, 2)
            g['ret_norm_w'][j] = dnw.reshape(-1)
            dw_into('ret_w_in', j, h, dproj, "ret_in_dw")
            dh = _mm(dproj, view('ret_w_in', j), 'nt', BF16, name="ret_in_dx")
        dx, dxb, dnw = _rms_bwd(x_in, row(p['norm_mix_w'][i]) + token, dh, dx)
        g['norm_mix_w'][i] = dnw.reshape(-1)
        token = layer_grads(2 * i, {n: gb[n] for n in stage_names[0]})

    grads = {n: (v if n == 'norm_out_w' else jnp.stack(v)) for n, v in g.items()}
    return loss[0, 0], dx, grads, token


def kernel(x, norm_mix_w, norm_ffn_w, norm_out_w, gla_w_in, gla_w_gk, gla_b_gk, gla_norm_w, gla_w_out, lru_w_in, lru_conv_w, lru_conv_b, lru_w_ga, lru_b_ga, lru_w_gx, lru_b_gx, lru_lambda, lru_w_out, ret_w_in, ret_norm_w, ret_w_out, ffn_w_up, ffn_conv_w, ffn_conv_b, ffn_w_down, loss_target, m_norm_mix_w, m_norm_ffn_w, m_norm_out_w, m_gla_w_in, m_gla_w_gk, m_gla_b_gk, m_gla_norm_w, m_gla_w_out, m_lru_w_in, m_lru_conv_w, m_lru_conv_b, m_lru_w_ga, m_lru_b_ga, m_lru_w_gx, m_lru_b_gx, m_lru_lambda, m_lru_w_out, m_ret_w_in, m_ret_norm_w, m_ret_w_out, m_ffn_w_up, m_ffn_conv_w, m_ffn_conv_b, m_ffn_w_down, v_norm_mix_w, v_norm_ffn_w, v_norm_out_w, v_gla_w_in, v_gla_w_gk, v_gla_b_gk, v_gla_norm_w, v_gla_w_out, v_lru_w_in, v_lru_conv_w, v_lru_conv_b, v_lru_w_ga, v_lru_b_ga, v_lru_w_gx, v_lru_b_gx, v_lru_lambda, v_lru_w_out, v_ret_w_in, v_ret_norm_w, v_ret_w_out, v_ffn_w_up, v_ffn_conv_w, v_ffn_conv_b, v_ffn_w_down):
    given = dict(locals())
    w = {n: given[n] for n in WEIGHTS}
    me_x, me_y, me_c = _mesh_pos()
    me = 4 * me_x + 2 * me_y + me_c

    n_stages = 2 * norm_mix_w.shape[0]

    def shards(s):
        return [w[n][j].astype(BF16) for n, j in _stage_big(s)]

    got0 = _all_gather(shards(0) + [_pack([w[n].reshape(-1) for n in SMALL_SHARDED], F32)])
    p = {n: w[n] for n in REPLICATED}
    for n, blk in zip(SMALL_SHARDED, _unpack(got0[-1], [w[n].shape for n in SMALL_SHARDED])):
        p[n] = _unshard(blk, SHARD_AX[n])
    gathers, token = {}, 0.0
    for s in range(1, n_stages):
        _, srcs = lax.optimization_barrier((got0[-1], shards(s)))
        lands =[lax.dynamic_update_index_in_dim(lax.empty((N_DEV,) + a.shape, BF16), a, me, 0) for a in srcs]
        *gathers[s], tok = _exchange_start(srcs, lands, False, "gather_start_%d" % s)
        token = token + tok[0, 0]

    def stage_weights(s, after):
        blocks = got0[:-1] if s == 0 else _exchange_wait(*gathers[s], after, False, "gather_wait_%d" % s)[1]
        out = {}
        for (n, _), blk in zip(_stage_big(s), blocks):
            out[n] = blk.reshape((N_DEV, 1) + blk.shape[1:]) if n in GATHERED else _unshard(blk, SHARD_AX[n] - 1)
        return out

    scatters = {}

    def stage_grads(s, gb):
        srcs = [gb[n].reshape((N_DEV,) + w[n].shape[1:]) for n, _ in _stage_big(s)]
        lands = [lax.empty((N_DEV - 1,) + a.shape[1:], BF16) for a in srcs]
        *scatters[s], tok = _exchange_start(srcs, lands, True, "scatter_start_%d" % s)
        return tok[0, 0]

    loss, grad_x, grads, token = _local_step(x[0], loss_target[0], p, stage_weights, stage_grads, token)
    loss = lax.psum(loss, ("x", "y", "c"))
    gw, delta, new_m, new_v = {}, {}, {}, {}

    after, big = grads['norm_out_w'] + token, {}
    for s in reversed(range(n_stages)):
        srcs, lands = _exchange_wait(*scatters[s], after, True, "scatter_wait_%d" % s)
        for (n, j), src, land in zip(_stage_big(s), srcs, lands):
            own = lax.dynamic_index_in_dim(src, me, 0, keepdims=False)
            big[n] = _adamw([own] + [(land, k) for k in range(N_DEV - 1)], w[n], given["m_" + n], given["v_" + n],
                            layer=j, prev=big.get(n))
            after = big[n][0]
    for n in BIG:
        gw[n], delta[n], new_m[n], new_v[n] = big[n]

    small = REPLICATED + SMALL_SHARDED
    _, packed = lax.optimization_barrier((after, _pack([grads[n].reshape(-1) for n in small], F32)))
    (parts,) = _all_gather([packed])
    summed = _unpack(_sum_list([parts[dev] for dev in range(N_DEV)], F32), [grads[n].shape for n in small])
    for n, gs in zip(small, summed):
        if SHARD_AX[n] is not None:
            gs = lax.dynamic_index_in_dim(_split(gs, SHARD_AX[n]), me, 0, keepdims=False)
        gw[n], delta[n], new_m[n], new_v[n] = _adamw([gs], w[n], given["m_" + n], given["v_" + n])

    return (loss, grad_x[None], *[gw[n] for n in WEIGHTS], *[delta[n] for n in WEIGHTS],
            *[new_m[n] for n in WEIGHTS], *[new_v[n] for n in WEIGHTS])
```

```python
import collections
import math

import jax
import jax.numpy as jnp
from jax import lax
from jax.experimental import pallas as pl
from jax.experimental.pallas import tpu as pltpu

F32 = jnp.float32
BF16 = jnp.bfloat16

N_DEV = 8
CHUNK = 64
RMS_EPS = 1e-6
GLA_HEADS = 4
GLA_GATE_RANK = 16
GLA_GATE_TAU = 16.0
GATE_PAD = 128
LRU_BLOCK_W = 256
LRU_C = 8.0
RET_HEADS = 8
ROPE_BASE = 10000.0
ADAM_LR, ADAM_B1, ADAM_B2, ADAM_EPS, ADAM_WD, ADAM_STEP = 0.001, 0.9, 0.999, 1e-08, 0.01, 10

HALO = 16
VMEM_LIMIT = 56 * 1024 * 1024
ROW_TILE = 256
COMM_LANES = 1024
MM_TM, MM_TN, MM_TK = 1024, 1024, 2048

MESH = pl.DeviceIdType.MESH

WEIGHTS = ['norm_mix_w', 'norm_ffn_w', 'norm_out_w', 'gla_w_in', 'gla_w_gk', 'gla_b_gk', 'gla_norm_w',
           'gla_w_out', 'lru_w_in', 'lru_conv_w', 'lru_conv_b', 'lru_w_ga', 'lru_b_ga', 'lru_w_gx',
           'lru_b_gx', 'lru_lambda', 'lru_w_out', 'ret_w_in', 'ret_norm_w', 'ret_w_out', 'ffn_w_up',
           'ffn_conv_w', 'ffn_conv_b', 'ffn_w_down']
SHARD_AX = {'norm_mix_w': None, 'norm_ffn_w': None, 'norm_out_w': None, 'gla_w_in': 2, 'gla_w_gk': 2,
            'gla_b_gk': 1, 'gla_norm_w': 1, 'gla_w_out': 1, 'lru_w_in': 2, 'lru_conv_w': 2,
            'lru_conv_b': None, 'lru_w_ga': 2, 'lru_b_ga': None, 'lru_w_gx': 2, 'lru_b_gx': None,
            'lru_lambda': None, 'lru_w_out': 1, 'ret_w_in': 2, 'ret_norm_w': 1, 'ret_w_out': 1,
            'ffn_w_up': 2, 'ffn_conv_w': 2, 'ffn_conv_b': None, 'ffn_w_down': 1}
BIG = ['gla_w_in', 'gla_w_out', 'lru_w_in', 'lru_w_ga', 'lru_w_gx', 'lru_w_out', 'ret_w_in', 'ret_w_out',
       'ffn_w_up', 'ffn_w_down']
GATHERED = ['gla_w_out', 'lru_w_in', 'lru_w_out', 'ret_w_in', 'ret_w_out', 'ffn_w_up', 'ffn_w_down']
SMALL_SHARDED = ['gla_w_gk', 'gla_b_gk', 'gla_norm_w', 'lru_conv_w', 'ret_norm_w', 'ffn_conv_w']
REPLICATED = [n for n in WEIGHTS if SHARD_AX[n] is None]


def _pcall(body, **kw):
    return pl.pallas_call(body, **kw)


def _params(sem=None, **kw):
    return pltpu.CompilerParams(dimension_semantics=sem, vmem_limit_bytes=VMEM_LIMIT, **kw)


def _tile(n, pref, align=128):
    if n <= pref:
        return n
    t = (pref // align) * align
    while t >= align:
        if n % t == 0:
            return t
        t -= align
    return n


def _row_iota(shape):
    return lax.broadcasted_iota(jnp.int32, shape, 0)


def _shift_down(x, halo, s):
    t, c = x.shape
    r = pltpu.roll(x.reshape(t // 8, 8, c), s, 1)
    prev = jnp.concatenate([pltpu.roll(halo, s, 0)[None], r[:-1]], axis=0)
    sub = lax.broadcasted_iota(jnp.int32, r.shape, 1)
    return jnp.where(sub < s, prev, r).reshape(t, c)


def _shift_up(x, nxt, s):
    t, c = x.shape
    r = pltpu.roll(x.reshape(t // 8, 8, c), 8 - s, 1)
    follow = jnp.concatenate([r[1:], pltpu.roll(nxt, 8 - s, 0)[None]], axis=0)
    sub = lax.broadcasted_iota(jnp.int32, r.shape, 1)
    return jnp.where(sub >= 8 - s, follow, r).reshape(t, c)


def _cumsum_rows(x):
    t, row, s = x.shape[0], _row_iota(x.shape), 1
    while s < t:
        x = x + jnp.where(row >= s, pltpu.roll(x, s, 0), 0.0)
        s *= 2
    return x


def _rev_cumsum_rows(x):
    t, row, s = x.shape[0], _row_iota(x.shape), 1
    while s < t:
        x = x + jnp.where(row < t - s, pltpu.roll(x, t - s, 0), 0.0)
        s *= 2
    return x


def _scan_fwd(a, u, h0):
    t, row, s = a.shape[0], _row_iota(a.shape), 1
    while s < t:
        keep = row >= s
        u = u + a * jnp.where(keep, pltpu.roll(u, s, 0), 0.0)
        a = a * jnp.where(keep, pltpu.roll(a, s, 0), 1.0)
        s *= 2
    return u + a * h0


def _scan_rev(c, g, d_end):
    t, row, s = c.shape[0], _row_iota(c.shape), 1
    while s < t:
        keep = row < t - s
        g = g + c * jnp.where(keep, pltpu.roll(g, t - s, 0), 0.0)
        c = c * jnp.where(keep, pltpu.roll(c, t - s, 0), 1.0)
        s *= 2
    return g + c * d_end


def _pick_row(x, r):
    return jnp.sum(jnp.where(_row_iota(x.shape) == r, x, 0.0), axis=0, keepdims=True)


def _sigmoid(x):
    return 1.0 / (1.0 + jnp.exp(-x))


def _softplus(x):
    return jnp.maximum(x, 0.0) + jnp.log(1.0 + jnp.exp(-jnp.abs(x)))


_GELU_C = math.sqrt(2.0 / math.pi)


def _gelu_and_grad(x):
    x2 = x * x
    th = jnp.tanh(_GELU_C * (x + 0.044715 * x * x2))
    g = 0.5 * x * (1.0 + th)
    dg = 0.5 * (1.0 + th) + 0.5 * x * (1.0 - th * th) * _GELU_C * (1.0 + 3.0 * 0.044715 * x2)
    return g, dg


def _neg_expm1(y):
    small = -(y * (1.0 + y * (0.5 + y * (1.0 / 6.0 + y * (1.0 / 24.0)))))
    return jnp.where(y > -0.01, small, 1.0 - jnp.exp(y))


def _dot(a, b, dims):
    return lax.dot_general(a.astype(BF16), b.astype(BF16), (dims, ((), ())), preferred_element_type=F32)


def _dot_nn(a, b):
    return _dot(a, b, ((1,), (0,)))


def _dot_nt(a, b):
    return _dot(a, b, ((1,), (1,)))


def _dot_tn(a, b):
    return _dot(a, b, ((0,), (0,)))


_View = collections.namedtuple("_View", "arr kind layer dev0 ndev")


def _view_shape(v):
    r, c = v.arr.shape[2:]
    return (r, v.ndev * c) if v.kind == 'col' else (N_DEV * r, c)


def _view_spec(v, tr, tc, rc_of):
    r, c = v.arr.shape[2:]
    if v.kind == 'col':
        per = c // tc

        def imap(*g):
            ri, ci = rc_of(*g)
            return (v.dev0 + ci // per, v.layer, ri, ci % per)
    elif tr > r:
        def imap_blocks(*g):
            ri, ci = rc_of(*g)
            return (ri, v.layer, 0, ci)
        return pl.BlockSpec((tr // r, None, r, tc), imap_blocks)
    else:
        per = r // tr

        def imap(*g):
            ri, ci = rc_of(*g)
            return (ri // per, v.layer, ri % per, ci)
    return pl.BlockSpec((None, None, tr, tc), imap)


def _row_tile(v, pref):
    r = v.arr.shape[2]
    if r >= pref:
        return _tile(r, pref)
    q = max(q for q in (1, 2, 4, 8) if r * q <= pref)
    return r * q


def _rows2d(val):
    return val.reshape(-1, val.shape[-1]) if val.ndim == 3 else val


def _mm(a, b, mode, out_dtype=F32, res=None, name="mm", out=None):
    bshape = _view_shape(b) if isinstance(b, _View) else b.shape
    if mode == 'nn':
        (m, k), n = a.shape, bshape[1]
    elif mode == 'nt':
        (m, k), n = a.shape, bshape[0]
    else:
        (k, m), n = a.shape, bshape[1]
    tm, tn, tk = _tile(m, MM_TM), _tile(n, MM_TN), _tile(k, MM_TK)
    if isinstance(b, _View):
        if b.kind == 'col' and mode == 'nt':
            tk = _tile(b.arr.shape[3], MM_TK)
        elif b.kind == 'col':
            tn = _tile(b.arr.shape[3], MM_TN)
        elif mode == 'nt':
            tn = _row_tile(b, MM_TN)
        else:
            tk = _row_tile(b, MM_TK)
    if out is not None:
        if out.kind == 'col':
            tn = _tile(out.arr.shape[3], MM_TN)
        else:
            tm = _row_tile(out, MM_TM)
    nk = k // tk
    a_spec = pl.BlockSpec((tk, tm), lambda i, j, kk: (kk, i)) if mode == 'tn' else pl.BlockSpec((tm, tk), lambda i, j, kk: (i, kk))
    if isinstance(b, _View):
        b_spec = (_view_spec(b, tn, tk, lambda i, j, kk: (j, kk)) if mode == 'nt'
                  else _view_spec(b, tk, tn, lambda i, j, kk: (kk, j)))
    else:
        b_spec = pl.BlockSpec((tn, tk), lambda i, j, kk: (j, kk)) if mode == 'nt' else pl.BlockSpec((tk, tn), lambda i, j, kk: (kk, j))
    r_spec = pl.BlockSpec((tm, tn), lambda i, j, kk: (i, j))
    o_spec = r_spec if out is None else _view_spec(out, tm, tn, lambda i, j, kk: (i, j))
    dot = {'nn': _dot_nn, 'nt': _dot_nt, 'tn': _dot_tn}[mode]

    def body(*refs):
        refs = list(refs)
        if out is not None:
            del refs[2 + (res is not None)]
        a_ref, b_ref = refs[:2]
        r_ref = None if res is None else refs[2]
        o_ref = refs[2 if res is None else 3]

        def finish(total):
            if res is not None:
                total = total + r_ref[...].astype(F32)
            o_ref[...] = total.astype(out_dtype).reshape(o_ref.shape)

        part = dot(a_ref[...], _rows2d(b_ref[...]))
        if nk == 1:
            finish(part)
            return
        acc = refs[-1]
        kk = pl.program_id(2)

        @pl.when(kk == 0)
        def _():
            acc[...] = part

        @pl.when(kk > 0)
        def _():
            acc[...] += part

        @pl.when(kk == nk - 1)
        def _():
            finish(acc[...])

    args, specs = [a, b.arr if isinstance(b, _View) else b], [a_spec, b_spec]
    if res is not None:
        args.append(res)
        specs.append(r_spec)
    aliases, out_shape = {}, jax.ShapeDtypeStruct((m, n), out_dtype)
    if out is not None:
        aliases, out_shape = {len(args): 0}, jax.ShapeDtypeStruct(out.arr.shape, out.arr.dtype)
        args.append(out.arr)
        specs.append(pl.BlockSpec(memory_space=pl.ANY))
    return _pcall(body, name=name, out_shape=out_shape,
                  grid=(m // tm, n // tn, nk), in_specs=specs, out_specs=o_spec,
                  scratch_shapes=[] if nk == 1 else [pltpu.VMEM((tm, tn), F32)], input_output_aliases=aliases,
                  compiler_params=_params(("parallel", "parallel", "arbitrary")))(*args)


def _rms_fwd(x, w):
    s, d = x.shape
    tr = _tile(s, ROW_TILE, 16)

    def body(x_ref, w_ref, o_ref):
        xv = x_ref[...]
        r = lax.rsqrt(jnp.mean(xv * xv, axis=-1, keepdims=True) + RMS_EPS)
        o_ref[...] = (xv * r * w_ref[...]).astype(BF16)

    return _pcall(body, name="rms_fwd", out_shape=jax.ShapeDtypeStruct((s, d), BF16), grid=(s // tr,),
                  in_specs=[pl.BlockSpec((tr, d), lambda i: (i, 0)), pl.BlockSpec((1, d), lambda i: (0, 0))],
                  out_specs=pl.BlockSpec((tr, d), lambda i: (i, 0)), compiler_params=_params(("parallel",)))(x, w)


def _rms_bwd(x, w, dh, dres):
    s, d = x.shape
    tr = _tile(s, ROW_TILE, 16)

    def body(x_ref, w_ref, dh_ref, dr_ref, dx_ref, dxb_ref, dw_ref):
        i = pl.program_id(0)
        xv = x_ref[...]
        r = lax.rsqrt(jnp.mean(xv * xv, axis=-1, keepdims=True) + RMS_EPS)
        xh = xv * r
        dhv = dh_ref[...].astype(F32)
        dxh = dhv * w_ref[...]
        dxv = dr_ref[...] + r * (dxh - xh * jnp.mean(dxh * xh, axis=-1, keepdims=True))
        dx_ref[...] = dxv
        dxb_ref[...] = dxv.astype(BF16)
        part = jnp.sum(dhv * xh, axis=0, keepdims=True)

        @pl.when(i == 0)
        def _():
            dw_ref[...] = part

        @pl.when(i > 0)
        def _():
            dw_ref[...] += part

    row = pl.BlockSpec((tr, d), lambda i: (i, 0))
    vec = pl.BlockSpec((1, d), lambda i: (0, 0))
    return _pcall(body, name="rms_bwd",
                  out_shape=(jax.ShapeDtypeStruct((s, d), F32), jax.ShapeDtypeStruct((s, d), BF16),
                             jax.ShapeDtypeStruct((1, d), F32)),
                  grid=(s // tr,), in_specs=[row, vec, row, row], out_specs=(row, row, vec),
                  compiler_params=_params(("arbitrary",)))(x, w, dh, dres)


def _final_loss(x, w, target):
    s, d = x.shape
    tr = _tile(s, ROW_TILE, 16)

    def body(x_ref, w_ref, t_ref, l_ref, dx_ref, dxb_ref, dw_ref):
        i = pl.program_id(0)
        xv = x_ref[...]
        r = lax.rsqrt(jnp.mean(xv * xv, axis=-1, keepdims=True) + RMS_EPS)
        xh = xv * r
        err = xh * w_ref[...] - t_ref[...]
        lpart = 0.5 * jnp.sum(jnp.mean(err * err, axis=-1, keepdims=True), axis=0, keepdims=True)
        dy = err * (1.0 / d)
        dxh = dy * w_ref[...]
        dxv = r * (dxh - xh * jnp.mean(dxh * xh, axis=-1, keepdims=True))
        dx_ref[...] = dxv
        dxb_ref[...] = dxv.astype(BF16)
        part = jnp.sum(dy * xh, axis=0, keepdims=True)

        @pl.when(i == 0)
        def _():
            dw_ref[...] = part
            l_ref[...] = jnp.broadcast_to(lpart, l_ref.shape)

        @pl.when(i > 0)
        def _():
            dw_ref[...] += part
            l_ref[...] += jnp.broadcast_to(lpart, l_ref.shape)

    row = pl.BlockSpec((tr, d), lambda i: (i, 0))
    vec = pl.BlockSpec((1, d), lambda i: (0, 0))
    return _pcall(body, name="final_loss",
                  out_shape=(jax.ShapeDtypeStruct((8, 128), F32), jax.ShapeDtypeStruct((s, d), F32),
                             jax.ShapeDtypeStruct((s, d), BF16), jax.ShapeDtypeStruct((1, d), F32)),
                  grid=(s // tr,), in_specs=[row, vec, row],
                  out_specs=(pl.BlockSpec((8, 128), lambda i: (0, 0)), row, row, vec),
                  compiler_params=_params(("arbitrary",)))(x, w, target)


def _ffn_up_act(h, w, cw, cb):
    s, k = h.shape
    c = w.arr.shape[3]
    half = w.ndev // 2
    f = half * c
    tm, tn = _tile(s, 512, 16), _tile(c, MM_TN)
    per, nj = c // tn, f // tn

    def body(h_ref, hh_ref, wg_ref, wv_ref, cwg_ref, cwv_ref, cbg_ref, cbv_ref,
             a_ref, gate_ref, val_ref, ug_ref, uv_ref):
        i = pl.program_id(1)
        hm, hh = h_ref[...], hh_ref[...]

        def conv_half(w_ref, cw_ref, cb_ref, u_ref):
            wt = w_ref[...]
            u = _dot_nn(hm, wt)
            hal = jnp.where(i > 0, _dot_nn(hh, wt)[8:16], 0.0)
            u_ref[...] = u.astype(BF16)
            return (cw_ref[2:3, :] * u + cw_ref[1:2, :] * _shift_down(u, hal, 1)
                    + cw_ref[0:1, :] * _shift_down(u, hal, 2) + cb_ref[...])

        gate = conv_half(wg_ref, cwg_ref, cbg_ref, ug_ref)
        val = conv_half(wv_ref, cwv_ref, cbv_ref, uv_ref)
        a_ref[...] = (_gelu_and_grad(gate)[0] * val).astype(BF16)
        gate_ref[...] = gate.astype(BF16)
        val_ref[...] = val.astype(BF16)

    def wspec(dev0):
        return pl.BlockSpec((None, None, k, tn), lambda j, i: (w.dev0 + dev0 + j // per, w.layer, 0, j % per))

    def vec(rows, off):
        return pl.BlockSpec((rows, tn), lambda j, i: (0, j + off))

    out = pl.BlockSpec((tm, tn), lambda j, i: (i, j))
    return _pcall(body, name="ffn_up_act", out_shape=(jax.ShapeDtypeStruct((s, f), BF16),) * 5, grid=(nj, s // tm),
                  in_specs=[pl.BlockSpec((tm, k), lambda j, i: (i, 0)),
                            pl.BlockSpec((HALO, k), lambda j, i: (jnp.maximum(i * (tm // HALO) - 1, 0), 0)),
                            wspec(0), wspec(half), vec(3, 0), vec(3, nj), vec(1, 0), vec(1, nj)],
                  out_specs=(out,) * 5,
                  compiler_params=_params(("parallel", "parallel")))(h, h, w.arr, w.arr, cw, cw, cb, cb)


def _ffn_down_dx_act(dy, w, ug, uv, gate, val, cw):
    s, k = dy.shape
    f = ug.shape[1]
    r = w.arr.shape[2]
    tm, tn = _tile(s, 512, 16), _tile(r, MM_TN)
    per, nj, ni = r // tn, f // tn, s // tm

    def body(dy_ref, dyn_ref, w_ref, ug_ref, uv_ref, gate_ref, val_ref, gaten_ref, valn_ref, wg_ref, wv_ref,
             dug_ref, duv_ref, dwg_ref, dwv_ref, dbg_ref, dbv_ref):
        i = pl.program_id(1)
        wt = w_ref[...]

        def d_gate_val(da, g_ref, v_ref):
            gl, dgl = _gelu_and_grad(g_ref[...].astype(F32))
            return da * v_ref[...].astype(F32) * dgl, da * gl

        dgate, dval = d_gate_val(_dot_nt(dy_ref[...], wt), gate_ref, val_ref)
        dgate_n, dval_n = d_gate_val(_dot_nt(dyn_ref[...], wt), gaten_ref, valn_ref)

        def back(d, d_next, x_ref, cw_ref, du_ref, dw_ref, db_ref):
            nxt = jnp.where(i < ni - 1, d_next[0:8], 0.0)
            ds = [_shift_up(d, nxt, 2), _shift_up(d, nxt, 1), d]
            du_ref[...] = (cw_ref[2:3, :] * ds[2] + cw_ref[1:2, :] * ds[1] + cw_ref[0:1, :] * ds[0]).astype(BF16)
            x = x_ref[...].astype(F32)
            parts = [jnp.sum(ds[t] * x, axis=0, keepdims=True) for t in range(3)]
            bpart = jnp.sum(d, axis=0, keepdims=True)

            @pl.when(i == 0)
            def _():
                for t in range(3):
                    dw_ref[t:t + 1, :] = parts[t]
                db_ref[...] = bpart

            @pl.when(i > 0)
            def _():
                for t in range(3):
                    dw_ref[t:t + 1, :] += parts[t]
                db_ref[...] += bpart

        back(dgate, dgate_n, ug_ref, wg_ref, dug_ref, dwg_ref, dbg_ref)
        back(dval, dval_n, uv_ref, wv_ref, duv_ref, dwv_ref, dbv_ref)

    def next_rows(cols, cmap):
        return pl.BlockSpec((HALO, cols), lambda j, i: (jnp.minimum((i + 1) * (tm // HALO), s // HALO - 1), cmap(j)))

    main = pl.BlockSpec((tm, tn), lambda j, i: (i, j))

    def vec(rows, off):
        return pl.BlockSpec((rows, tn), lambda j, i: (0, j + off))

    dug, duv, dwg, dwv, dbg, dbv = _pcall(
        body, name="ffn_down_dx_act",
        out_shape=(jax.ShapeDtypeStruct((s, f), BF16), jax.ShapeDtypeStruct((s, f), BF16),
                   jax.ShapeDtypeStruct((3, f), F32), jax.ShapeDtypeStruct((3, f), F32),
                   jax.ShapeDtypeStruct((1, f), F32), jax.ShapeDtypeStruct((1, f), F32)),
        grid=(nj, ni),
        in_specs=[pl.BlockSpec((tm, k), lambda j, i: (i, 0)), next_rows(k, lambda j: 0),
                  pl.BlockSpec((None, None, tn, k), lambda j, i: (j // per, w.layer, j % per, 0)),
                  main, main, main, main, next_rows(tn, lambda j: j), next_rows(tn, lambda j: j),
                  vec(3, 0), vec(3, nj)],
        out_specs=(main, main, vec(3, 0), vec(3, 0), vec(1, 0), vec(1, 0)),
        compiler_params=_params(("parallel", "arbitrary")))(dy, dy, w.arr, ug, uv, gate, val, gate, val, cw, cw)
    return dug, duv, jnp.concatenate([dwg, dwv], axis=1), jnp.concatenate([dbg, dbv], axis=1)


def _chunk_cols(h, kd, vd, heads):
    dk, dv = kd // heads, vd // heads
    return (slice(h * dk, (h + 1) * dk), slice(kd + h * dk, kd + (h + 1) * dk),
            slice(2 * kd + h * dv, 2 * kd + (h + 1) * dv), slice(2 * kd + vd + h * dv, 2 * kd + vd + (h + 1) * dv))


def _rope(x, cos, sin):
    half = x.shape[1] // 2
    x1, x2 = x[:, :half], x[:, half:]
    return jnp.concatenate([x1 * cos - x2 * sin, x2 * cos + x1 * sin], axis=1)


def _unrope(d, cos, sin):
    half = d.shape[1] // 2
    d1, d2 = d[:, :half], d[:, half:]
    return jnp.concatenate([d1 * cos + d2 * sin, d2 * cos - d1 * sin], axis=1)


def _chunk_inputs(gla, h, heads, kd, vd, rows, proj_ref, aux):
    qc, kc, vc, gc = _chunk_cols(h, kd, vd, heads)
    dk = kd // heads
    q = proj_ref[rows, qc].astype(F32)
    k = proj_ref[rows, kc].astype(F32)
    v = proj_ref[rows, vc]
    g = proj_ref[rows, gc].astype(F32)
    c = {}
    if gla:
        z_ref, wgk_ref, bgk_ref = aux
        c['z'] = z_ref[rows, :]
        c['gk'] = _dot_nn(c['z'], wgk_ref[:, qc]) + bgk_ref[:, qc]
        la = (jnp.minimum(c['gk'], 0.0) - jnp.log(1.0 + jnp.exp(-jnp.abs(c['gk'])))) * (1.0 / GLA_GATE_TAU)
        b = _cumsum_rows(la)
        bl = jnp.sum(la, axis=0, keepdims=True)
        q = q * (dk ** -0.5)
    else:
        cos_ref, sin_ref = aux
        c['cos'], c['sin'] = cos_ref[rows, :], sin_ref[rows, :]
        q = _rope(q, c['cos'], c['sin'])
        k = _rope(k, c['cos'], c['sin']) * (dk ** -0.5)
        lg = math.log(1.0 - 2.0 ** (-5.0 - h))
        b = lg * (_row_iota((CHUNK, 1)).astype(F32) + 1.0)
        bl = jnp.full((1, 1), lg * CHUNK, F32)
    eb, enb = jnp.exp(b), jnp.exp(-b)
    c.update(q=q, k=k, v=v, g=g, b=b, bl=bl, eb=eb, enb=enb, ebl=jnp.exp(bl),
             qd=q * eb, kg=k * enb, qg=q * enb, kd=k * eb, ks=k * jnp.exp(bl - b))
    lower = _row_iota((CHUNK, CHUNK)) >= lax.broadcasted_iota(jnp.int32, (CHUNK, CHUNK), 1)
    c['lower'] = lower
    c['A'] = jnp.where(lower, _dot_nt(c['qd'], c['kg']), _dot_nt(c['qg'], c['kd']))
    return c


def _head_norm(gla, o):
    if not gla:
        o = o - jnp.mean(o, axis=-1, keepdims=True)
    r = lax.rsqrt(jnp.mean(o * o, axis=-1, keepdims=True) + RMS_EPS)
    return o * r, r


def _chunk_fwd(gla, proj, aux_arrays, nw, heads, kd, vd, cps):
    s, pw = proj.shape
    dk, dv = kd // heads, vd // heads
    rt = CHUNK * cps
    nb = s // rt
    n_aux = len(aux_arrays)

    def body(*refs):
        proj_ref, aux, nw_ref = refs[0], refs[1:1 + n_aux], refs[1 + n_aux]
        y_ref, o_ref, st_ref, state = refs[2 + n_aux:]

        @pl.when(pl.program_id(0) == 0)
        def _():
            state[...] = jnp.zeros_like(state)

        def chunk(ci, carry):
            rows = pl.ds(pl.multiple_of(ci * CHUNK, CHUNK), CHUNK)
            for h in range(heads):
                c = _chunk_inputs(gla, h, heads, kd, vd, rows, proj_ref, aux)
                vcols = slice(h * dv, (h + 1) * dv)
                st0 = state[h]
                st_ref[ci, h] = st0.astype(BF16)
                o = _dot_nn(c['A'], c['v']) + _dot_nt(c['qd'], st0)
                state[h] = st0 * c['ebl'] + _dot_tn(c['v'], c['ks'])
                oh, _ = _head_norm(gla, o)
                gv = c['g']
                y_ref[rows, vcols] = (oh * nw_ref[:, vcols] * (gv * _sigmoid(gv))).astype(BF16)
                o_ref[rows, vcols] = o.astype(BF16)
            return carry

        lax.fori_loop(0, cps, chunk, 0)

    row = lambda w: pl.BlockSpec((rt, w), lambda n: (n, 0))
    full = lambda a: pl.BlockSpec(a.shape, lambda n: (0,) * a.ndim)
    aux_specs = [row(a.shape[1]) if a.shape[0] == s else full(a) for a in aux_arrays]
    return _pcall(
        body, name="gla_fwd" if gla else "ret_fwd",
        out_shape=(jax.ShapeDtypeStruct((s, vd), BF16), jax.ShapeDtypeStruct((s, vd), BF16),
                   jax.ShapeDtypeStruct((s // CHUNK, heads, dv, dk), BF16)),
        grid=(nb,), in_specs=[row(pw)] + aux_specs + [full(nw)],
        out_specs=(row(vd), row(vd), pl.BlockSpec((cps, heads, dv, dk), lambda n: (n, 0, 0, 0))),
        scratch_shapes=[pltpu.VMEM((heads, dv, dk), F32)],
        compiler_params=_params(("arbitrary",)))(proj, *aux_arrays, nw)


def _chunk_bwd(gla, proj, aux_arrays, nw, o_st, st, dy, heads, kd, vd, cps):
    s, pw = proj.shape
    dk, dv = kd // heads, vd // heads
    rt = CHUNK * cps
    nb = s // rt
    n_aux = len(aux_arrays)

    def body(*refs):
        proj_ref, aux, nw_ref = refs[0], refs[1:1 + n_aux], refs[1 + n_aux]
        o_ref, st_ref, dy_ref = refs[2 + n_aux:5 + n_aux]
        outs = refs[5 + n_aux:]
        dp_ref, dnw_ref = outs[0], outs[1]
        if gla:
            dz_ref, dwgk_ref, dbgk_ref, dstate = outs[2:]
        else:
            dstate = outs[2]

        @pl.when(pl.program_id(0) == 0)
        def _():
            dstate[...] = jnp.zeros_like(dstate)
            dnw_ref[...] = jnp.zeros_like(dnw_ref)
            if gla:
                dwgk_ref[...] = jnp.zeros_like(dwgk_ref)
                dbgk_ref[...] = jnp.zeros_like(dbgk_ref)

        def chunk(i, carry):
            ci = cps - 1 - i
            rows = pl.ds(pl.multiple_of(ci * CHUNK, CHUNK), CHUNK)
            dz = jnp.zeros((CHUNK, GATE_PAD), F32)
            for h in range(heads):
                c = _chunk_inputs(gla, h, heads, kd, vd, rows, proj_ref, aux)
                qc, kc, vc, gc = _chunk_cols(h, kd, vd, heads)
                vcols = slice(h * dv, (h + 1) * dv)
                o = o_ref[rows, vcols].astype(F32)
                oh, r = _head_norm(gla, o)
                dyv = dy_ref[rows, vcols].astype(F32)
                gv = c['g']
                sg = _sigmoid(gv)
                nwv = nw_ref[:, vcols]
                dp_ref[rows, gc] = (dyv * oh * nwv * (sg * (1.0 + gv * (1.0 - sg)))).astype(BF16)
                dn = dyv * (gv * sg)
                dnw_ref[:, vcols] += jnp.sum(dn * oh, axis=0, keepdims=True)
                doh = dn * nwv
                do = doh - oh * jnp.mean(doh * oh, axis=-1, keepdims=True)
                if not gla:
                    do = do - jnp.mean(doh, axis=-1, keepdims=True)
                do = r * do
                st0 = st_ref[ci, h]
                dst1 = dstate[h]
                v = c['v']
                da = _dot_nt(do, v)
                dal = jnp.where(c['lower'], da, 0.0)
                dau = da - dal
                dp_ref[rows, vc] = (_dot_tn(c['A'], do) + _dot_nt(c['ks'], dst1)).astype(BF16)
                dqd = _dot_nn(dal, c['kg']) + _dot_nn(do, st0)
                dkg = _dot_tn(dal, c['qd'])
                dqg = _dot_nn(dau, c['kd'])
                dkd = _dot_tn(dau, c['qg'])
                dks = _dot_nn(v, dst1)
                dstate[h] = _dot_tn(do, c['qd']) + dst1 * c['ebl']
                dq = dqd * c['eb'] + dqg * c['enb']
                dkk = dkg * c['enb'] + dkd * c['eb'] + dks * jnp.exp(c['bl'] - c['b'])
                if gla:
                    db = dqd * c['qd'] - dkg * c['kg'] - dqg * c['qg'] + dkd * c['kd'] - dks * c['ks']
                    dbl = (jnp.sum(dks * c['ks'], axis=0, keepdims=True)
                           + c['ebl'] * jnp.sum(dst1 * st0.astype(F32), axis=0, keepdims=True))
                    db = db + jnp.where(_row_iota(db.shape) == CHUNK - 1, dbl, 0.0)
                    dgk = _rev_cumsum_rows(db) * (1.0 / GLA_GATE_TAU) / (1.0 + jnp.exp(c['gk']))
                    _, wgk_ref, _ = aux
                    dz = dz + _dot_nt(dgk, wgk_ref[:, qc])
                    dwgk_ref[:, qc] += _dot_tn(c['z'], dgk)
                    dbgk_ref[:, qc] += jnp.sum(dgk, axis=0, keepdims=True)
                    dp_ref[rows, qc] = (dq * (dk ** -0.5)).astype(BF16)
                    dp_ref[rows, kc] = dkk.astype(BF16)
                else:
                    dp_ref[rows, qc] = _unrope(dq, c['cos'], c['sin']).astype(BF16)
                    dp_ref[rows, kc] = (_unrope(dkk, c['cos'], c['sin']) * (dk ** -0.5)).astype(BF16)
            if gla:
                dz_ref[rows, :] = dz
            return carry

        lax.fori_loop(0, cps, chunk, 0)

    row = lambda w: pl.BlockSpec((rt, w), lambda n: (nb - 1 - n, 0))
    full = lambda a: pl.BlockSpec(a.shape, lambda n: (0,) * a.ndim)
    aux_specs = [row(a.shape[1]) if a.shape[0] == s else full(a) for a in aux_arrays]
    out_shape = [jax.ShapeDtypeStruct((s, pw), BF16), jax.ShapeDtypeStruct((1, vd), F32)]
    out_specs = [row(pw), full(nw)]
    if gla:
        wgk, bgk = aux_arrays[1], aux_arrays[2]
        out_shape += [jax.ShapeDtypeStruct((s, GATE_PAD), F32), jax.ShapeDtypeStruct(wgk.shape, F32),
                      jax.ShapeDtypeStruct(bgk.shape, F32)]
        out_specs += [row(GATE_PAD), full(wgk), full(bgk)]
    return _pcall(
        body, name="gla_bwd" if gla else "ret_bwd", out_shape=tuple(out_shape), grid=(nb,),
        in_specs=[row(pw)] + aux_specs + [full(nw), row(vd),
                                          pl.BlockSpec((cps, heads, dv, dk), lambda n: (nb - 1 - n, 0, 0, 0)), row(vd)],
        out_specs=tuple(out_specs), scratch_shapes=[pltpu.VMEM((heads, dv, dk), F32)],
        compiler_params=_params(("arbitrary",)))(proj, *aux_arrays, nw, o_st, st, dy)


def _lru_gates(xc, wga_ref, bga_ref, wgx_ref, bgx_ref, lam_ref):
    r = _sigmoid(_dot_nn(xc, wga_ref[0]) + bga_ref[...])
    i = _sigmoid(_dot_nn(xc, wgx_ref[0]) + bgx_ref[...])
    sp = _softplus(-lam_ref[...])
    la = -LRU_C * r * sp
    return r, i, sp, la, jnp.exp(la), jnp.sqrt(_neg_expm1(2.0 * la))


def _lru_specs(w, nbk, tt, tmap):
    main = lambda off: pl.BlockSpec((tt, LRU_BLOCK_W), lambda n, t: (tmap(t), n + off))
    halo = pl.BlockSpec((HALO, LRU_BLOCK_W), lambda n, t: (jnp.maximum(tmap(t) * (tt // HALO) - 1, 0), n))
    vec = lambda rows: pl.BlockSpec((rows, LRU_BLOCK_W), lambda n, t: (0, n))
    mat = pl.BlockSpec((1, LRU_BLOCK_W, LRU_BLOCK_W), lambda n, t: (n, 0, 0))
    return main, halo, vec, mat


def _lru_fwd(proj, cw, cb, wga, bga, wgx, bgx, lam):
    s, w2 = proj.shape
    w = w2 // 2
    nbk, tt = w // LRU_BLOCK_W, _tile(s, ROW_TILE, 16)
    main, halo, vec, mat = _lru_specs(w, nbk, tt, lambda t: t)

    def body(x_ref, xh_ref, y_ref, cw_ref, cb_ref, wga_ref, bga_ref, wgx_ref, bgx_ref, lam_ref,
             out_ref, hs_ref, hcar):
        t = pl.program_id(1)
        x = x_ref[...]
        hal = jnp.where(t > 0, xh_ref[8:16, :], 0.0)
        xc = (cw_ref[3:4, :] * x + cw_ref[2:3, :] * _shift_down(x, hal, 1) + cw_ref[1:2, :] * _shift_down(x, hal, 2)
              + cw_ref[0:1, :] * _shift_down(x, hal, 3) + cb_ref[...])
        r, i, sp, la, a, mlt = _lru_gates(xc, wga_ref, bga_ref, wgx_ref, bgx_ref, lam_ref)

        @pl.when(t == 0)
        def _():
            hcar[...] = jnp.zeros_like(hcar)

        h = _scan_fwd(a, xc * i * mlt, hcar[...])
        hcar[...] = _pick_row(h, tt - 1)
        hs_ref[...] = h
        out_ref[...] = (h * _gelu_and_grad(y_ref[...])[0]).astype(BF16)

    return _pcall(body, name="lru_fwd",
                  out_shape=(jax.ShapeDtypeStruct((s, w), BF16), jax.ShapeDtypeStruct((s, w), F32)),
                  grid=(nbk, s // tt),
                  in_specs=[main(0), halo, main(nbk), vec(4), vec(1), mat, vec(1), mat, vec(1), vec(1)],
                  out_specs=(main(0), main(0)), scratch_shapes=[pltpu.VMEM((1, LRU_BLOCK_W), F32)],
                  compiler_params=_params(("parallel", "arbitrary")))(proj, proj, proj, cw, cb, wga, bga, wgx, bgx, lam)


def _lru_bwd(proj, hs, dout, cw, cb, wga, bga, wgx, bgx, lam):
    s, w2 = proj.shape
    w = w2 // 2
    nbk, tt = w // LRU_BLOCK_W, _tile(s, ROW_TILE, 16)
    nt = s // tt
    main, halo, vec, mat = _lru_specs(w, nbk, tt, lambda t: nt - 1 - t)

    def body(x_ref, xh_ref, y_ref, hs_ref, hh_ref, do_ref, cw_ref, cb_ref, wga_ref, bga_ref, wgx_ref, bgx_ref,
             lam_ref, dx_ref, dy_ref, dcw_ref, dcb_ref, dbga_ref, dbgx_ref, dlam_ref, dwga_ref, dwgx_ref,
             dhcar, dxcar):
        t = pl.program_id(1)
        first_tile = t == nt - 1

        @pl.when(t == 0)
        def _():
            dhcar[...] = jnp.zeros_like(dhcar)
            dxcar[...] = jnp.zeros_like(dxcar)

        x = x_ref[...]
        hal = jnp.where(first_tile, 0.0, xh_ref[8:16, :])
        xs = [x, _shift_down(x, hal, 1), _shift_down(x, hal, 2), _shift_down(x, hal, 3)]
        xc = cw_ref[3:4, :] * xs[0] + cw_ref[2:3, :] * xs[1] + cw_ref[1:2, :] * xs[2] + cw_ref[0:1, :] * xs[3] + cb_ref[...]
        r, i, sp, la, a, mlt = _lru_gates(xc, wga_ref, bga_ref, wgx_ref, bgx_ref, lam_ref)
        h = hs_ref[...]
        hprev = _shift_down(h, jnp.where(first_tile, 0.0, hh_ref[8:16, :]), 1)
        gl, dgl = _gelu_and_grad(y_ref[...])
        dov = do_ref[...].astype(F32)
        dy_ref[...] = (dov * h * dgl).astype(BF16)
        row = _row_iota(a.shape)
        coef = jnp.where(row == tt - 1, 1.0, pltpu.roll(a, tt - 1, 0))
        dh = _scan_rev(coef, dov * gl, dhcar[...])
        dhcar[...] = _pick_row(a * dh, 0)
        dxc = dh * i * mlt
        di = dh * xc * mlt
        dm = dh * xc * i
        dla = dh * hprev * a - dm * jnp.exp(2.0 * la) / mlt
        dpa = dla * (-LRU_C * sp) * r * (1.0 - r)
        dpx = di * i * (1.0 - i)
        dxc = dxc + _dot_nt(dpa, wga_ref[0]) + _dot_nt(dpx, wgx_ref[0])
        nxt = dxcar[...]
        dx_ref[...] = (cw_ref[3:4, :] * dxc + cw_ref[2:3, :] * _shift_up(dxc, nxt, 1)
                       + cw_ref[1:2, :] * _shift_up(dxc, nxt, 2) + cw_ref[0:1, :] * _shift_up(dxc, nxt, 3)).astype(BF16)
        dxcar[...] = dxc[0:8]
        colsum = lambda v: jnp.sum(v, axis=0, keepdims=True)
        parts = [(dcb_ref, colsum(dxc)), (dbga_ref, colsum(dpa)), (dbgx_ref, colsum(dpx)),
                 (dlam_ref, colsum(dla * LRU_C * r) * _sigmoid(-lam_ref[...]))]
        wparts = [colsum(dxc * xs[3 - k]) for k in range(4)]
        dwa, dwx = _dot_tn(xc, dpa), _dot_tn(xc, dpx)

        @pl.when(t == 0)
        def _():
            for ref, val in parts:
                ref[...] = val
            for k in range(4):
                dcw_ref[k:k + 1, :] = wparts[k]
            dwga_ref[0] = dwa
            dwgx_ref[0] = dwx

        @pl.when(t > 0)
        def _():
            for ref, val in parts:
                ref[...] += val
            for k in range(4):
                dcw_ref[k:k + 1, :] += wparts[k]
            dwga_ref[0] += dwa
            dwgx_ref[0] += dwx

    sd = jax.ShapeDtypeStruct
    return _pcall(
        body, name="lru_bwd",
        out_shape=(sd((s, w), BF16), sd((s, w), BF16), sd((4, w), F32), sd((1, w), F32), sd((1, w), F32),
                   sd((1, w), F32), sd((1, w), F32), sd(wga.shape, F32), sd(wgx.shape, F32)),
        grid=(nbk, nt),
        in_specs=[main(0), halo, main(nbk), main(0), halo, main(0), vec(4), vec(1), mat, vec(1), mat, vec(1), vec(1)],
        out_specs=(main(0), main(0), vec(4), vec(1), vec(1), vec(1), vec(1), mat, mat),
        scratch_shapes=[pltpu.VMEM((1, LRU_BLOCK_W), F32), pltpu.VMEM((8, LRU_BLOCK_W), F32)],
        compiler_params=_params(("parallel", "arbitrary")))(proj, proj, proj, hs, hs, dout, cw, cb, wga, bga, wgx, bgx, lam)


def _adamw(parts, w, m, v, layer=None, prev=None):
    shape = w.shape
    cols = shape[-1]
    layers = 1 if layer is None else shape[0]
    w3, m3, v3 = (a.reshape(layers, -1, cols) for a in (w, m, v))
    rows = w3.shape[1]
    tr = _tile(rows, max(16, (1 << 17) // cols // 16 * 16), 16)
    blk = pl.BlockSpec((None, tr, cols), lambda i: (layer or 0, i, 0))
    p_args, p_specs = [], []
    for part in parts:
        if isinstance(part, tuple):
            stack, idx = part
            p_args.append(stack.reshape(stack.shape[0], rows, cols))
            p_specs.append(pl.BlockSpec((None, tr, cols), lambda i, idx=idx: (idx, i, 0)))
        else:
            p_args.append(part.reshape(rows, cols))
            p_specs.append(pl.BlockSpec((tr, cols), lambda i: (i, 0)))
    n_parts = len(parts)
    n_prev = 0 if prev is None else 4

    def body(*refs):
        w_ref, m_ref, v_ref = refs[n_parts:n_parts + 3]
        g_ref, d_ref, nm_ref, nv_ref = refs[n_parts + 3 + n_prev:]
        gv = refs[0][...].astype(F32)
        for p_ref in refs[1:n_parts]:
            gv = gv + p_ref[...].astype(F32)
        g_ref[...] = gv
        nm = ADAM_B1 * m_ref[...] + (1.0 - ADAM_B1) * gv
        nv = ADAM_B2 * v_ref[...] + (1.0 - ADAM_B2) * (gv * gv)
        m_hat = nm / (1.0 - ADAM_B1 ** ADAM_STEP)
        v_hat = nv / (1.0 - ADAM_B2 ** ADAM_STEP)
        d_ref[...] = -ADAM_LR * (m_hat / (jnp.sqrt(v_hat) + ADAM_EPS) + ADAM_WD * w_ref[...])
        nm_ref[...] = nm
        nv_ref[...] = nv

    prev_args = [] if prev is None else [a.reshape(layers, rows, cols) for a in prev]
    n_in = n_parts + 3
    out = _pcall(body, name="adamw", out_shape=tuple(jax.ShapeDtypeStruct((layers, rows, cols), F32) for _ in range(4)),
                 grid=(rows // tr,), in_specs=p_specs + [blk] * 3 + [pl.BlockSpec(memory_space=pl.ANY)] * n_prev,
                 out_specs=(blk,) * 4, input_output_aliases={n_in + q: q for q in range(n_prev)},
                 compiler_params=_params(("parallel",)))(*p_args, w3, m3, v3, *prev_args)
    return tuple(o.reshape(shape) for o in out)


def _sum_list(arrs, out_dtype):
    shape = arrs[0].shape
    cols = shape[-1]
    flat = [a.reshape(-1, cols) for a in arrs]
    rows = flat[0].shape[0]
    tr = _tile(rows, max(16, (1 << 18) // cols // 16 * 16), 16)

    def body(*refs):
        acc = refs[0][...].astype(F32)
        for p_ref in refs[1:-1]:
            acc = acc + p_ref[...].astype(F32)
        refs[-1][...] = acc.astype(out_dtype)

    blk = pl.BlockSpec((tr, cols), lambda i: (i, 0))
    out = _pcall(body, name="sum_list", out_shape=jax.ShapeDtypeStruct((rows, cols), out_dtype), grid=(rows // tr,),
                 in_specs=[blk] * len(flat), out_specs=blk, compiler_params=_params(("parallel",)))(*flat)
    return out.reshape(shape)


def _mesh_pos():
    return lax.axis_index("x"), lax.axis_index("y"), lax.axis_index("c")


def _all_gather(arrs):
    n = len(arrs)

    def body(*refs):
        x_refs, out_refs = refs[:n], refs[n:2 * n]
        send_sems, recv_sems, local_sems = refs[2 * n:]
        x, y, c = _mesh_pos()
        me, sibling = (x, y, c), (x, y, 1 - c)
        chips = [(1 - x, y), (x, 1 - y), (1 - x, 1 - y)]

        def slot(a, px, py, pc):
            return out_refs[a].at[4 * px + 2 * py + pc]

        def copy(a, k, block, to, src=None):
            return pltpu.make_async_remote_copy(
                src_ref=slot(a, *block) if src is None else src, dst_ref=slot(a, *block),
                send_sem=send_sems.at[7 * a + k], recv_sem=recv_sems.at[7 * a + k], device_id=to, device_id_type=MESH)

        mine = [pltpu.make_async_copy(x_refs[a], slot(a, *me), local_sems.at[a]) for a in range(n)]
        for cp in mine:
            cp.start()
        first = []
        for j, chip in enumerate(chips):
            first += [copy(a, 1 + j, me, (*chip, c), src=x_refs[a]) for a in range(n)]
        first += [copy(a, 0, me, sibling, src=x_refs[a]) for a in range(n)]
        for cp in first:
            cp.start()
        passed = []
        for j, chip in enumerate(chips):
            for a in range(n):
                copy(a, 1 + j, (*chip, c), me).wait_recv()
                passed.append(copy(a, 4 + j, (*chip, c), sibling))
                passed[-1].start()
        for a in range(n):
            copy(a, 0, sibling, me).wait_recv()
        for j, chip in enumerate(chips):
            for a in range(n):
                copy(a, 4 + j, (*chip, 1 - c), me).wait_recv()
        for cp in first + passed:
            cp.wait_send()
        for cp in mine:
            cp.wait()

    hbm = pl.BlockSpec(memory_space=pl.ANY)
    return _pcall(body, name="all_gather",
                  out_shape=tuple(jax.ShapeDtypeStruct((N_DEV,) + a.shape, a.dtype) for a in arrs),
                  in_specs=[hbm] * n, out_specs=(hbm,) * n,
                  scratch_shapes=[pltpu.SemaphoreType.DMA((7 * n,)), pltpu.SemaphoreType.DMA((7 * n,)),
                                  pltpu.SemaphoreType.DMA((n,))],
                  compiler_params=pltpu.CompilerParams(has_side_effects=True))(*arrs)


def _peers():
    x, y, c = _mesh_pos()
    out = []
    for k in (1, 2, 4, 3, 5, 6, 7):
        px, py, pc = (x + (k >> 2)) % 2, (y + ((k >> 1) & 1)) % 2, (c + (k & 1)) % 2
        out.append((k, (px, py, pc), 4 * px + 2 * py + pc))
    return out, 4 * x + 2 * y + c


def _peer_copies(n, scatter, src_refs, land_refs, send_sems, recv_sems):
    peers, me = _peers()
    copies = []
    for k, peer, peer_id in peers:
        for a in range(n):
            copies.append(pltpu.make_async_remote_copy(
                src_ref=src_refs[a].at[peer_id] if scatter else src_refs[a],
                dst_ref=land_refs[a].at[k - 1] if scatter else land_refs[a].at[me],
                send_sem=send_sems.at[7 * a + k - 1], recv_sem=recv_sems.at[7 * a + k - 1],
                device_id=peer, device_id_type=MESH))
    return copies


_HBM = pl.BlockSpec(memory_space=pltpu.HBM)
_SEM = pl.BlockSpec(memory_space=pltpu.SEMAPHORE)
_EFFECT = pltpu.SideEffectType.DATAFLOW_SIDE_EFFECTING


def _in_hbm(a):
    return pltpu.with_memory_space_constraint(a, pltpu.HBM)


def _exchange_start(srcs, lands, scatter, name):
    n = len(srcs)

    def body(*refs):
        send_sems, recv_sems, token = refs[2 * n], refs[2 * n + 1], refs[-1]
        for cp in _peer_copies(n, scatter, refs[:n], refs[n:2 * n], send_sems, recv_sems):
            cp.start()
        token[...] = jnp.zeros_like(token)

    thru = [pltpu.HBM(a.shape, a.dtype) for a in list(srcs) + list(lands)]
    out = _pcall(body, name=name,
                 out_shape=(pltpu.SemaphoreType.DMA((7 * n,)), pltpu.SemaphoreType.DMA((7 * n,)), *thru,
                            jax.ShapeDtypeStruct((8, 128), F32)),
                 in_specs=[_HBM] * (2 * n), out_specs=(_SEM, _SEM, *([_HBM] * (2 * n)), pl.BlockSpec(memory_space=pltpu.VMEM)),
                 input_output_aliases={i: 2 + i for i in range(2 * n)},
                 compiler_params=pltpu.CompilerParams(has_side_effects=_EFFECT))(*[_in_hbm(a) for a in list(srcs) + list(lands)])
    return out[0], out[1], out[2:2 + n], out[2 + n:2 + 2 * n], out[-1]


def _exchange_wait(send_sems, recv_sems, srcs, lands, after, scatter, name):
    n = len(srcs)

    def body(*refs):
        for cp in _peer_copies(n, scatter, refs[:n], refs[n:2 * n], refs[2 * n], refs[2 * n + 1]):
            cp.wait_send()
            cp.wait_recv()

    thru = tuple(pltpu.HBM(a.shape, a.dtype) for a in list(srcs) + list(lands))
    out = _pcall(body, name=name, out_shape=thru,
                 in_specs=[_HBM] * (2 * n) + [_SEM, _SEM, pl.BlockSpec(memory_space=pl.ANY)], out_specs=(_HBM,) * (2 * n),
                 input_output_aliases={i: i for i in range(2 * n)},
                 compiler_params=pltpu.CompilerParams(has_side_effects=_EFFECT))(*srcs, *lands, send_sems, recv_sems, after)
    return out[:n], out[n:]


def _pack(arrays, dtype):
    lead = arrays[0].shape[:-1]
    flat = jnp.concatenate([a.astype(dtype) for a in arrays], axis=-1)
    n = flat.shape[-1]
    unit = 16 * COMM_LANES
    pad = (-n) % unit
    flat = jnp.pad(flat, [(0, 0)] * len(lead) + [(0, pad)])
    return flat.reshape(lead + ((n + pad) // COMM_LANES, COMM_LANES))


def _unpack(packed, shapes):
    lead = packed.shape[:-2]
    flat = packed.reshape(lead + (-1,))
    out, off = [], 0
    for shp in shapes:
        n = math.prod(shp)
        out.append(flat[..., off:off + n].reshape(lead + tuple(shp)))
        off += n
    return out


def _unshard(g, ax):
    shp = list(g.shape[1:])
    shp[ax] *= N_DEV
    return jnp.moveaxis(g, 0, ax).reshape(shp)


def _split(full, ax):
    shp = list(full.shape)
    r = full.reshape(shp[:ax] + [N_DEV, shp[ax] // N_DEV] + shp[ax + 1:])
    return jnp.moveaxis(r, ax, 0)


def _rope_tables(s, dk):
    half = dk // 2
    inv = ROPE_BASE ** (-jnp.arange(half, dtype=F32) / half)
    ang = jnp.arange(s, dtype=F32)[:, None] * inv[None, :]
    return jnp.cos(ang), jnp.sin(ang)


def _stage_big(s):
    i, ffn = divmod(s, 2)
    if ffn:
        return [('ffn_w_up', i), ('ffn_w_down', i)]
    kind, j = i % 3, i // 3
    mixer = [['gla_w_in', 'gla_w_out'], ['lru_w_in', 'lru_w_ga', 'lru_w_gx', 'lru_w_out'], ['ret_w_in', 'ret_w_out']][kind]
    return [(n, j) for n in mixer]


def _local_step(x, target, p, layer_weights, layer_grads, token):
    s, d = x.shape
    depth = p['norm_mix_w'].shape[0]
    gla_kd, gla_vd = d // 2, d
    ret_kd, ret_vd = d, 2 * d
    cos, sin = _rope_tables(s, ret_kd // RET_HEADS)
    row = lambda v: v.reshape(1, -1)
    saved = []
    cur, gb = {}, {}

    def view(name, layer=0, dev0=0, ndev=N_DEV, arr=None):
        return _View(cur[name] if arr is None else arr, 'col' if SHARD_AX[name] == 2 else 'row', 0, dev0, ndev)

    def dw_into(name, layer, a, b, label, dev0=0, ndev=N_DEV):
        gb[name] = _mm(a, b, 'tn', BF16, name=label, out=view(name, 0, dev0, ndev, gb[name]))

    def gla_weights(j):
        w_in = cur['gla_w_in']
        w_main, w_z = w_in[:, :2 * gla_kd + 2 * gla_vd], w_in[:, 2 * gla_kd + 2 * gla_vd:]
        w_z = jnp.pad(w_z, ((0, 0), (0, GATE_PAD - GLA_GATE_RANK)))
        w_gk = jnp.pad(p['gla_w_gk'][j], ((0, GATE_PAD - GLA_GATE_RANK), (0, 0)))
        return w_main, w_z, w_gk, row(p['gla_b_gk'][j]), row(p['gla_norm_w'][j]), view('gla_w_out', j)

    def lru_weights(j):
        return (view('lru_w_in', j), p['lru_conv_w'][j], row(p['lru_conv_b'][j]), cur['lru_w_ga'], row(p['lru_b_ga'][j]),
                cur['lru_w_gx'], row(p['lru_b_gx'][j]), row(p['lru_lambda'][j]), view('lru_w_out', j))

    for i in range(depth):
        kind, j = i % 3, i // 3
        cur = layer_weights(2 * i, x)
        h = _rms_fwd(x, row(p['norm_mix_w'][i]) + (token if i == 0 else 0.0))
        if kind == 0:
            w_main, w_z, w_gk, b_gk, nw, w_out = gla_weights(j)
            proj = _mm(h, w_main, 'nn', BF16, name="gla_in")
            z = _mm(h, w_z, 'nn', F32, name="gla_z")
            y, o_st, st = _chunk_fwd(True, proj, (z, w_gk, b_gk), nw, GLA_HEADS, gla_kd, gla_vd, 4)
            mix = (proj, z, o_st, st, y)
        elif kind == 1:
            w_in, cw, cb, wga, bga, wgx, bgx, lam, w_out = lru_weights(j)
            proj = _mm(h, w_in, 'nn', F32, name="lru_in")
            y, hs = _lru_fwd(proj, cw, cb, wga, bga, wgx, bgx, lam)
            mix = (proj, hs, y)
        else:
            nw, w_out = row(p['ret_norm_w'][j]), view('ret_w_out', j)
            proj = _mm(h, view('ret_w_in', j), 'nn', BF16, name="ret_in")
            y, o_st, st = _chunk_fwd(False, proj, (cos, sin), nw, RET_HEADS, ret_kd, ret_vd, 2)
            mix = (proj, o_st, st, y)
        x_mid = _mm(y, w_out, 'nn', F32, res=x, name="mix_out")
        cur.update(layer_weights(2 * i + 1, x_mid))
        h2 = _rms_fwd(x_mid, row(p['norm_ffn_w'][i]))
        act, gate, val, ug, uv = _ffn_up_act(h2, view('ffn_w_up', i), p['ffn_conv_w'][i], row(p['ffn_conv_b'][i]))
        x_out = _mm(act, view('ffn_w_down', i), 'nn', F32, res=x_mid, name="ffn_down")
        saved.append((x, h, mix, x_mid, h2, (ug, uv, gate, val), act, cur))
        x = x_out

    loss, dx, dxb, dw = _final_loss(x, row(p['norm_out_w']), target)
    g = {n: [None] * v.shape[0] for n, v in p.items() if n != 'norm_out_w'}
    g['norm_out_w'] = dw.reshape(-1)
    half = N_DEV // 2
    token = 0.0

    for i in reversed(range(depth)):
        kind, j = i % 3, i // 3
        x_in, h, mix, x_mid, h2, u, act, cur = saved[i]
        stage_names = [[n for n, _ in _stage_big(2 * i + part)] for part in (0, 1)]
        gb = {n: lax.empty(cur[n].shape, BF16) for n in stage_names[0] + stage_names[1] if n in GATHERED}
        dw_into('ffn_w_down', i, act, dxb, "ffn_down_dw")
        dug, duv, dcw, dcb = _ffn_down_dx_act(dxb, view('ffn_w_down', i), *u, p['ffn_conv_w'][i])
        g['ffn_conv_w'][i], g['ffn_conv_b'][i] = dcw, dcb.reshape(-1)
        dw_into('ffn_w_up', i, h2, dug, "ffn_up_dw", 0, half)
        dw_into('ffn_w_up', i, h2, duv, "ffn_up_dw", half, half)
        tok_ffn, dug, duv = lax.optimization_barrier(
            (layer_grads(2 * i + 1, {n: gb[n] for n in stage_names[1]}), dug, duv))
        dh2 = _mm(dug, view('ffn_w_up', i, 0, half), 'nt', F32, name="ffn_up_dx")
        dh2 = _mm(duv, view('ffn_w_up', i, half, half), 'nt', BF16, res=dh2, name="ffn_up_dx")
        dx, dxb, dnw = _rms_bwd(x_mid, row(p['norm_ffn_w'][i]) + token, dh2, dx)
        g['norm_ffn_w'][i] = dnw.reshape(-1)
        token = tok_ffn

        def scatter_mixer(v):
            return lax.optimization_barrier((layer_grads(2 * i, {n: gb[n] for n in stage_names[0]}), v))

        if kind == 0:
            w_main, w_z, w_gk, b_gk, nw, w_out = gla_weights(j)
            proj, z, o_st, st, y = mix
            dw_into('gla_w_out', j, y, dxb, "mix_out_dw")
            dy = _mm(dxb, w_out, 'nt', BF16, name="mix_out_dx")
            dproj, dnw, dz, dwgk, dbgk = _chunk_bwd(True, proj, (z, w_gk, b_gk), nw, o_st, st, dy,
                                                    GLA_HEADS, gla_kd, gla_vd, 4)
            g['gla_norm_w'][j], g['gla_b_gk'][j] = dnw.reshape(-1), dbgk.reshape(-1)
            g['gla_w_gk'][j] = dwgk[:GLA_GATE_RANK]
            dw_main = _mm(h, dproj, 'tn', F32, name="gla_in_dw")
            dw_z = _mm(h, dz, 'tn', F32, name="gla_z_dw")
            dw_in = jnp.concatenate([dw_main, dw_z[:, :GLA_GATE_RANK]], axis=1)
            gb['gla_w_in'] = _split(dw_in, 1).astype(BF16)
            tok_mix, dproj = scatter_mixer(dproj)
            dh = _mm(dproj, w_main, 'nt', F32, name="gla_in_dx")
            dh = _mm(dz, w_z, 'nt', BF16, res=dh, name="gla_z_dx")
        elif kind == 1:
            w_in, cw, cb, wga, bga, wgx, bgx, lam, w_out = lru_weights(j)
            proj, hs, y = mix
            dw_into('lru_w_out', j, y, dxb, "mix_out_dw")
            dy = _mm(dxb, w_out, 'nt', BF16, name="mix_out_dx")
            dxb, dyb, dcw, dcb, dbga, dbgx, dlam, dwga, dwgx = _lru_bwd(proj, hs, dy, cw, cb, wga, bga, wgx, bgx, lam)
            g['lru_conv_w'][j], g['lru_conv_b'][j] = dcw, dcb.reshape(-1)
            g['lru_b_ga'][j], g['lru_b_gx'][j], g['lru_lambda'][j] = dbga.reshape(-1), dbgx.reshape(-1), dlam.reshape(-1)
            gb['lru_w_ga'], gb['lru_w_gx'] = _split(dwga, 1).astype(BF16), _split(dwgx, 1).astype(BF16)
            dproj = jnp.concatenate([dxb, dyb], axis=1)
            dw_into('lru_w_in', j, h, dproj, "lru_in_dw")
            tok_mix, dproj = scatter_mixer(dproj)
            dh = _mm(dproj, w_in, 'nt', BF16, name="lru_in_dx")
        else:
            nw, w_out = row(p['ret_norm_w'][j]), view('ret_w_out', j)
            proj, o_st, st, y = mix
            dw_into('ret_w_out', j, y, dxb, "mix_out_dw")
            dy = _mm(dxb, w_out, 'nt', BF16, name="mix_out_dx")
            dproj, dnw = _chunk_bwd(False, proj, (cos, sin), nw, o_st, st, dy, RET_HEADS, ret_kd, ret_vd, 2)
            g['ret_norm_w'][j] = dnw.reshape(-1)
            dw_into('ret_w_in', j, h, dproj, "ret_in_dw")
            tok_mix, dproj = scatter_mixer(dproj)
            dh = _mm(dproj, view('ret_w_in', j), 'nt', BF16, name="ret_in_dx")
        dx, dxb, dnw = _rms_bwd(x_in, row(p['norm_mix_w'][i]) + token, dh, dx)
        g['norm_mix_w'][i] = dnw.reshape(-1)
        token = tok_mix

    grads = {n: (v if n == 'norm_out_w' else jnp.stack(v)) for n, v in g.items()}
    return loss[0, 0], dx, grads, token


def kernel(x, norm_mix_w, norm_ffn_w, norm_out_w, gla_w_in, gla_w_gk, gla_b_gk, gla_norm_w, gla_w_out, lru_w_in, lru_conv_w, lru_conv_b, lru_w_ga, lru_b_ga, lru_w_gx, lru_b_gx, lru_lambda, lru_w_out, ret_w_in, ret_norm_w, ret_w_out, ffn_w_up, ffn_conv_w, ffn_conv_b, ffn_w_down, loss_target, m_norm_mix_w, m_norm_ffn_w, m_norm_out_w, m_gla_w_in, m_gla_w_gk, m_gla_b_gk, m_gla_norm_w, m_gla_w_out, m_lru_w_in, m_lru_conv_w, m_lru_conv_b, m_lru_w_ga, m_lru_b_ga, m_lru_w_gx, m_lru_b_gx, m_lru_lambda, m_lru_w_out, m_ret_w_in, m_ret_norm_w, m_ret_w_out, m_ffn_w_up, m_ffn_conv_w, m_ffn_conv_b, m_ffn_w_down, v_norm_mix_w, v_norm_ffn_w, v_norm_out_w, v_gla_w_in, v_gla_w_gk, v_gla_b_gk, v_gla_norm_w, v_gla_w_out, v_lru_w_in, v_lru_conv_w, v_lru_conv_b, v_lru_w_ga, v_lru_b_ga, v_lru_w_gx, v_lru_b_gx, v_lru_lambda, v_lru_w_out, v_ret_w_in, v_ret_norm_w, v_ret_w_out, v_ffn_w_up, v_ffn_conv_w, v_ffn_conv_b, v_ffn_w_down):
    given = dict(locals())
    w = {n: given[n] for n in WEIGHTS}
    me_x, me_y, me_c = _mesh_pos()
    me = 4 * me_x + 2 * me_y + me_c

    n_stages = 2 * norm_mix_w.shape[0]

    def shards(s):
        return [w[n][j].astype(BF16) for n, j in _stage_big(s)]

    got0 = _all_gather(shards(0) + [_pack([w[n].reshape(-1) for n in SMALL_SHARDED], F32)])
    p = {n: w[n] for n in REPLICATED}
    for n, blk in zip(SMALL_SHARDED, _unpack(got0[-1], [w[n].shape for n in SMALL_SHARDED])):
        p[n] = _unshard(blk, SHARD_AX[n])
    gathers, token = {}, 0.0
    for s in range(1, n_stages):
        _, srcs = lax.optimization_barrier((got0[-1], shards(s)))
        lands =[lax.dynamic_update_index_in_dim(lax.empty((N_DEV,) + a.shape, BF16), a, me, 0) for a in srcs]
        *gathers[s], tok = _exchange_start(srcs, lands, False, "gather_start_%d" % s)
        token = token + tok[0, 0]

    def stage_weights(s, after):
        blocks = got0[:-1] if s == 0 else _exchange_wait(*gathers[s], after, False, "gather_wait_%d" % s)[1]
        out = {}
        for (n, _), blk in zip(_stage_big(s), blocks):
            out[n] = blk.reshape((N_DEV, 1) + blk.shape[1:]) if n in GATHERED else _unshard(blk, SHARD_AX[n] - 1)
        return out

    scatters = {}

    def stage_grads(s, gb):
        srcs = [gb[n].reshape((N_DEV,) + w[n].shape[1:]) for n, _ in _stage_big(s)]
        lands = [lax.empty((N_DEV - 1,) + a.shape[1:], BF16) for a in srcs]
        *scatters[s], tok = _exchange_start(srcs, lands, True, "scatter_start_%d" % s)
        return tok[0, 0]

    loss, grad_x, grads, token = _local_step(x[0], loss_target[0], p, stage_weights, stage_grads, token)
    loss = lax.psum(loss, ("x", "y", "c"))
    gw, delta, new_m, new_v = {}, {}, {}, {}

    after, big = grads['norm_out_w'] + token, {}
    for s in reversed(range(n_stages)):
        srcs, lands = _exchange_wait(*scatters[s], after, True, "scatter_wait_%d" % s)
        for (n, j), src, land in zip(_stage_big(s), srcs, lands):
            own = lax.dynamic_index_in_dim(src, me, 0, keepdims=False)
            big[n] = _adamw([own] + [(land, k) for k in range(N_DEV - 1)], w[n], given["m_" + n], given["v_" + n],
                            layer=j, prev=big.get(n))
            after = big[n][0]
    for n in BIG:
        gw[n], delta[n], new_m[n], new_v[n] = big[n]

    small = REPLICATED + SMALL_SHARDED
    _, packed = lax.optimization_barrier((after, _pack([grads[n].reshape(-1) for n in small], F32)))
    (parts,) = _all_gather([packed])
    summed = _unpack(_sum_list([parts[dev] for dev in range(N_DEV)], F32), [grads[n].shape for n in small])
    for n, gs in zip(small, summed):
        if SHARD_AX[n] is not None:
            gs = lax.dynamic_index_in_dim(_split(gs, SHARD_AX[n]), me, 0, keepdims=False)
        gw[n], delta[n], new_m[n], new_v[n] = _adamw([gs], w[n], given["m_" + n], given["v_" + n])

    return (loss, grad_x[None], *[gw[n] for n in WEIGHTS], *[delta[n] for n in WEIGHTS],
            *[new_m[n] for n in WEIGHTS], *[new_v[n] for n in WEIGHTS])
```

```python
import collections
import math

import jax
import jax.numpy as jnp
from jax import lax
from jax.experimental import pallas as pl
from jax.experimental.pallas import tpu as pltpu

F32 = jnp.float32
BF16 = jnp.bfloat16

N_DEV = 8
CHUNK = 64
RMS_EPS = 1e-6
GLA_HEADS = 4
GLA_GATE_RANK = 16
GLA_GATE_TAU = 16.0
GATE_PAD = 128
LRU_BLOCK_W = 256
LRU_C = 8.0
RET_HEADS = 8
ROPE_BASE = 10000.0
ADAM_LR, ADAM_B1, ADAM_B2, ADAM_EPS, ADAM_WD, ADAM_STEP = 0.001, 0.9, 0.999, 1e-08, 0.01, 10

HALO = 16
VMEM_LIMIT = 56 * 1024 * 1024
ROW_TILE = 256
COMM_LANES = 1024
MM_TM, MM_TN, MM_TK = 1024, 1024, 2048

MESH = pl.DeviceIdType.MESH

WEIGHTS = ['norm_mix_w', 'norm_ffn_w', 'norm_out_w', 'gla_w_in', 'gla_w_gk', 'gla_b_gk', 'gla_norm_w',
           'gla_w_out', 'lru_w_in', 'lru_conv_w', 'lru_conv_b', 'lru_w_ga', 'lru_b_ga', 'lru_w_gx',
           'lru_b_gx', 'lru_lambda', 'lru_w_out', 'ret_w_in', 'ret_norm_w', 'ret_w_out', 'ffn_w_up',
           'ffn_conv_w', 'ffn_conv_b', 'ffn_w_down']
SHARD_AX = {'norm_mix_w': None, 'norm_ffn_w': None, 'norm_out_w': None, 'gla_w_in': 2, 'gla_w_gk': 2,
            'gla_b_gk': 1, 'gla_norm_w': 1, 'gla_w_out': 1, 'lru_w_in': 2, 'lru_conv_w': 2,
            'lru_conv_b': None, 'lru_w_ga': 2, 'lru_b_ga': None, 'lru_w_gx': 2, 'lru_b_gx': None,
            'lru_lambda': None, 'lru_w_out': 1, 'ret_w_in': 2, 'ret_norm_w': 1, 'ret_w_out': 1,
            'ffn_w_up': 2, 'ffn_conv_w': 2, 'ffn_conv_b': None, 'ffn_w_down': 1}
BIG = ['gla_w_in', 'gla_w_out', 'lru_w_in', 'lru_w_ga', 'lru_w_gx', 'lru_w_out', 'ret_w_in', 'ret_w_out',
       'ffn_w_up', 'ffn_w_down']
GATHERED = ['gla_w_out', 'lru_w_in', 'lru_w_out', 'ret_w_in', 'ret_w_out', 'ffn_w_up', 'ffn_w_down']
SMALL_SHARDED = ['gla_w_gk', 'gla_b_gk', 'gla_norm_w', 'lru_conv_w', 'ret_norm_w', 'ffn_conv_w']
REPLICATED = [n for n in WEIGHTS if SHARD_AX[n] is None]


def _pcall(body, **kw):
    return pl.pallas_call(body, **kw)


def _params(sem=None, **kw):
    return pltpu.CompilerParams(dimension_semantics=sem, vmem_limit_bytes=VMEM_LIMIT, **kw)


def _tile(n, pref, align=128):
    if n <= pref:
        return n
    t = (pref // align) * align
    while t >= align:
        if n % t == 0:
            return t
        t -= align
    return n


def _row_iota(shape):
    return lax.broadcasted_iota(jnp.int32, shape, 0)


def _shift_down(x, halo, s):
    t, c = x.shape
    r = pltpu.roll(x.reshape(t // 8, 8, c), s, 1)
    prev = jnp.concatenate([pltpu.roll(halo, s, 0)[None], r[:-1]], axis=0)
    sub = lax.broadcasted_iota(jnp.int32, r.shape, 1)
    return jnp.where(sub < s, prev, r).reshape(t, c)


def _shift_up(x, nxt, s):
    t, c = x.shape
    r = pltpu.roll(x.reshape(t // 8, 8, c), 8 - s, 1)
    follow = jnp.concatenate([r[1:], pltpu.roll(nxt, 8 - s, 0)[None]], axis=0)
    sub = lax.broadcasted_iota(jnp.int32, r.shape, 1)
    return jnp.where(sub >= 8 - s, follow, r).reshape(t, c)


def _cumsum_rows(x):
    t, row, s = x.shape[0], _row_iota(x.shape), 1
    while s < t:
        x = x + jnp.where(row >= s, pltpu.roll(x, s, 0), 0.0)
        s *= 2
    return x


def _rev_cumsum_rows(x):
    t, row, s = x.shape[0], _row_iota(x.shape), 1
    while s < t:
        x = x + jnp.where(row < t - s, pltpu.roll(x, t - s, 0), 0.0)
        s *= 2
    return x


def _scan_fwd(a, u, h0):
    t, row, s = a.shape[0], _row_iota(a.shape), 1
    while s < t:
        keep = row >= s
        u = u + a * jnp.where(keep, pltpu.roll(u, s, 0), 0.0)
        a = a * jnp.where(keep, pltpu.roll(a, s, 0), 1.0)
        s *= 2
    return u + a * h0


def _scan_rev(c, g, d_end):
    t, row, s = c.shape[0], _row_iota(c.shape), 1
    while s < t:
        keep = row < t - s
        g = g + c * jnp.where(keep, pltpu.roll(g, t - s, 0), 0.0)
        c = c * jnp.where(keep, pltpu.roll(c, t - s, 0), 1.0)
        s *= 2
    return g + c * d_end


def _pick_row(x, r):
    return jnp.sum(jnp.where(_row_iota(x.shape) == r, x, 0.0), axis=0, keepdims=True)


def _sigmoid(x):
    return 1.0 / (1.0 + jnp.exp(-x))


def _softplus(x):
    return jnp.maximum(x, 0.0) + jnp.log(1.0 + jnp.exp(-jnp.abs(x)))


_GELU_C = math.sqrt(2.0 / math.pi)


def _gelu_and_grad(x):
    x2 = x * x
    th = jnp.tanh(_GELU_C * (x + 0.044715 * x * x2))
    g = 0.5 * x * (1.0 + th)
    dg = 0.5 * (1.0 + th) + 0.5 * x * (1.0 - th * th) * _GELU_C * (1.0 + 3.0 * 0.044715 * x2)
    return g, dg


def _neg_expm1(y):
    small = -(y * (1.0 + y * (0.5 + y * (1.0 / 6.0 + y * (1.0 / 24.0)))))
    return jnp.where(y > -0.01, small, 1.0 - jnp.exp(y))


def _dot(a, b, dims):
    return lax.dot_general(a.astype(BF16), b.astype(BF16), (dims, ((), ())), preferred_element_type=F32)


def _dot_nn(a, b):
    return _dot(a, b, ((1,), (0,)))


def _dot_nt(a, b):
    return _dot(a, b, ((1,), (1,)))


def _dot_tn(a, b):
    return _dot(a, b, ((0,), (0,)))


_View = collections.namedtuple("_View", "arr kind layer dev0 ndev")


def _view_shape(v):
    r, c = v.arr.shape[2:]
    return (r, v.ndev * c) if v.kind == 'col' else (N_DEV * r, c)


def _view_spec(v, tr, tc, rc_of):
    r, c = v.arr.shape[2:]
    if v.kind == 'col':
        per = c // tc

        def imap(*g):
            ri, ci = rc_of(*g)
            return (v.dev0 + ci // per, v.layer, ri, ci % per)
    elif tr > r:
        def imap_blocks(*g):
            ri, ci = rc_of(*g)
            return (ri, v.layer, 0, ci)
        return pl.BlockSpec((tr // r, None, r, tc), imap_blocks)
    else:
        per = r // tr

        def imap(*g):
            ri, ci = rc_of(*g)
            return (ri // per, v.layer, ri % per, ci)
    return pl.BlockSpec((None, None, tr, tc), imap)


def _row_tile(v, pref):
    r = v.arr.shape[2]
    if r >= pref:
        return _tile(r, pref)
    q = max(q for q in (1, 2, 4, 8) if r * q <= pref)
    return r * q


def _rows2d(val):
    return val.reshape(-1, val.shape[-1]) if val.ndim == 3 else val


def _mm(a, b, mode, out_dtype=F32, res=None, name="mm", out=None, pair2=None):
    bshape = _view_shape(b) if isinstance(b, _View) else b.shape
    if mode == 'nn':
        (m, k), n = a.shape, bshape[1]
    elif mode == 'nt':
        (m, k), n = a.shape, bshape[0]
    else:
        (k, m), n = a.shape, bshape[1]
    tm, tn, tk = _tile(m, MM_TM), _tile(n, MM_TN), _tile(k, MM_TK)
    if isinstance(b, _View):
        if b.kind == 'col' and mode == 'nt':
            tk = _tile(b.arr.shape[3], MM_TK)
        elif b.kind == 'col':
            tn = _tile(b.arr.shape[3], MM_TN)
        elif mode == 'nt':
            tn = _row_tile(b, MM_TN)
        else:
            tk = _row_tile(b, MM_TK)
    if out is not None:
        if out.kind == 'col':
            tn = _tile(out.arr.shape[3], MM_TN)
        else:
            tm = _row_tile(out, MM_TM)
    nk = k // tk
    a_spec = pl.BlockSpec((tk, tm), lambda i, j, kk: (kk, i)) if mode == 'tn' else pl.BlockSpec((tm, tk), lambda i, j, kk: (i, kk))
    if isinstance(b, _View):
        b_spec = (_view_spec(b, tn, tk, lambda i, j, kk: (j, kk)) if mode == 'nt'
                  else _view_spec(b, tk, tn, lambda i, j, kk: (kk, j)))
    else:
        b_spec = pl.BlockSpec((tn, tk), lambda i, j, kk: (j, kk)) if mode == 'nt' else pl.BlockSpec((tk, tn), lambda i, j, kk: (kk, j))
    r_spec = pl.BlockSpec((tm, tn), lambda i, j, kk: (i, j))
    o_spec = r_spec if out is None else _view_spec(out, tm, tn, lambda i, j, kk: (i, j))
    dot = {'nn': _dot_nn, 'nt': _dot_nt, 'tn': _dot_tn}[mode]

    def body(*refs):
        refs = list(refs)
        if out is not None:
            del refs[2 + (res is not None)]
        a_ref, b_ref = refs[:2]
        r_ref = None if res is None else refs[2]
        n_in = 2 + (res is not None) + (0 if pair2 is None else 2)
        o_ref = refs[n_in]

        def finish(total):
            if res is not None:
                total = total + r_ref[...].astype(F32)
            o_ref[...] = total.astype(out_dtype).reshape(o_ref.shape)

        part = dot(a_ref[...], _rows2d(b_ref[...]))
        if pair2 is not None:
            part = part + dot(refs[n_in - 2][...], _rows2d(refs[n_in - 1][...]))
        if nk == 1:
            finish(part)
            return
        acc = refs[-1]
        kk = pl.program_id(2)

        @pl.when(kk == 0)
        def _():
            acc[...] = part

        @pl.when(kk > 0)
        def _():
            acc[...] += part

        @pl.when(kk == nk - 1)
        def _():
            finish(acc[...])

    args, specs = [a, b.arr if isinstance(b, _View) else b], [a_spec, b_spec]
    if res is not None:
        args.append(res)
        specs.append(r_spec)
    aliases, out_shape = {}, jax.ShapeDtypeStruct((m, n), out_dtype)
    if out is not None:
        aliases, out_shape = {len(args): 0}, jax.ShapeDtypeStruct(out.arr.shape, out.arr.dtype)
        args.append(out.arr)
        specs.append(pl.BlockSpec(memory_space=pl.ANY))
    if pair2 is not None:
        a2, b2 = pair2
        args += [a2, b2.arr]
        specs += [a_spec, (_view_spec(b2, tn, tk, lambda i, j, kk: (j, kk)) if mode == 'nt'
                           else _view_spec(b2, tk, tn, lambda i, j, kk: (kk, j)))]
    return _pcall(body, name=name, out_shape=out_shape,
                  grid=(m // tm, n // tn, nk), in_specs=specs, out_specs=o_spec,
                  scratch_shapes=[] if nk == 1 else [pltpu.VMEM((tm, tn), F32)], input_output_aliases=aliases,
                  compiler_params=_params(("parallel", "parallel", "arbitrary")))(*args)


def _rms_fwd(x, w):
    s, d = x.shape
    tr = _tile(s, ROW_TILE, 16)

    def body(x_ref, w_ref, o_ref):
        xv = x_ref[...]
        r = lax.rsqrt(jnp.mean(xv * xv, axis=-1, keepdims=True) + RMS_EPS)
        o_ref[...] = (xv * r * w_ref[...]).astype(BF16)

    return _pcall(body, name="rms_fwd", out_shape=jax.ShapeDtypeStruct((s, d), BF16), grid=(s // tr,),
                  in_specs=[pl.BlockSpec((tr, d), lambda i: (i, 0)), pl.BlockSpec((1, d), lambda i: (0, 0))],
                  out_specs=pl.BlockSpec((tr, d), lambda i: (i, 0)), compiler_params=_params(("parallel",)))(x, w)


def _rms_bwd(x, w, dh, dres):
    s, d = x.shape
    tr = _tile(s, ROW_TILE, 16)

    def body(x_ref, w_ref, dh_ref, dr_ref, dx_ref, dxb_ref, dw_ref):
        i = pl.program_id(0)
        xv = x_ref[...]
        r = lax.rsqrt(jnp.mean(xv * xv, axis=-1, keepdims=True) + RMS_EPS)
        xh = xv * r
        dhv = dh_ref[...].astype(F32)
        dxh = dhv * w_ref[...]
        dxv = dr_ref[...] + r * (dxh - xh * jnp.mean(dxh * xh, axis=-1, keepdims=True))
        dx_ref[...] = dxv
        dxb_ref[...] = dxv.astype(BF16)
        part = jnp.sum(dhv * xh, axis=0, keepdims=True)

        @pl.when(i == 0)
        def _():
            dw_ref[...] = part

        @pl.when(i > 0)
        def _():
            dw_ref[...] += part

    row = pl.BlockSpec((tr, d), lambda i: (i, 0))
    vec = pl.BlockSpec((1, d), lambda i: (0, 0))
    return _pcall(body, name="rms_bwd",
                  out_shape=(jax.ShapeDtypeStruct((s, d), F32), jax.ShapeDtypeStruct((s, d), BF16),
                             jax.ShapeDtypeStruct((1, d), F32)),
                  grid=(s // tr,), in_specs=[row, vec, row, row], out_specs=(row, row, vec),
                  compiler_params=_params(("arbitrary",)))(x, w, dh, dres)


def _final_loss(x, w, target):
    s, d = x.shape
    tr = _tile(s, ROW_TILE, 16)

    def body(x_ref, w_ref, t_ref, l_ref, dx_ref, dxb_ref, dw_ref):
        i = pl.program_id(0)
        xv = x_ref[...]
        r = lax.rsqrt(jnp.mean(xv * xv, axis=-1, keepdims=True) + RMS_EPS)
        xh = xv * r
        err = xh * w_ref[...] - t_ref[...]
        lpart = 0.5 * jnp.sum(jnp.mean(err * err, axis=-1, keepdims=True), axis=0, keepdims=True)
        dy = err * (1.0 / d)
        dxh = dy * w_ref[...]
        dxv = r * (dxh - xh * jnp.mean(dxh * xh, axis=-1, keepdims=True))
        dx_ref[...] = dxv
        dxb_ref[...] = dxv.astype(BF16)
        part = jnp.sum(dy * xh, axis=0, keepdims=True)

        @pl.when(i == 0)
        def _():
            dw_ref[...] = part
            l_ref[...] = jnp.broadcast_to(lpart, l_ref.shape)

        @pl.when(i > 0)
        def _():
            dw_ref[...] += part
            l_ref[...] += jnp.broadcast_to(lpart, l_ref.shape)

    row = pl.BlockSpec((tr, d), lambda i: (i, 0))
    vec = pl.BlockSpec((1, d), lambda i: (0, 0))
    return _pcall(body, name="final_loss",
                  out_shape=(jax.ShapeDtypeStruct((8, 128), F32), jax.ShapeDtypeStruct((s, d), F32),
                             jax.ShapeDtypeStruct((s, d), BF16), jax.ShapeDtypeStruct((1, d), F32)),
                  grid=(s // tr,), in_specs=[row, vec, row],
                  out_specs=(pl.BlockSpec((8, 128), lambda i: (0, 0)), row, row, vec),
                  compiler_params=_params(("arbitrary",)))(x, w, target)


def _ffn_up_act(h, w, cw, cb):
    s, k = h.shape
    c = w.arr.shape[3]
    half = w.ndev // 2
    f = half * c
    tm, tn = _tile(s, 512, 16), _tile(c, MM_TN)
    per, nj = c // tn, f // tn

    def body(h_ref, hh_ref, wg_ref, wv_ref, cwg_ref, cwv_ref, cbg_ref, cbv_ref,
             a_ref, gate_ref, val_ref, ug_ref, uv_ref):
        i = pl.program_id(1)
        hm, hh = h_ref[...], hh_ref[...]

        def conv_half(w_ref, cw_ref, cb_ref, u_ref):
            wt = w_ref[...]
            u = _dot_nn(hm, wt)
            hal = jnp.where(i > 0, _dot_nn(hh, wt)[8:16], 0.0)
            u_ref[...] = u.astype(BF16)
            return (cw_ref[2:3, :] * u + cw_ref[1:2, :] * _shift_down(u, hal, 1)
                    + cw_ref[0:1, :] * _shift_down(u, hal, 2) + cb_ref[...])

        gate = conv_half(wg_ref, cwg_ref, cbg_ref, ug_ref)
        val = conv_half(wv_ref, cwv_ref, cbv_ref, uv_ref)
        a_ref[...] = (_gelu_and_grad(gate)[0] * val).astype(BF16)
        gate_ref[...] = gate.astype(BF16)
        val_ref[...] = val.astype(BF16)

    def wspec(dev0):
        return pl.BlockSpec((None, None, k, tn), lambda j, i: (w.dev0 + dev0 + j // per, w.layer, 0, j % per))

    def vec(rows, off):
        return pl.BlockSpec((rows, tn), lambda j, i: (0, j + off))

    out = pl.BlockSpec((tm, tn), lambda j, i: (i, j))
    return _pcall(body, name="ffn_up_act", out_shape=(jax.ShapeDtypeStruct((s, f), BF16),) * 5, grid=(nj, s // tm),
                  in_specs=[pl.BlockSpec((tm, k), lambda j, i: (i, 0)),
                            pl.BlockSpec((HALO, k), lambda j, i: (jnp.maximum(i * (tm // HALO) - 1, 0), 0)),
                            wspec(0), wspec(half), vec(3, 0), vec(3, nj), vec(1, 0), vec(1, nj)],
                  out_specs=(out,) * 5,
                  compiler_params=_params(("parallel", "parallel")))(h, h, w.arr, w.arr, cw, cw, cb, cb)


def _ffn_down_dx_act(dy, w, ug, uv, gate, val, cw):
    s, k = dy.shape
    f = ug.shape[1]
    r = w.arr.shape[2]
    tm, tn = _tile(s, 512, 16), _tile(r, MM_TN)
    per, nj, ni = r // tn, f // tn, s // tm

    def body(dy_ref, dyn_ref, w_ref, ug_ref, uv_ref, gate_ref, val_ref, gaten_ref, valn_ref, wg_ref, wv_ref,
             dug_ref, duv_ref, dwg_ref, dwv_ref, dbg_ref, dbv_ref):
        i = pl.program_id(1)
        wt = w_ref[...]

        def d_gate_val(da, g_ref, v_ref):
            gl, dgl = _gelu_and_grad(g_ref[...].astype(F32))
            return da * v_ref[...].astype(F32) * dgl, da * gl

        dgate, dval = d_gate_val(_dot_nt(dy_ref[...], wt), gate_ref, val_ref)
        dgate_n, dval_n = d_gate_val(_dot_nt(dyn_ref[...], wt), gaten_ref, valn_ref)

        def back(d, d_next, x_ref, cw_ref, du_ref, dw_ref, db_ref):
            nxt = jnp.where(i < ni - 1, d_next[0:8], 0.0)
            ds = [_shift_up(d, nxt, 2), _shift_up(d, nxt, 1), d]
            du_ref[...] = (cw_ref[2:3, :] * ds[2] + cw_ref[1:2, :] * ds[1] + cw_ref[0:1, :] * ds[0]).astype(BF16)
            x = x_ref[...].astype(F32)
            parts = [jnp.sum(ds[t] * x, axis=0, keepdims=True) for t in range(3)]
            bpart = jnp.sum(d, axis=0, keepdims=True)

            @pl.when(i == 0)
            def _():
                for t in range(3):
                    dw_ref[t:t + 1, :] = parts[t]
                db_ref[...] = bpart

            @pl.when(i > 0)
            def _():
                for t in range(3):
                    dw_ref[t:t + 1, :] += parts[t]
                db_ref[...] += bpart

        back(dgate, dgate_n, ug_ref, wg_ref, dug_ref, dwg_ref, dbg_ref)
        back(dval, dval_n, uv_ref, wv_ref, duv_ref, dwv_ref, dbv_ref)

    def next_rows(cols, cmap):
        return pl.BlockSpec((HALO, cols), lambda j, i: (jnp.minimum((i + 1) * (tm // HALO), s // HALO - 1), cmap(j)))

    main = pl.BlockSpec((tm, tn), lambda j, i: (i, j))

    def vec(rows, off):
        return pl.BlockSpec((rows, tn), lambda j, i: (0, j + off))

    dug, duv, dwg, dwv, dbg, dbv = _pcall(
        body, name="ffn_down_dx_act",
        out_shape=(jax.ShapeDtypeStruct((s, f), BF16), jax.ShapeDtypeStruct((s, f), BF16),
                   jax.ShapeDtypeStruct((3, f), F32), jax.ShapeDtypeStruct((3, f), F32),
                   jax.ShapeDtypeStruct((1, f), F32), jax.ShapeDtypeStruct((1, f), F32)),
        grid=(nj, ni),
        in_specs=[pl.BlockSpec((tm, k), lambda j, i: (i, 0)), next_rows(k, lambda j: 0),
                  pl.BlockSpec((None, None, tn, k), lambda j, i: (j // per, w.layer, j % per, 0)),
                  main, main, main, main, next_rows(tn, lambda j: j), next_rows(tn, lambda j: j),
                  vec(3, 0), vec(3, nj)],
        out_specs=(main, main, vec(3, 0), vec(3, 0), vec(1, 0), vec(1, 0)),
        compiler_params=_params(("parallel", "arbitrary")))(dy, dy, w.arr, ug, uv, gate, val, gate, val, cw, cw)
    return dug, duv, jnp.concatenate([dwg, dwv], axis=1), jnp.concatenate([dbg, dbv], axis=1)


def _chunk_cols(h, kd, vd, heads):
    dk, dv = kd // heads, vd // heads
    return (slice(h * dk, (h + 1) * dk), slice(kd + h * dk, kd + (h + 1) * dk),
            slice(2 * kd + h * dv, 2 * kd + (h + 1) * dv), slice(2 * kd + vd + h * dv, 2 * kd + vd + (h + 1) * dv))


def _rope(x, cos, sin):
    half = x.shape[1] // 2
    x1, x2 = x[:, :half], x[:, half:]
    return jnp.concatenate([x1 * cos - x2 * sin, x2 * cos + x1 * sin], axis=1)


def _unrope(d, cos, sin):
    half = d.shape[1] // 2
    d1, d2 = d[:, :half], d[:, half:]
    return jnp.concatenate([d1 * cos + d2 * sin, d2 * cos - d1 * sin], axis=1)


def _chunk_inputs(gla, h, heads, kd, vd, rows, proj_ref, aux):
    qc, kc, vc, gc = _chunk_cols(h, kd, vd, heads)
    dk = kd // heads
    q = proj_ref[rows, qc].astype(F32)
    k = proj_ref[rows, kc].astype(F32)
    v = proj_ref[rows, vc]
    g = proj_ref[rows, gc].astype(F32)
    c = {}
    if gla:
        z_ref, wgk_ref, bgk_ref = aux
        c['z'] = z_ref[rows, :]
        c['gk'] = _dot_nn(c['z'], wgk_ref[:, qc]) + bgk_ref[:, qc]
        la = (jnp.minimum(c['gk'], 0.0) - jnp.log(1.0 + jnp.exp(-jnp.abs(c['gk'])))) * (1.0 / GLA_GATE_TAU)
        b = _cumsum_rows(la)
        bl = jnp.sum(la, axis=0, keepdims=True)
        q = q * (dk ** -0.5)
    else:
        cos_ref, sin_ref = aux
        c['cos'], c['sin'] = cos_ref[rows, :], sin_ref[rows, :]
        q = _rope(q, c['cos'], c['sin'])
        k = _rope(k, c['cos'], c['sin']) * (dk ** -0.5)
        lg = math.log(1.0 - 2.0 ** (-5.0 - h))
        b = lg * (_row_iota((CHUNK, 1)).astype(F32) + 1.0)
        bl = jnp.full((1, 1), lg * CHUNK, F32)
    eb, enb = jnp.exp(b), jnp.exp(-b)
    c.update(q=q, k=k, v=v, g=g, b=b, bl=bl, eb=eb, enb=enb, ebl=jnp.exp(bl),
             qd=q * eb, kg=k * enb, qg=q * enb, kd=k * eb, ks=k * jnp.exp(bl - b))
    lower = _row_iota((CHUNK, CHUNK)) >= lax.broadcasted_iota(jnp.int32, (CHUNK, CHUNK), 1)
    c['lower'] = lower
    c['A'] = jnp.where(lower, _dot_nt(c['qd'], c['kg']), _dot_nt(c['qg'], c['kd']))
    return c


def _head_norm(gla, o):
    if not gla:
        o = o - jnp.mean(o, axis=-1, keepdims=True)
    r = lax.rsqrt(jnp.mean(o * o, axis=-1, keepdims=True) + RMS_EPS)
    return o * r, r


def _chunk_fwd(gla, proj, aux_arrays, nw, heads, kd, vd, cps):
    s, pw = proj.shape
    dk, dv = kd // heads, vd // heads
    rt = CHUNK * cps
    nb = s // rt
    n_aux = len(aux_arrays)

    def body(*refs):
        proj_ref, aux, nw_ref = refs[0], refs[1:1 + n_aux], refs[1 + n_aux]
        y_ref, o_ref, st_ref, state = refs[2 + n_aux:]

        @pl.when(pl.program_id(0) == 0)
        def _():
            state[...] = jnp.zeros_like(state)

        def chunk(ci, carry):
            rows = pl.ds(pl.multiple_of(ci * CHUNK, CHUNK), CHUNK)
            for h in range(heads):
                c = _chunk_inputs(gla, h, heads, kd, vd, rows, proj_ref, aux)
                vcols = slice(h * dv, (h + 1) * dv)
                st0 = state[h]
                st_ref[ci, h] = st0.astype(BF16)
                o = _dot_nn(c['A'], c['v']) + _dot_nt(c['qd'], st0)
                state[h] = st0 * c['ebl'] + _dot_tn(c['v'], c['ks'])
                oh, _ = _head_norm(gla, o)
                gv = c['g']
                y_ref[rows, vcols] = (oh * nw_ref[:, vcols] * (gv * _sigmoid(gv))).astype(BF16)
                o_ref[rows, vcols] = o.astype(BF16)
            return carry

        lax.fori_loop(0, cps, chunk, 0)

    row = lambda w: pl.BlockSpec((rt, w), lambda n: (n, 0))
    full = lambda a: pl.BlockSpec(a.shape, lambda n: (0,) * a.ndim)
    aux_specs = [row(a.shape[1]) if a.shape[0] == s else full(a) for a in aux_arrays]
    return _pcall(
        body, name="gla_fwd" if gla else "ret_fwd",
        out_shape=(jax.ShapeDtypeStruct((s, vd), BF16), jax.ShapeDtypeStruct((s, vd), BF16),
                   jax.ShapeDtypeStruct((s // CHUNK, heads, dv, dk), BF16)),
        grid=(nb,), in_specs=[row(pw)] + aux_specs + [full(nw)],
        out_specs=(row(vd), row(vd), pl.BlockSpec((cps, heads, dv, dk), lambda n: (n, 0, 0, 0))),
        scratch_shapes=[pltpu.VMEM((heads, dv, dk), F32)],
        compiler_params=_params(("arbitrary",)))(proj, *aux_arrays, nw)


def _chunk_bwd(gla, proj, aux_arrays, nw, o_st, st, dy, heads, kd, vd, cps):
    s, pw = proj.shape
    dk, dv = kd // heads, vd // heads
    rt = CHUNK * cps
    nb = s // rt
    n_aux = len(aux_arrays)

    def body(*refs):
        proj_ref, aux, nw_ref = refs[0], refs[1:1 + n_aux], refs[1 + n_aux]
        o_ref, st_ref, dy_ref = refs[2 + n_aux:5 + n_aux]
        outs = refs[5 + n_aux:]
        dp_ref, dnw_ref = outs[0], outs[1]
        if gla:
            dz_ref, dwgk_ref, dbgk_ref, dstate = outs[2:]
        else:
            dstate = outs[2]

        @pl.when(pl.program_id(0) == 0)
        def _():
            dstate[...] = jnp.zeros_like(dstate)
            dnw_ref[...] = jnp.zeros_like(dnw_ref)
            if gla:
                dwgk_ref[...] = jnp.zeros_like(dwgk_ref)
                dbgk_ref[...] = jnp.zeros_like(dbgk_ref)

        def chunk(i, carry):
            ci = cps - 1 - i
            rows = pl.ds(pl.multiple_of(ci * CHUNK, CHUNK), CHUNK)
            dz = jnp.zeros((CHUNK, GATE_PAD), F32)
            for h in range(heads):
                c = _chunk_inputs(gla, h, heads, kd, vd, rows, proj_ref, aux)
                qc, kc, vc, gc = _chunk_cols(h, kd, vd, heads)
                vcols = slice(h * dv, (h + 1) * dv)
                o = o_ref[rows, vcols].astype(F32)
                oh, r = _head_norm(gla, o)
                dyv = dy_ref[rows, vcols].astype(F32)
                gv = c['g']
                sg = _sigmoid(gv)
                nwv = nw_ref[:, vcols]
                dp_ref[rows, gc] = (dyv * oh * nwv * (sg * (1.0 + gv * (1.0 - sg)))).astype(BF16)
                dn = dyv * (gv * sg)
                dnw_ref[:, vcols] += jnp.sum(dn * oh, axis=0, keepdims=True)
                doh = dn * nwv
                do = doh - oh * jnp.mean(doh * oh, axis=-1, keepdims=True)
                if not gla:
                    do = do - jnp.mean(doh, axis=-1, keepdims=True)
                do = r * do
                st0 = st_ref[ci, h]
                dst1 = dstate[h]
                v = c['v']
                da = _dot_nt(do, v)
                dal = jnp.where(c['lower'], da, 0.0)
                dau = da - dal
                dp_ref[rows, vc] = (_dot_tn(c['A'], do) + _dot_nt(c['ks'], dst1)).astype(BF16)
                dqd = _dot_nn(dal, c['kg']) + _dot_nn(do, st0)
                dkg = _dot_tn(dal, c['qd'])
                dqg = _dot_nn(dau, c['kd'])
                dkd = _dot_tn(dau, c['qg'])
                dks = _dot_nn(v, dst1)
                dstate[h] = _dot_tn(do, c['qd']) + dst1 * c['ebl']
                dq = dqd * c['eb'] + dqg * c['enb']
                dkk = dkg * c['enb'] + dkd * c['eb'] + dks * jnp.exp(c['bl'] - c['b'])
                if gla:
                    db = dqd * c['qd'] - dkg * c['kg'] - dqg * c['qg'] + dkd * c['kd'] - dks * c['ks']
                    dbl = (jnp.sum(dks * c['ks'], axis=0, keepdims=True)
                           + c['ebl'] * jnp.sum(dst1 * st0.astype(F32), axis=0, keepdims=True))
                    db = db + jnp.where(_row_iota(db.shape) == CHUNK - 1, dbl, 0.0)
                    dgk = _rev_cumsum_rows(db) * (1.0 / GLA_GATE_TAU) / (1.0 + jnp.exp(c['gk']))
                    _, wgk_ref, _ = aux
                    dz = dz + _dot_nt(dgk, wgk_ref[:, qc])
                    dwgk_ref[:, qc] += _dot_tn(c['z'], dgk)
                    dbgk_ref[:, qc] += jnp.sum(dgk, axis=0, keepdims=True)
                    dp_ref[rows, qc] = (dq * (dk ** -0.5)).astype(BF16)
                    dp_ref[rows, kc] = dkk.astype(BF16)
                else:
                    dp_ref[rows, qc] = _unrope(dq, c['cos'], c['sin']).astype(BF16)
                    dp_ref[rows, kc] = (_unrope(dkk, c['cos'], c['sin']) * (dk ** -0.5)).astype(BF16)
            if gla:
                dz_ref[rows, :] = dz
            return carry

        lax.fori_loop(0, cps, chunk, 0)

    row = lambda w: pl.BlockSpec((rt, w), lambda n: (nb - 1 - n, 0))
    full = lambda a: pl.BlockSpec(a.shape, lambda n: (0,) * a.ndim)
    aux_specs = [row(a.shape[1]) if a.shape[0] == s else full(a) for a in aux_arrays]
    out_shape = [jax.ShapeDtypeStruct((s, pw), BF16), jax.ShapeDtypeStruct((1, vd), F32)]
    out_specs = [row(pw), full(nw)]
    if gla:
        wgk, bgk = aux_arrays[1], aux_arrays[2]
        out_shape += [jax.ShapeDtypeStruct((s, GATE_PAD), F32), jax.ShapeDtypeStruct(wgk.shape, F32),
                      jax.ShapeDtypeStruct(bgk.shape, F32)]
        out_specs += [row(GATE_PAD), full(wgk), full(bgk)]
    return _pcall(
        body, name="gla_bwd" if gla else "ret_bwd", out_shape=tuple(out_shape), grid=(nb,),
        in_specs=[row(pw)] + aux_specs + [full(nw), row(vd),
                                          pl.BlockSpec((cps, heads, dv, dk), lambda n: (nb - 1 - n, 0, 0, 0)), row(vd)],
        out_specs=tuple(out_specs), scratch_shapes=[pltpu.VMEM((heads, dv, dk), F32)],
        compiler_params=_params(("arbitrary",)))(proj, *aux_arrays, nw, o_st, st, dy)


def _lru_gates(xc, wga_ref, bga_ref, wgx_ref, bgx_ref, lam_ref):
    r = _sigmoid(_dot_nn(xc, wga_ref[0]) + bga_ref[...])
    i = _sigmoid(_dot_nn(xc, wgx_ref[0]) + bgx_ref[...])
    sp = _softplus(-lam_ref[...])
    la = -LRU_C * r * sp
    return r, i, sp, la, jnp.exp(la), jnp.sqrt(_neg_expm1(2.0 * la))


def _lru_specs(w, nbk, tt, tmap):
    main = lambda off: pl.BlockSpec((tt, LRU_BLOCK_W), lambda n, t: (tmap(t), n + off))
    halo = pl.BlockSpec((HALO, LRU_BLOCK_W), lambda n, t: (jnp.maximum(tmap(t) * (tt // HALO) - 1, 0), n))
    vec = lambda rows: pl.BlockSpec((rows, LRU_BLOCK_W), lambda n, t: (0, n))
    mat = pl.BlockSpec((1, LRU_BLOCK_W, LRU_BLOCK_W), lambda n, t: (n, 0, 0))
    return main, halo, vec, mat


def _lru_fwd(proj, cw, cb, wga, bga, wgx, bgx, lam):
    s, w2 = proj.shape
    w = w2 // 2
    nbk, tt = w // LRU_BLOCK_W, _tile(s, ROW_TILE, 16)
    main, halo, vec, mat = _lru_specs(w, nbk, tt, lambda t: t)

    def body(x_ref, xh_ref, y_ref, cw_ref, cb_ref, wga_ref, bga_ref, wgx_ref, bgx_ref, lam_ref,
             out_ref, hs_ref, hcar):
        t = pl.program_id(1)
        x = x_ref[...]
        hal = jnp.where(t > 0, xh_ref[8:16, :], 0.0)
        xc = (cw_ref[3:4, :] * x + cw_ref[2:3, :] * _shift_down(x, hal, 1) + cw_ref[1:2, :] * _shift_down(x, hal, 2)
              + cw_ref[0:1, :] * _shift_down(x, hal, 3) + cb_ref[...])
        r, i, sp, la, a, mlt = _lru_gates(xc, wga_ref, bga_ref, wgx_ref, bgx_ref, lam_ref)

        @pl.when(t == 0)
        def _():
            hcar[...] = jnp.zeros_like(hcar)

        h = _scan_fwd(a, xc * i * mlt, hcar[...])
        hcar[...] = _pick_row(h, tt - 1)
        hs_ref[...] = h
        out_ref[...] = (h * _gelu_and_grad(y_ref[...])[0]).astype(BF16)

    return _pcall(body, name="lru_fwd",
                  out_shape=(jax.ShapeDtypeStruct((s, w), BF16), jax.ShapeDtypeStruct((s, w), F32)),
                  grid=(nbk, s // tt),
                  in_specs=[main(0), halo, main(nbk), vec(4), vec(1), mat, vec(1), mat, vec(1), vec(1)],
                  out_specs=(main(0), main(0)), scratch_shapes=[pltpu.VMEM((1, LRU_BLOCK_W), F32)],
                  compiler_params=_params(("parallel", "arbitrary")))(proj, proj, proj, cw, cb, wga, bga, wgx, bgx, lam)


def _lru_bwd(proj, hs, dout, cw, cb, wga, bga, wgx, bgx, lam):
    s, w2 = proj.shape
    w = w2 // 2
    nbk, tt = w // LRU_BLOCK_W, _tile(s, ROW_TILE, 16)
    nt = s // tt
    main, halo, vec, mat = _lru_specs(w, nbk, tt, lambda t: nt - 1 - t)

    def body(x_ref, xh_ref, y_ref, hs_ref, hh_ref, do_ref, cw_ref, cb_ref, wga_ref, bga_ref, wgx_ref, bgx_ref,
             lam_ref, dx_ref, dy_ref, dcw_ref, dcb_ref, dbga_ref, dbgx_ref, dlam_ref, dwga_ref, dwgx_ref,
             dhcar, dxcar):
        t = pl.program_id(1)
        first_tile = t == nt - 1

        @pl.when(t == 0)
        def _():
            dhcar[...] = jnp.zeros_like(dhcar)
            dxcar[...] = jnp.zeros_like(dxcar)

        x = x_ref[...]
        hal = jnp.where(first_tile, 0.0, xh_ref[8:16, :])
        xs = [x, _shift_down(x, hal, 1), _shift_down(x, hal, 2), _shift_down(x, hal, 3)]
        xc = cw_ref[3:4, :] * xs[0] + cw_ref[2:3, :] * xs[1] + cw_ref[1:2, :] * xs[2] + cw_ref[0:1, :] * xs[3] + cb_ref[...]
        r, i, sp, la, a, mlt = _lru_gates(xc, wga_ref, bga_ref, wgx_ref, bgx_ref, lam_ref)
        h = hs_ref[...]
        hprev = _shift_down(h, jnp.where(first_tile, 0.0, hh_ref[8:16, :]), 1)
        gl, dgl = _gelu_and_grad(y_ref[...])
        dov = do_ref[...].astype(F32)
        dy_ref[...] = (dov * h * dgl).astype(BF16)
        row = _row_iota(a.shape)
        coef = jnp.where(row == tt - 1, 1.0, pltpu.roll(a, tt - 1, 0))
        dh = _scan_rev(coef, dov * gl, dhcar[...])
        dhcar[...] = _pick_row(a * dh, 0)
        dxc = dh * i * mlt
        di = dh * xc * mlt
        dm = dh * xc * i
        dla = dh * hprev * a - dm * jnp.exp(2.0 * la) / mlt
        dpa = dla * (-LRU_C * sp) * r * (1.0 - r)
        dpx = di * i * (1.0 - i)
        dxc = dxc + _dot_nt(dpa, wga_ref[0]) + _dot_nt(dpx, wgx_ref[0])
        nxt = dxcar[...]
        dx_ref[...] = (cw_ref[3:4, :] * dxc + cw_ref[2:3, :] * _shift_up(dxc, nxt, 1)
                       + cw_ref[1:2, :] * _shift_up(dxc, nxt, 2) + cw_ref[0:1, :] * _shift_up(dxc, nxt, 3)).astype(BF16)
        dxcar[...] = dxc[0:8]
        colsum = lambda v: jnp.sum(v, axis=0, keepdims=True)
        parts = [(dcb_ref, colsum(dxc)), (dbga_ref, colsum(dpa)), (dbgx_ref, colsum(dpx)),
                 (dlam_ref, colsum(dla * LRU_C * r) * _sigmoid(-lam_ref[...]))]
        wparts = [colsum(dxc * xs[3 - k]) for k in range(4)]
        dwa, dwx = _dot_tn(xc, dpa), _dot_tn(xc, dpx)

        @pl.when(t == 0)
        def _():
            for ref, val in parts:
                ref[...] = val
            for k in range(4):
                dcw_ref[k:k + 1, :] = wparts[k]
            dwga_ref[0] = dwa
            dwgx_ref[0] = dwx

        @pl.when(t > 0)
        def _():
            for ref, val in parts:
                ref[...] += val
            for k in range(4):
                dcw_ref[k:k + 1, :] += wparts[k]
            dwga_ref[0] += dwa
            dwgx_ref[0] += dwx

    sd = jax.ShapeDtypeStruct
    return _pcall(
        body, name="lru_bwd",
        out_shape=(sd((s, w), BF16), sd((s, w), BF16), sd((4, w), F32), sd((1, w), F32), sd((1, w), F32),
                   sd((1, w), F32), sd((1, w), F32), sd(wga.shape, F32), sd(wgx.shape, F32)),
        grid=(nbk, nt),
        in_specs=[main(0), halo, main(nbk), main(0), halo, main(0), vec(4), vec(1), mat, vec(1), mat, vec(1), vec(1)],
        out_specs=(main(0), main(0), vec(4), vec(1), vec(1), vec(1), vec(1), mat, mat),
        scratch_shapes=[pltpu.VMEM((1, LRU_BLOCK_W), F32), pltpu.VMEM((8, LRU_BLOCK_W), F32)],
        compiler_params=_params(("parallel", "arbitrary")))(proj, proj, proj, hs, hs, dout, cw, cb, wga, bga, wgx, bgx, lam)


def _adamw(parts, w, m, v, layer=None, prev=None):
    shape = w.shape
    cols = shape[-1]
    layers = 1 if layer is None else shape[0]
    w3, m3, v3 = (a.reshape(layers, -1, cols) for a in (w, m, v))
    rows = w3.shape[1]
    tr = _tile(rows, max(16, (1 << 17) // cols // 16 * 16), 16)
    blk = pl.BlockSpec((None, tr, cols), lambda i: (layer or 0, i, 0))
    p_args, p_specs = [], []
    for part in parts:
        if isinstance(part, tuple):
            stack, idx = part
            p_args.append(stack.reshape(stack.shape[0], rows, cols))
            p_specs.append(pl.BlockSpec((None, tr, cols), lambda i, idx=idx: (idx, i, 0)))
        else:
            p_args.append(part.reshape(rows, cols))
            p_specs.append(pl.BlockSpec((tr, cols), lambda i: (i, 0)))
    n_parts = len(parts)
    n_prev = 0 if prev is None else 4

    def body(*refs):
        w_ref, m_ref, v_ref = refs[n_parts:n_parts + 3]
        g_ref, d_ref, nm_ref, nv_ref = refs[n_parts + 3 + n_prev:]
        gv = refs[0][...].astype(F32)
        for p_ref in refs[1:n_parts]:
            gv = gv + p_ref[...].astype(F32)
        g_ref[...] = gv
        nm = ADAM_B1 * m_ref[...] + (1.0 - ADAM_B1) * gv
        nv = ADAM_B2 * v_ref[...] + (1.0 - ADAM_B2) * (gv * gv)
        m_hat = nm / (1.0 - ADAM_B1 ** ADAM_STEP)
        v_hat = nv / (1.0 - ADAM_B2 ** ADAM_STEP)
        d_ref[...] = -ADAM_LR * (m_hat / (jnp.sqrt(v_hat) + ADAM_EPS) + ADAM_WD * w_ref[...])
        nm_ref[...] = nm
        nv_ref[...] = nv

    prev_args = [] if prev is None else [a.reshape(layers, rows, cols) for a in prev]
    n_in = n_parts + 3
    out = _pcall(body, name="adamw", out_shape=tuple(jax.ShapeDtypeStruct((layers, rows, cols), F32) for _ in range(4)),
                 grid=(rows // tr,), in_specs=p_specs + [blk] * 3 + [pl.BlockSpec(memory_space=pl.ANY)] * n_prev,
                 out_specs=(blk,) * 4, input_output_aliases={n_in + q: q for q in range(n_prev)},
                 compiler_params=_params(("parallel",)))(*p_args, w3, m3, v3, *prev_args)
    return tuple(o.reshape(shape) for o in out)


def _sum_list(arrs, out_dtype):
    shape = arrs[0].shape
    cols = shape[-1]
    flat = [a.reshape(-1, cols) for a in arrs]
    rows = flat[0].shape[0]
    tr = _tile(rows, max(16, (1 << 18) // cols // 16 * 16), 16)

    def body(*refs):
        acc = refs[0][...].astype(F32)
        for p_ref in refs[1:-1]:
            acc = acc + p_ref[...].astype(F32)
        refs[-1][...] = acc.astype(out_dtype)

    blk = pl.BlockSpec((tr, cols), lambda i: (i, 0))
    out = _pcall(body, name="sum_list", out_shape=jax.ShapeDtypeStruct((rows, cols), out_dtype), grid=(rows // tr,),
                 in_specs=[blk] * len(flat), out_specs=blk, compiler_params=_params(("parallel",)))(*flat)
    return out.reshape(shape)


def _mesh_pos():
    return lax.axis_index("x"), lax.axis_index("y"), lax.axis_index("c")


def _all_gather(arrs):
    n = len(arrs)

    def body(*refs):
        x_refs, out_refs = refs[:n], refs[n:2 * n]
        send_sems, recv_sems, local_sems = refs[2 * n:]
        x, y, c = _mesh_pos()
        me, sibling = (x, y, c), (x, y, 1 - c)
        chips = [(1 - x, y), (x, 1 - y), (1 - x, 1 - y)]

        def slot(a, px, py, pc):
            return out_refs[a].at[4 * px + 2 * py + pc]

        def copy(a, k, block, to, src=None):
            return pltpu.make_async_remote_copy(
                src_ref=slot(a, *block) if src is None else src, dst_ref=slot(a, *block),
                send_sem=send_sems.at[7 * a + k], recv_sem=recv_sems.at[7 * a + k], device_id=to, device_id_type=MESH)

        mine = [pltpu.make_async_copy(x_refs[a], slot(a, *me), local_sems.at[a]) for a in range(n)]
        for cp in mine:
            cp.start()
        first = []
        for j, chip in enumerate(chips):
            first += [copy(a, 1 + j, me, (*chip, c), src=x_refs[a]) for a in range(n)]
        first += [copy(a, 0, me, sibling, src=x_refs[a]) for a in range(n)]
        for cp in first:
            cp.start()
        passed = []
        for j, chip in enumerate(chips):
            for a in range(n):
                copy(a, 1 + j, (*chip, c), me).wait_recv()
                passed.append(copy(a, 4 + j, (*chip, c), sibling))
                passed[-1].start()
        for a in range(n):
            copy(a, 0, sibling, me).wait_recv()
        for j, chip in enumerate(chips):
            for a in range(n):
                copy(a, 4 + j, (*chip, 1 - c), me).wait_recv()
        for cp in first + passed:
            cp.wait_send()
        for cp in mine:
            cp.wait()

    hbm = pl.BlockSpec(memory_space=pl.ANY)
    return _pcall(body, name="all_gather",
                  out_shape=tuple(jax.ShapeDtypeStruct((N_DEV,) + a.shape, a.dtype) for a in arrs),
                  in_specs=[hbm] * n, out_specs=(hbm,) * n,
                  scratch_shapes=[pltpu.SemaphoreType.DMA((7 * n,)), pltpu.SemaphoreType.DMA((7 * n,)),
                                  pltpu.SemaphoreType.DMA((n,))],
                  compiler_params=pltpu.CompilerParams(has_side_effects=True))(*arrs)


def _peers():
    x, y, c = _mesh_pos()
    out = []
    for k in (1, 2, 4, 3, 5, 6, 7):
        px, py, pc = (x + (k >> 2)) % 2, (y + ((k >> 1) & 1)) % 2, (c + (k & 1)) % 2
        out.append((k, (px, py, pc), 4 * px + 2 * py + pc))
    return out, 4 * x + 2 * y + c


def _peer_copies(n, scatter, src_refs, land_refs, send_sems, recv_sems):
    peers, me = _peers()
    copies = []
    for k, peer, peer_id in peers:
        for a in range(n):
            copies.append(pltpu.make_async_remote_copy(
                src_ref=src_refs[a].at[peer_id] if scatter else src_refs[a],
                dst_ref=land_refs[a].at[k - 1] if scatter else land_refs[a].at[me],
                send_sem=send_sems.at[7 * a + k - 1], recv_sem=recv_sems.at[7 * a + k - 1],
                device_id=peer, device_id_type=MESH))
    return copies


_HBM = pl.BlockSpec(memory_space=pltpu.HBM)
_SEM = pl.BlockSpec(memory_space=pltpu.SEMAPHORE)
_EFFECT = pltpu.SideEffectType.DATAFLOW_SIDE_EFFECTING


def _in_hbm(a):
    return pltpu.with_memory_space_constraint(a, pltpu.HBM)


def _exchange_start(srcs, lands, scatter, name):
    n = len(srcs)

    def body(*refs):
        send_sems, recv_sems, token = refs[2 * n], refs[2 * n + 1], refs[-1]
        for cp in _peer_copies(n, scatter, refs[:n], refs[n:2 * n], send_sems, recv_sems):
            cp.start()
        token[...] = jnp.zeros_like(token)

    thru = [pltpu.HBM(a.shape, a.dtype) for a in list(srcs) + list(lands)]
    out = _pcall(body, name=name,
                 out_shape=(pltpu.SemaphoreType.DMA((7 * n,)), pltpu.SemaphoreType.DMA((7 * n,)), *thru,
                            jax.ShapeDtypeStruct((8, 128), F32)),
                 in_specs=[_HBM] * (2 * n), out_specs=(_SEM, _SEM, *([_HBM] * (2 * n)), pl.BlockSpec(memory_space=pltpu.VMEM)),
                 input_output_aliases={i: 2 + i for i in range(2 * n)},
                 compiler_params=pltpu.CompilerParams(has_side_effects=_EFFECT))(*[_in_hbm(a) for a in list(srcs) + list(lands)])
    return out[0], out[1], out[2:2 + n], out[2 + n:2 + 2 * n], out[-1]


def _exchange_wait(send_sems, recv_sems, srcs, lands, after, scatter, name):
    n = len(srcs)

    def body(*refs):
        for cp in _peer_copies(n, scatter, refs[:n], refs[n:2 * n], refs[2 * n], refs[2 * n + 1]):
            cp.wait_send()
            cp.wait_recv()

    thru = tuple(pltpu.HBM(a.shape, a.dtype) for a in list(srcs) + list(lands))
    out = _pcall(body, name=name, out_shape=thru,
                 in_specs=[_HBM] * (2 * n) + [_SEM, _SEM, pl.BlockSpec(memory_space=pl.ANY)], out_specs=(_HBM,) * (2 * n),
                 input_output_aliases={i: i for i in range(2 * n)},
                 compiler_params=pltpu.CompilerParams(has_side_effects=_EFFECT))(*srcs, *lands, send_sems, recv_sems, after)
    return out[:n], out[n:]


def _pack(arrays, dtype):
    lead = arrays[0].shape[:-1]
    flat = jnp.concatenate([a.astype(dtype) for a in arrays], axis=-1)
    n = flat.shape[-1]
    unit = 16 * COMM_LANES
    pad = (-n) % unit
    flat = jnp.pad(flat, [(0, 0)] * len(lead) + [(0, pad)])
    return flat.reshape(lead + ((n + pad) // COMM_LANES, COMM_LANES))


def _unpack(packed, shapes):
    lead = packed.shape[:-2]
    flat = packed.reshape(lead + (-1,))
    out, off = [], 0
    for shp in shapes:
        n = math.prod(shp)
        out.append(flat[..., off:off + n].reshape(lead + tuple(shp)))
        off += n
    return out


def _unshard(g, ax):
    shp = list(g.shape[1:])
    shp[ax] *= N_DEV
    return jnp.moveaxis(g, 0, ax).reshape(shp)


def _split(full, ax):
    shp = list(full.shape)
    r = full.reshape(shp[:ax] + [N_DEV, shp[ax] // N_DEV] + shp[ax + 1:])
    return jnp.moveaxis(r, ax, 0)


def _rope_tables(s, dk):
    half = dk // 2
    inv = ROPE_BASE ** (-jnp.arange(half, dtype=F32) / half)
    ang = jnp.arange(s, dtype=F32)[:, None] * inv[None, :]
    return jnp.cos(ang), jnp.sin(ang)


def _stage_big(s):
    i, ffn = divmod(s, 2)
    if ffn:
        return [('ffn_w_up', i), ('ffn_w_down', i)]
    kind, j = i % 3, i // 3
    mixer = [['gla_w_in', 'gla_w_out'], ['lru_w_in', 'lru_w_ga', 'lru_w_gx', 'lru_w_out'], ['ret_w_in', 'ret_w_out']][kind]
    return [(n, j) for n in mixer]


def _local_step(x, target, p, layer_weights, layer_grads, token):
    s, d = x.shape
    depth = p['norm_mix_w'].shape[0]
    gla_kd, gla_vd = d // 2, d
    ret_kd, ret_vd = d, 2 * d
    cos, sin = _rope_tables(s, ret_kd // RET_HEADS)
    row = lambda v: v.reshape(1, -1)
    saved = []
    cur, gb = {}, {}

    def view(name, layer=0, dev0=0, ndev=N_DEV, arr=None):
        return _View(cur[name] if arr is None else arr, 'col' if SHARD_AX[name] == 2 else 'row', 0, dev0, ndev)

    def dw_into(name, layer, a, b, label, dev0=0, ndev=N_DEV):
        gb[name] = _mm(a, b, 'tn', BF16, name=label, out=view(name, 0, dev0, ndev, gb[name]))

    def gla_weights(j):
        w_in = cur['gla_w_in']
        w_main, w_z = w_in[:, :2 * gla_kd + 2 * gla_vd], w_in[:, 2 * gla_kd + 2 * gla_vd:]
        w_z = jnp.pad(w_z, ((0, 0), (0, GATE_PAD - GLA_GATE_RANK)))
        w_gk = jnp.pad(p['gla_w_gk'][j], ((0, GATE_PAD - GLA_GATE_RANK), (0, 0)))
        return w_main, w_z, w_gk, row(p['gla_b_gk'][j]), row(p['gla_norm_w'][j]), view('gla_w_out', j)

    def lru_weights(j):
        return (view('lru_w_in', j), p['lru_conv_w'][j], row(p['lru_conv_b'][j]), cur['lru_w_ga'], row(p['lru_b_ga'][j]),
                cur['lru_w_gx'], row(p['lru_b_gx'][j]), row(p['lru_lambda'][j]), view('lru_w_out', j))

    for i in range(depth):
        kind, j = i % 3, i // 3
        cur = layer_weights(2 * i, x)
        h = _rms_fwd(x, row(p['norm_mix_w'][i]) + (token if i == 0 else 0.0))
        if kind == 0:
            w_main, w_z, w_gk, b_gk, nw, w_out = gla_weights(j)
            proj = _mm(h, w_main, 'nn', BF16, name="gla_in")
            z = _mm(h, w_z, 'nn', F32, name="gla_z")
            y, o_st, st = _chunk_fwd(True, proj, (z, w_gk, b_gk), nw, GLA_HEADS, gla_kd, gla_vd, 4)
            mix = (proj, z, o_st, st, y)
        elif kind == 1:
            w_in, cw, cb, wga, bga, wgx, bgx, lam, w_out = lru_weights(j)
            proj = _mm(h, w_in, 'nn', F32, name="lru_in")
            y, hs = _lru_fwd(proj, cw, cb, wga, bga, wgx, bgx, lam)
            mix = (proj, hs, y)
        else:
            nw, w_out = row(p['ret_norm_w'][j]), view('ret_w_out', j)
            proj = _mm(h, view('ret_w_in', j), 'nn', BF16, name="ret_in")
            y, o_st, st = _chunk_fwd(False, proj, (cos, sin), nw, RET_HEADS, ret_kd, ret_vd, 2)
            mix = (proj, o_st, st, y)
        x_mid = _mm(y, w_out, 'nn', F32, res=x, name="mix_out")
        cur.update(layer_weights(2 * i + 1, x_mid))
        h2 = _rms_fwd(x_mid, row(p['norm_ffn_w'][i]))
        act, gate, val, ug, uv = _ffn_up_act(h2, view('ffn_w_up', i), p['ffn_conv_w'][i], row(p['ffn_conv_b'][i]))
        x_out = _mm(act, view('ffn_w_down', i), 'nn', F32, res=x_mid, name="ffn_down")
        saved.append((x, h, mix, x_mid, h2, (ug, uv, gate, val), act, cur))
        x = x_out

    loss, dx, dxb, dw = _final_loss(x, row(p['norm_out_w']), target)
    g = {n: [None] * v.shape[0] for n, v in p.items() if n != 'norm_out_w'}
    g['norm_out_w'] = dw.reshape(-1)
    half = N_DEV // 2
    token = 0.0

    for i in reversed(range(depth)):
        kind, j = i % 3, i // 3
        x_in, h, mix, x_mid, h2, u, act, cur = saved[i]
        stage_names = [[n for n, _ in _stage_big(2 * i + part)] for part in (0, 1)]
        gb = {n: lax.empty(cur[n].shape, BF16) for n in stage_names[0] + stage_names[1] if n in GATHERED}
        dw_into('ffn_w_down', i, act, dxb, "ffn_down_dw")
        dug, duv, dcw, dcb = _ffn_down_dx_act(dxb, view('ffn_w_down', i), *u, p['ffn_conv_w'][i])
        g['ffn_conv_w'][i], g['ffn_conv_b'][i] = dcw, dcb.reshape(-1)
        dw_into('ffn_w_up', i, h2, dug, "ffn_up_dw", 0, half)
        dw_into('ffn_w_up', i, h2, duv, "ffn_up_dw", half, half)
        tok_ffn, dug, duv = lax.optimization_barrier(
            (layer_grads(2 * i + 1, {n: gb[n] for n in stage_names[1]}), dug, duv))
        dh2 = _mm(dug, view('ffn_w_up', i, 0, half), 'nt', BF16, name="ffn_up_dx",
                  pair2=(duv, view('ffn_w_up', i, half, half)))
        dx, dxb, dnw = _rms_bwd(x_mid, row(p['norm_ffn_w'][i]) + token, dh2, dx)
        g['norm_ffn_w'][i] = dnw.reshape(-1)
        token = tok_ffn

        def scatter_mixer(v):
            return lax.optimization_barrier((layer_grads(2 * i, {n: gb[n] for n in stage_names[0]}), v))

        if kind == 0:
            w_main, w_z, w_gk, b_gk, nw, w_out = gla_weights(j)
            proj, z, o_st, st, y = mix
            dw_into('gla_w_out', j, y, dxb, "mix_out_dw")
            dy = _mm(dxb, w_out, 'nt', BF16, name="mix_out_dx")
            dproj, dnw, dz, dwgk, dbgk = _chunk_bwd(True, proj, (z, w_gk, b_gk), nw, o_st, st, dy,
                                                    GLA_HEADS, gla_kd, gla_vd, 4)
            g['gla_norm_w'][j], g['gla_b_gk'][j] = dnw.reshape(-1), dbgk.reshape(-1)
            g['gla_w_gk'][j] = dwgk[:GLA_GATE_RANK]
            dw_main = _mm(h, dproj, 'tn', F32, name="gla_in_dw")
            dw_z = _mm(h, dz, 'tn', F32, name="gla_z_dw")
            dw_in = jnp.concatenate([dw_main, dw_z[:, :GLA_GATE_RANK]], axis=1)
            gb['gla_w_in'] = _split(dw_in, 1).astype(BF16)
            tok_mix, dproj = scatter_mixer(dproj)
            dh = _mm(dproj, w_main, 'nt', F32, name="gla_in_dx")
            dh = _mm(dz, w_z, 'nt', BF16, res=dh, name="gla_z_dx")
        elif kind == 1:
            w_in, cw, cb, wga, bga, wgx, bgx, lam, w_out = lru_weights(j)
            proj, hs, y = mix
            dw_into('lru_w_out', j, y, dxb, "mix_out_dw")
            dy = _mm(dxb, w_out, 'nt', BF16, name="mix_out_dx")
            dxb, dyb, dcw, dcb, dbga, dbgx, dlam, dwga, dwgx = _lru_bwd(proj, hs, dy, cw, cb, wga, bga, wgx, bgx, lam)
            g['lru_conv_w'][j], g['lru_conv_b'][j] = dcw, dcb.reshape(-1)
            g['lru_b_ga'][j], g['lru_b_gx'][j], g['lru_lambda'][j] = dbga.reshape(-1), dbgx.reshape(-1), dlam.reshape(-1)
            gb['lru_w_ga'], gb['lru_w_gx'] = _split(dwga, 1).astype(BF16), _split(dwgx, 1).astype(BF16)
            dproj = jnp.concatenate([dxb, dyb], axis=1)
            dw_into('lru_w_in', j, h, dproj, "lru_in_dw")
            tok_mix, dproj = scatter_mixer(dproj)
            dh = _mm(dproj, w_in, 'nt', BF16, name="lru_in_dx")
        else:
            nw, w_out = row(p['ret_norm_w'][j]), view('ret_w_out', j)
            proj, o_st, st, y = mix
            dw_into('ret_w_out', j, y, dxb, "mix_out_dw")
            dy = _mm(dxb, w_out, 'nt', BF16, name="mix_out_dx")
            dproj, dnw = _chunk_bwd(False, proj, (cos, sin), nw, o_st, st, dy, RET_HEADS, ret_kd, ret_vd, 2)
            g['ret_norm_w'][j] = dnw.reshape(-1)
            dw_into('ret_w_in', j, h, dproj, "ret_in_dw")
            tok_mix, dproj = scatter_mixer(dproj)
            dh = _mm(dproj, view('ret_w_in', j), 'nt', BF16, name="ret_in_dx")
        dx, dxb, dnw = _rms_bwd(x_in, row(p['norm_mix_w'][i]) + token, dh, dx)
        g['norm_mix_w'][i] = dnw.reshape(-1)
        token = tok_mix

    grads = {n: (v if n == 'norm_out_w' else jnp.stack(v)) for n, v in g.items()}
    return loss[0, 0], dx, grads, token


def kernel(x, norm_mix_w, norm_ffn_w, norm_out_w, gla_w_in, gla_w_gk, gla_b_gk, gla_norm_w, gla_w_out, lru_w_in, lru_conv_w, lru_conv_b, lru_w_ga, lru_b_ga, lru_w_gx, lru_b_gx, lru_lambda, lru_w_out, ret_w_in, ret_norm_w, ret_w_out, ffn_w_up, ffn_conv_w, ffn_conv_b, ffn_w_down, loss_target, m_norm_mix_w, m_norm_ffn_w, m_norm_out_w, m_gla_w_in, m_gla_w_gk, m_gla_b_gk, m_gla_norm_w, m_gla_w_out, m_lru_w_in, m_lru_conv_w, m_lru_conv_b, m_lru_w_ga, m_lru_b_ga, m_lru_w_gx, m_lru_b_gx, m_lru_lambda, m_lru_w_out, m_ret_w_in, m_ret_norm_w, m_ret_w_out, m_ffn_w_up, m_ffn_conv_w, m_ffn_conv_b, m_ffn_w_down, v_norm_mix_w, v_norm_ffn_w, v_norm_out_w, v_gla_w_in, v_gla_w_gk, v_gla_b_gk, v_gla_norm_w, v_gla_w_out, v_lru_w_in, v_lru_conv_w, v_lru_conv_b, v_lru_w_ga, v_lru_b_ga, v_lru_w_gx, v_lru_b_gx, v_lru_lambda, v_lru_w_out, v_ret_w_in, v_ret_norm_w, v_ret_w_out, v_ffn_w_up, v_ffn_conv_w, v_ffn_conv_b, v_ffn_w_down):
    given = dict(locals())
    w = {n: given[n] for n in WEIGHTS}
    me_x, me_y, me_c = _mesh_pos()
    me = 4 * me_x + 2 * me_y + me_c

    n_stages = 2 * norm_mix_w.shape[0]

    def shards(s):
        return [w[n][j].astype(BF16) for n, j in _stage_big(s)]

    got0 = _all_gather(shards(0) + [_pack([w[n].reshape(-1) for n in SMALL_SHARDED], F32)])
    p = {n: w[n] for n in REPLICATED}
    for n, blk in zip(SMALL_SHARDED, _unpack(got0[-1], [w[n].shape for n in SMALL_SHARDED])):
        p[n] = _unshard(blk, SHARD_AX[n])
    gathers, token = {}, 0.0
    for s in range(1, n_stages):
        _, srcs = lax.optimization_barrier((got0[-1], shards(s)))
        lands =[lax.dynamic_update_index_in_dim(lax.empty((N_DEV,) + a.shape, BF16), a, me, 0) for a in srcs]
        *gathers[s], tok = _exchange_start(srcs, lands, False, "gather_start_%d" % s)
        token = token + tok[0, 0]

    def stage_weights(s, after):
        blocks = got0[:-1] if s == 0 else _exchange_wait(*gathers[s], after, False, "gather_wait_%d" % s)[1]
        out = {}
        for (n, _), blk in zip(_stage_big(s), blocks):
            out[n] = blk.reshape((N_DEV, 1) + blk.shape[1:]) if n in GATHERED else _unshard(blk, SHARD_AX[n] - 1)
        return out

    scatters = {}

    def stage_grads(s, gb):
        srcs = [gb[n].reshape((N_DEV,) + w[n].shape[1:]) for n, _ in _stage_big(s)]
        lands = [lax.empty((N_DEV - 1,) + a.shape[1:], BF16) for a in srcs]
        *scatters[s], tok = _exchange_start(srcs, lands, True, "scatter_start_%d" % s)
        return tok[0, 0]

    loss, grad_x, grads, token = _local_step(x[0], loss_target[0], p, stage_weights, stage_grads, token)
    loss = lax.psum(loss, ("x", "y", "c"))
    gw, delta, new_m, new_v = {}, {}, {}, {}

    after, big = grads['norm_out_w'] + token, {}
    for s in reversed(range(n_stages)):
        srcs, lands = _exchange_wait(*scatters[s], after, True, "scatter_wait_%d" % s)
        for (n, j), src, land in zip(_stage_big(s), srcs, lands):
            own = lax.dynamic_index_in_dim(src, me, 0, keepdims=False)
            big[n] = _adamw([own] + [(land, k) for k in range(N_DEV - 1)], w[n], given["m_" + n], given["v_" + n],
                            layer=j, prev=big.get(n))
            after = big[n][0]
    for n in BIG:
        gw[n], delta[n], new_m[n], new_v[n] = big[n]

    small = REPLICATED + SMALL_SHARDED
    _, packed = lax.optimization_barrier((after, _pack([grads[n].reshape(-1) for n in small], F32)))
    (parts,) = _all_gather([packed])
    summed = _unpack(_sum_list([parts[dev] for dev in range(N_DEV)], F32), [grads[n].shape for n in small])
    for n, gs in zip(small, summed):
        if SHARD_AX[n] is not None:
            gs = lax.dynamic_index_in_dim(_split(gs, SHARD_AX[n]), me, 0, keepdims=False)
        gw[n], delta[n], new_m[n], new_v[n] = _adamw([gs], w[n], given["m_" + n], given["v_" + n])

    return (loss, grad_x[None], *[gw[n] for n in WEIGHTS], *[delta[n] for n in WEIGHTS],
            *[new_m[n] for n in WEIGHTS], *[new_v[n] for n in WEIGHTS])
```
